```python
import math
import jax, jax.numpy as jnp
from jax import lax
import numpy as np

D_MODEL = 1024
BATCH = 2
SEQ = 8192
DEPTH = 1

CHUNK = 64
D_MIX = D_MODEL
SB_HEADS = 8
SB_HEAD_DIM = 64
D_SB = SB_HEADS * SB_HEAD_DIM
CONV_GROUPS = 8
D_CONV = D_MIX - D_SB
CONV_WIDTH = 3
D_FF = 2816
Q_BLOCK = 128
EPS = 1e-6
D_IN = 3 * D_SB + 3 * D_CONV

kernel_name = "hybrid_stickbreaking_shortconv_macaron"


def rmsnorm(x, g):
    xf = x.astype(jnp.float32)
    y = xf * lax.rsqrt(jnp.mean(xf * xf, axis=-1, keepdims=True) + EPS)
    return (y * g.astype(jnp.float32)).astype(x.dtype)


def swiglu(h, w_gate, w_up, w_down):
    return (jax.nn.silu(h @ w_gate) * (h @ w_up)) @ w_down


def stick_breaking_attention(q, k, v):
    b, h, s, d = q.shape
    n_blk = s // Q_BLOCK
    scale = 1.0 / math.sqrt(d)
    kf = k.astype(jnp.float32)
    vf = v.astype(jnp.float32)
    key_pos = jnp.arange(s)
    q_blocks = q.reshape(b, h, n_blk, Q_BLOCK, d).transpose(2, 0, 1, 3, 4)

    def one_block(args):
        q_blk, blk = args
        z = jnp.einsum('bhqd,bhkd->bhqk', q_blk.astype(jnp.float32), kf) * scale
        q_pos = blk * Q_BLOCK + jnp.arange(Q_BLOCK)
        strict = key_pos[None, :] < q_pos[:, None]
        log_beta = jax.nn.log_sigmoid(z)
        log_keep = jnp.where(strict, log_beta - z, 0.0)
        later = lax.cumsum(log_keep, axis=3, reverse=True) - log_keep
        a = jnp.where(strict, jnp.exp(log_beta + later), 0.0)
        return jnp.einsum('bhqk,bhkd->bhqd', a, vf)

    out = lax.map(one_block, (q_blocks, jnp.arange(n_blk)))
    return out.transpose(1, 2, 0, 3, 4).reshape(b, h, s, d).astype(q.dtype)


def short_conv_mixer(gate_b, gate_c, u, conv_w, conv_b):
    s = u.shape[1]
    xc = gate_c * u
    pad = jnp.pad(xc, ((0, 0), (CONV_WIDTH - 1, 0), (0, 0)))
    y = conv_b
    for i in range(CONV_WIDTH):
        y = y + pad[:, i:i + s, :] * conv_w[i]
    return gate_b * y


def hybrid_mixer(h, w_in, conv_w, conv_b, sb_out_norm, conv_out_norm, w_out):
    b, s, _ = h.shape
    proj = h @ w_in
    q, k, v, gate_b, gate_c, u = jnp.split(
        proj, [D_SB, 2 * D_SB, 3 * D_SB, 3 * D_SB + D_CONV, 3 * D_SB + 2 * D_CONV], axis=-1)

    def heads(t):
        return t.reshape(b, s, SB_HEADS, SB_HEAD_DIM).transpose(0, 2, 1, 3)

    y_sb = stick_breaking_attention(heads(q), heads(k), heads(v))
    y_sb = y_sb.transpose(0, 2, 1, 3).reshape(b, s, D_SB)
    y_conv = short_conv_mixer(gate_b, gate_c, u, conv_w, conv_b)
    y = jnp.concatenate([rmsnorm(y_sb, sb_out_norm), rmsnorm(y_conv, conv_out_norm)], axis=-1)
    return y @ w_out


def setup_inputs(seed: int = 0) -> dict:
    key = jax.random.key(seed)
    ks = jax.random.split(key, 20)

    def w(k, shape, fan_in):
        return jax.random.normal(k, shape, jnp.float32) * fan_in ** -0.5

    def gain(k, shape):
        return 1.0 + 0.02 * jax.random.normal(k, shape, jnp.float32)

    return {
        "x": jax.random.normal(ks[0], (BATCH, SEQ, D_MODEL), jnp.float32),
        "ffn1_norm": gain(ks[1], (DEPTH, D_MODEL)),
        "ffn1_w_gate": w(ks[2], (DEPTH, D_MODEL, D_FF), D_MODEL),
        "ffn1_w_up": w(ks[3], (DEPTH, D_MODEL, D_FF), D_MODEL),
        "ffn1_w_down": w(ks[4], (DEPTH, D_FF, D_MODEL), D_FF),
        "mix_norm": gain(ks[5], (DEPTH, D_MODEL)),
        "w_in": w(ks[6], (DEPTH, D_MODEL, D_IN), D_MODEL),
        "conv_w": w(ks[7], (DEPTH, CONV_WIDTH, D_CONV), CONV_WIDTH),
        "conv_b": 0.01 * jax.random.normal(ks[8], (DEPTH, D_CONV), jnp.float32),
        "sb_out_norm": gain(ks[9], (DEPTH, D_SB)),
        "conv_out_norm": gain(ks[10], (DEPTH, D_CONV)),
        "w_out": w(ks[11], (DEPTH, D_MIX, D_MODEL), D_MIX),
        "ffn2_norm": gain(ks[12], (DEPTH, D_MODEL)),
        "ffn2_w_gate": w(ks[13], (DEPTH, D_MODEL, D_FF), D_MODEL),
        "ffn2_w_up": w(ks[14], (DEPTH, D_MODEL, D_FF), D_MODEL),
        "ffn2_w_down": w(ks[15], (DEPTH, D_FF, D_MODEL), D_FF),
        "final_norm": gain(ks[16], (D_MODEL,)),
    }


def reference(x, ffn1_norm, ffn1_w_gate, ffn1_w_up, ffn1_w_down, mix_norm, w_in, conv_w, conv_b,
              sb_out_norm, conv_out_norm, w_out, ffn2_norm, ffn2_w_gate, ffn2_w_up, ffn2_w_down,
              final_norm):
    for l in range(DEPTH):
        x = x + 0.5 * swiglu(rmsnorm(x, ffn1_norm[l]), ffn1_w_gate[l], ffn1_w_up[l], ffn1_w_down[l])
        x = x + hybrid_mixer(rmsnorm(x, mix_norm[l]), w_in[l], conv_w[l], conv_b[l],
                             sb_out_norm[l], conv_out_norm[l], w_out[l])
        x = x + 0.5 * swiglu(rmsnorm(x, ffn2_norm[l]), ffn2_w_gate[l], ffn2_w_up[l], ffn2_w_down[l])
    return rmsnorm(x, final_norm)
```

```python
import functools
import math

import jax
import jax.numpy as jnp
from jax import lax
from jax.experimental import pallas as pl
from jax.experimental.pallas import tpu as pltpu

F32 = jnp.float32
BF16 = jnp.bfloat16

EPS = 1e-6
SB_HEADS = 8
SB_HEAD_DIM = 64
D_SB = SB_HEADS * SB_HEAD_DIM
CONV_WIDTH = 3
LANES = 128
N_PAIRS = D_SB // LANES
ROW_TILE = 256
ATT_BLOCK = 128
CONV_HALO = 8
LOG_WEIGHT_UNDERFLOW = -104.0
VMEM_LIMIT = 56 * 1024 * 1024


def _rms(x, g):
    return x * lax.rsqrt(jnp.mean(x * x, axis=-1, keepdims=True) + EPS) * g


def _swiglu_half_step(x, g, wg_ref, wu_ref, wd_ref):
    h = _rms(x, g).astype(BF16)
    gate = jnp.dot(h, wg_ref[...], preferred_element_type=F32)
    up = jnp.dot(h, wu_ref[...], preferred_element_type=F32)
    act = (gate * jax.nn.sigmoid(gate) * up).astype(BF16)
    return x + 0.5 * jnp.dot(act, wd_ref[...], preferred_element_type=F32)


def _pre_kernel(x_ref, n1_ref, wg_ref, wu_ref, wd_ref, nm_ref, win_ref, cw_ref, cb_ref, cn_ref,
                x1_ref, q_ref, k_ref, v_ref, yc_ref, xc_buf, *, tiles_per_seq, q_scale):
    tm = x_ref.shape[0]
    i = pl.program_id(0)
    x1 = _swiglu_half_step(x_ref[...], n1_ref[...], wg_ref, wu_ref, wd_ref)
    x1_ref[...] = x1
    h = _rms(x1, nm_ref[...]).astype(BF16)
    proj = jnp.dot(h, win_ref[...], preferred_element_type=F32)
    d = D_SB
    q_ref[...] = (proj[:, 0:d] * q_scale).astype(BF16)
    k_ref[...] = proj[:, d:2 * d].astype(BF16)
    v_ref[...] = proj[:, 2 * d:3 * d].astype(BF16)
    gate_b = proj[:, 3 * d:4 * d]
    xc = proj[:, 4 * d:5 * d] * proj[:, 5 * d:6 * d]

    first = i % tiles_per_seq == 0

    @pl.when(first)
    def _():
        xc_buf[0:CONV_HALO, :] = jnp.zeros((CONV_HALO, xc_buf.shape[1]), F32)

    @pl.when(jnp.logical_not(first))
    def _():
        xc_buf[0:CONV_HALO, :] = xc_buf[tm:tm + CONV_HALO, :]

    xc_buf[CONV_HALO:CONV_HALO + tm, :] = xc
    y = cb_ref[...]
    for j in range(CONV_WIDTH):
        off = CONV_HALO - (CONV_WIDTH - 1) + j
        y = y + xc_buf[off:off + tm, :] * cw_ref[j:j + 1, :]
    yc_ref[...] = _rms(gate_b * y, cn_ref[...]).astype(BF16)


def _attn_kernel(q_ref, k_ref, v_ref, tri_ref, o_ref, qs_ref, carry_ref, acc_ref):
    tq = q_ref.shape[0]
    i = pl.program_id(1)
    q0 = pl.multiple_of(i * tq, tq)
    lane = lax.broadcasted_iota(jnp.int32, (tq, LANES), 1)
    low = lane < SB_HEAD_DIM
    row = lax.broadcasted_iota(jnp.int32, (2 * tq, tq), 0)
    col = lax.broadcasted_iota(jnp.int32, (2 * tq, tq), 1)
    strict = col < (row & (tq - 1))
    tri = tri_ref[...]
    zero = jnp.zeros((), BF16)

    for p in range(N_PAIRS):
        q2 = q_ref[:, p * LANES:(p + 1) * LANES]
        qs_ref[p] = jnp.concatenate([jnp.where(low, q2, zero), jnp.where(low, zero, q2)], axis=0)

    def pair_block(p, kstart, diagonal):
        ls = slice(p * LANES, (p + 1) * LANES)
        kb = k_ref[pl.ds(kstart, tq), ls]
        vb = v_ref[pl.ds(kstart, tq), ls]
        z = lax.dot_general(qs_ref[p], kb, (((1,), (1,)), ((), ())), preferred_element_type=F32)
        sp = jnp.maximum(z, 0.0) + jnp.log1p(jnp.exp(-jnp.abs(z)))
        lk = -sp
        if diagonal:
            lk = jnp.where(strict, lk, 0.0)
        hi = lk.astype(BF16)
        lo = (lk - hi.astype(F32)).astype(BF16)
        sums = jnp.dot(jnp.concatenate([hi, lo], axis=1), tri, preferred_element_type=F32)
        later = sums[:, 0:tq]
        total = sums[:, tq:2 * tq]
        if not diagonal:
            carry = carry_ref[p]
            later = later + carry
            total = total + carry
        a = jnp.exp((z - sp) + later)
        if diagonal:
            a = jnp.where(strict, a, 0.0)
        ab = a.astype(BF16)
        a2 = jnp.concatenate([ab[0:tq], ab[tq:2 * tq]], axis=1)
        v2 = jnp.concatenate([jnp.where(low, vb, zero), jnp.where(low, zero, vb)], axis=0)
        out = jnp.dot(a2, v2, preferred_element_type=F32)
        carry_ref[p] = total
        if diagonal:
            acc_ref[p] = out
        else:
            acc_ref[p] = acc_ref[p] + out
        return jnp.max(total)

    m = pair_block(0, q0, True)
    for p in range(1, N_PAIRS):
        m = jnp.maximum(m, pair_block(p, q0, True))

    def cond(state):
        kb, m = state
        return jnp.logical_and(kb >= 0, m > LOG_WEIGHT_UNDERFLOW)

    def body(state):
        kb, _ = state
        kstart = pl.multiple_of(kb * tq, tq)
        m = pair_block(0, kstart, False)
        for p in range(1, N_PAIRS):
            m = jnp.maximum(m, pair_block(p, kstart, False))
        return kb - 1, m

    lax.while_loop(cond, body, (i - 1, m))

    for p in range(N_PAIRS):
        o_ref[:, p * LANES:(p + 1) * LANES] = acc_ref[p]


def _post_kernel(x1_ref, ysb_ref, yc_ref, sbn_ref, wo_ref, n2_ref, wg_ref, wu_ref, wd_ref, fn_ref,
                 out_ref, *, final):
    ysn = _rms(ysb_ref[...], sbn_ref[...]).astype(BF16)
    y = jnp.concatenate([ysn, yc_ref[...]], axis=1)
    x2 = x1_ref[...] + jnp.dot(y, wo_ref[...], preferred_element_type=F32)
    x3 = _swiglu_half_step(x2, n2_ref[...], wg_ref, wu_ref, wd_ref)
    out_ref[...] = _rms(x3, fn_ref[...]) if final else x3


def _resident(shape):
    return pl.BlockSpec(shape, lambda *_: (0,) * len(shape), pipeline_mode=pl.Buffered(1))


def _rows(tm, width):
    return pl.BlockSpec((tm, width), lambda i: (i, 0))


def _pre_call(x, n1, wg, wu, wd, nm, win, cw, cb, cn, *, seq):
    n, d = x.shape
    tm = ROW_TILE
    d_ff = wg.shape[1]
    d_conv = cw.shape[1]
    kern = functools.partial(_pre_kernel, tiles_per_seq=seq // tm, q_scale=1.0 / math.sqrt(SB_HEAD_DIM))
    return pl.pallas_call(
        kern,
        grid=(n // tm,),
        in_specs=[_rows(tm, d), _resident((1, d)), _resident((d, d_ff)), _resident((d, d_ff)),
                  _resident((d_ff, d)), _resident((1, d)), _resident(win.shape),
                  _resident(cw.shape), _resident((1, d_conv)), _resident((1, d_conv))],
        out_specs=[_rows(tm, d), _rows(tm, D_SB), _rows(tm, D_SB), _rows(tm, D_SB), _rows(tm, d_conv)],
        out_shape=[jax.ShapeDtypeStruct((n, d), F32)] + [jax.ShapeDtypeStruct((n, D_SB), BF16)] * 3
                  + [jax.ShapeDtypeStruct((n, d_conv), BF16)],
        scratch_shapes=[pltpu.VMEM((tm + 2 * CONV_HALO, d_conv), F32)],
        compiler_params=pltpu.CompilerParams(dimension_semantics=("arbitrary",),
                                             vmem_limit_bytes=VMEM_LIMIT),
        name="ffn1_proj_conv",
    )(x, n1, wg, wu, wd, nm, win, cw, cb, cn)


def _suffix_sum_matrix(tk):
    j = jnp.arange(2 * tk)[:, None] % tk
    s = jnp.arange(2 * tk)[None, :]
    return jnp.where(s < tk, j > s, True).astype(BF16)


def _attn_call(q, k, v, *, batch, seq):
    n, d = q.shape
    tq = ATT_BLOCK
    blocks = seq // tq
    return pl.pallas_call(
        _attn_kernel,
        grid=(batch, blocks),
        in_specs=[pl.BlockSpec((tq, d), lambda b, i: (b * blocks + i, 0)),
                  pl.BlockSpec((seq, d), lambda b, i: (b, 0)),
                  pl.BlockSpec((seq, d), lambda b, i: (b, 0)),
                  _resident((2 * tq, 2 * tq))],
        out_specs=pl.BlockSpec((tq, d), lambda b, i: (b * blocks + i, 0)),
        out_shape=jax.ShapeDtypeStruct((n, d), F32),
        scratch_shapes=[pltpu.VMEM((N_PAIRS, 2 * tq, LANES), BF16),
                        pltpu.VMEM((N_PAIRS, 2 * tq, tq), F32),
                        pltpu.VMEM((N_PAIRS, tq, LANES), F32)],
        compiler_params=pltpu.CompilerParams(dimension_semantics=("arbitrary", "arbitrary"),
                                             vmem_limit_bytes=VMEM_LIMIT),
        name="stickbreaking_attention",
    )(q, k, v, _suffix_sum_matrix(tq))


def _post_call(x1, ysb, yc, sbn, wo, n2, wg, wu, wd, fn, *, final):
    n, d = x1.shape
    tm = ROW_TILE
    d_ff = wg.shape[1]
    return pl.pallas_call(
        functools.partial(_post_kernel, final=final),
        grid=(n // tm,),
        in_specs=[_rows(tm, d), _rows(tm, ysb.shape[1]), _rows(tm, yc.shape[1]),
                  _resident((1, ysb.shape[1])), _resident(wo.shape), _resident((1, d)),
                  _resident((d, d_ff)), _resident((d, d_ff)), _resident((d_ff, d)), _resident((1, d))],
        out_specs=_rows(tm, d),
        out_shape=jax.ShapeDtypeStruct((n, d), F32),
        compiler_params=pltpu.CompilerParams(dimension_semantics=("arbitrary",),
                                             vmem_limit_bytes=VMEM_LIMIT),
        name="out_proj_ffn2",
    )(x1, ysb, yc, sbn, wo, n2, wg, wu, wd, fn)


def kernel(x, ffn1_norm, ffn1_w_gate, ffn1_w_up, ffn1_w_down, mix_norm, w_in, conv_w, conv_b,
           sb_out_norm, conv_out_norm, w_out, ffn2_norm, ffn2_w_gate, ffn2_w_up, ffn2_w_down,
           final_norm):
    batch, seq, d = x.shape
    depth = ffn1_norm.shape[0]
    assert seq % ROW_TILE == 0 and seq % ATT_BLOCK == 0 and w_in.shape[2] == 6 * D_SB
    row = lambda t: t.reshape(1, -1)
    xs = x.reshape(batch * seq, d)
    for l in range(depth):
        x1, q, k, v, yc = _pre_call(
            xs, row(ffn1_norm[l]), ffn1_w_gate[l].astype(BF16), ffn1_w_up[l].astype(BF16),
            ffn1_w_down[l].astype(BF16), row(mix_norm[l]), w_in[l].astype(BF16), conv_w[l],
            row(conv_b[l]), row(conv_out_norm[l]), seq=seq)
        ysb = _attn_call(q, k, v, batch=batch, seq=seq)
        xs = _post_call(
            x1, ysb, yc, row(sb_out_norm[l]), w_out[l].astype(BF16), row(ffn2_norm[l]),
            ffn2_w_gate[l].astype(BF16), ffn2_w_up[l].astype(BF16), ffn2_w_down[l].astype(BF16),
            row(final_norm), final=(l == depth - 1))
    return xs.reshape(batch, seq, d)
```

```python
import functools
import math

import jax
import jax.numpy as jnp
from jax import lax
from jax.experimental import pallas as pl
from jax.experimental.pallas import tpu as pltpu

F32 = jnp.float32
BF16 = jnp.bfloat16

EPS = 1e-6
SB_HEADS = 8
SB_HEAD_DIM = 64
D_SB = SB_HEADS * SB_HEAD_DIM
CONV_WIDTH = 3
LANES = 128
N_PAIRS = D_SB // LANES
ROW_TILE = 256
ATT_BLOCK = 128
CONV_HALO = 8
LOG_WEIGHT_UNDERFLOW = -104.0
VMEM_LIMIT = 56 * 1024 * 1024


def _rms(x, g):
    return x * lax.rsqrt(jnp.mean(x * x, axis=-1, keepdims=True) + EPS) * g


def _swiglu_half_step(x, g, wg_ref, wu_ref, wd_ref):
    h = _rms(x, g).astype(BF16)
    gate = jnp.dot(h, wg_ref[...], preferred_element_type=F32)
    up = jnp.dot(h, wu_ref[...], preferred_element_type=F32)
    act = (gate * jax.nn.sigmoid(gate) * up).astype(BF16)
    return x + 0.5 * jnp.dot(act, wd_ref[...], preferred_element_type=F32)


def _pre_kernel(x_ref, n1_ref, wg_ref, wu_ref, wd_ref, nm_ref, win_ref, cw_ref, cb_ref, cn_ref,
                x1_ref, q_ref, k_ref, v_ref, yc_ref, xc_buf, *, tiles_per_seq, q_scale):
    tm = x_ref.shape[0]
    i = pl.program_id(0)
    x1 = _swiglu_half_step(x_ref[...], n1_ref[...], wg_ref, wu_ref, wd_ref)
    x1_ref[...] = x1
    h = _rms(x1, nm_ref[...]).astype(BF16)
    proj = jnp.dot(h, win_ref[...], preferred_element_type=F32)
    d = D_SB
    q_ref[...] = (proj[:, 0:d] * q_scale).astype(BF16)
    k_ref[...] = proj[:, d:2 * d].astype(BF16)
    v_ref[...] = proj[:, 2 * d:3 * d].astype(BF16)
    gate_b = proj[:, 3 * d:4 * d]
    xc = proj[:, 4 * d:5 * d] * proj[:, 5 * d:6 * d]

    first = i % tiles_per_seq == 0

    @pl.when(first)
    def _():
        xc_buf[0:CONV_HALO, :] = jnp.zeros((CONV_HALO, xc_buf.shape[1]), F32)

    @pl.when(jnp.logical_not(first))
    def _():
        xc_buf[0:CONV_HALO, :] = xc_buf[tm:tm + CONV_HALO, :]

    xc_buf[CONV_HALO:CONV_HALO + tm, :] = xc
    y = cb_ref[...]
    for j in range(CONV_WIDTH):
        off = CONV_HALO - (CONV_WIDTH - 1) + j
        y = y + xc_buf[off:off + tm, :] * cw_ref[j:j + 1, :]
    yc_ref[...] = _rms(gate_b * y, cn_ref[...]).astype(BF16)


def _attn_kernel(q_ref, k_ref, v_ref, tri_ref, o_ref, qs_ref, carry_ref, acc_ref):
    tq = q_ref.shape[0]
    rows = N_PAIRS * 2 * tq
    i = pl.program_id(1)
    q0 = pl.multiple_of(i * tq, tq)
    lane = lax.broadcasted_iota(jnp.int32, (tq, LANES), 1)
    low = lane < SB_HEAD_DIM
    row = lax.broadcasted_iota(jnp.int32, (rows, tq), 0)
    col = lax.broadcasted_iota(jnp.int32, (rows, tq), 1)
    strict = col < (row & (tq - 1))
    tri = tri_ref[...]
    zero = jnp.zeros((), BF16)

    for p in range(N_PAIRS):
        q2 = q_ref[:, p * LANES:(p + 1) * LANES]
        qs_ref[p] = jnp.concatenate([jnp.where(low, q2, zero), jnp.where(low, zero, q2)], axis=0)

    def key_block(kstart, diagonal):
        z = jnp.concatenate(
            [lax.dot_general(qs_ref[p], k_ref[pl.ds(kstart, tq), p * LANES:(p + 1) * LANES],
                             (((1,), (1,)), ((), ())), preferred_element_type=F32)
             for p in range(N_PAIRS)], axis=0)
        sp = jnp.maximum(z, 0.0) + jnp.log(1.0 + jnp.exp(-jnp.abs(z)))
        lk = -sp
        if diagonal:
            lk = jnp.where(strict, lk, 0.0)
        hi = lk.astype(BF16)
        lo = (lk - hi.astype(F32)).astype(BF16)
        sums = jnp.dot(jnp.concatenate([hi, lo], axis=1), tri, preferred_element_type=F32)
        later = sums[:, 0:tq]
        total = sums[:, tq:2 * tq]
        if not diagonal:
            carry = carry_ref[...]
            later = later + carry
            total = total + carry
        a = jnp.exp((z - sp) + later)
        if diagonal:
            a = jnp.where(strict, a, 0.0)
        ab = a.astype(BF16)
        carry_ref[...] = total
        for p in range(N_PAIRS):
            r0 = p * 2 * tq
            vb = v_ref[pl.ds(kstart, tq), p * LANES:(p + 1) * LANES]
            a2 = jnp.concatenate([ab[r0:r0 + tq], ab[r0 + tq:r0 + 2 * tq]], axis=1)
            v2 = jnp.concatenate([jnp.where(low, vb, zero), jnp.where(low, zero, vb)], axis=0)
            out = jnp.dot(a2, v2, preferred_element_type=F32)
            acc_ref[p] = out if diagonal else acc_ref[p] + out
        return jnp.max(total)

    def cond(state):
        kb, m = state
        return jnp.logical_and(kb >= 0, m > LOG_WEIGHT_UNDERFLOW)

    def body(state):
        kb, _ = state
        return kb - 1, key_block(pl.multiple_of(kb * tq, tq), False)

    lax.while_loop(cond, body, (i - 1, key_block(q0, True)))

    for p in range(N_PAIRS):
        o_ref[:, p * LANES:(p + 1) * LANES] = acc_ref[p]


def _post_kernel(x1_ref, ysb_ref, yc_ref, sbn_ref, wo_ref, n2_ref, wg_ref, wu_ref, wd_ref, fn_ref,
                 out_ref, *, final):
    ysn = _rms(ysb_ref[...], sbn_ref[...]).astype(BF16)
    y = jnp.concatenate([ysn, yc_ref[...]], axis=1)
    x2 = x1_ref[...] + jnp.dot(y, wo_ref[...], preferred_element_type=F32)
    x3 = _swiglu_half_step(x2, n2_ref[...], wg_ref, wu_ref, wd_ref)
    out_ref[...] = _rms(x3, fn_ref[...]) if final else x3


def _resident(shape):
    return pl.BlockSpec(shape, lambda *_: (0,) * len(shape), pipeline_mode=pl.Buffered(1))


def _rows(tm, width):
    return pl.BlockSpec((tm, width), lambda i: (i, 0))


def _pre_call(x, n1, wg, wu, wd, nm, win, cw, cb, cn, *, seq):
    n, d = x.shape
    tm = ROW_TILE
    d_ff = wg.shape[1]
    d_conv = cw.shape[1]
    kern = functools.partial(_pre_kernel, tiles_per_seq=seq // tm, q_scale=1.0 / math.sqrt(SB_HEAD_DIM))
    return pl.pallas_call(
        kern,
        grid=(n // tm,),
        in_specs=[_rows(tm, d), _resident((1, d)), _resident((d, d_ff)), _resident((d, d_ff)),
                  _resident((d_ff, d)), _resident((1, d)), _resident(win.shape),
                  _resident(cw.shape), _resident((1, d_conv)), _resident((1, d_conv))],
        out_specs=[_rows(tm, d), _rows(tm, D_SB), _rows(tm, D_SB), _rows(tm, D_SB), _rows(tm, d_conv)],
        out_shape=[jax.ShapeDtypeStruct((n, d), F32)] + [jax.ShapeDtypeStruct((n, D_SB), BF16)] * 3
                  + [jax.ShapeDtypeStruct((n, d_conv), BF16)],
        scratch_shapes=[pltpu.VMEM((tm + 2 * CONV_HALO, d_conv), F32)],
        compiler_params=pltpu.CompilerParams(dimension_semantics=("arbitrary",),
                                             vmem_limit_bytes=VMEM_LIMIT),
        name="ffn1_proj_conv",
    )(x, n1, wg, wu, wd, nm, win, cw, cb, cn)


def _suffix_sum_matrix(tk):
    j = jnp.arange(2 * tk)[:, None] % tk
    s = jnp.arange(2 * tk)[None, :]
    return jnp.where(s < tk, j > s, True).astype(BF16)


def _attn_call(q, k, v, *, batch, seq):
    n, d = q.shape
    tq = ATT_BLOCK
    blocks = seq // tq
    return pl.pallas_call(
        _attn_kernel,
        grid=(batch, blocks),
        in_specs=[pl.BlockSpec((tq, d), lambda b, i: (b * blocks + i, 0)),
                  pl.BlockSpec((seq, d), lambda b, i: (b, 0)),
                  pl.BlockSpec((seq, d), lambda b, i: (b, 0)),
                  _resident((2 * tq, 2 * tq))],
        out_specs=pl.BlockSpec((tq, d), lambda b, i: (b * blocks + i, 0)),
        out_shape=jax.ShapeDtypeStruct((n, d), F32),
        scratch_shapes=[pltpu.VMEM((N_PAIRS, 2 * tq, LANES), BF16),
                        pltpu.VMEM((N_PAIRS * 2 * tq, tq), F32),
                        pltpu.VMEM((N_PAIRS, tq, LANES), F32)],
        compiler_params=pltpu.CompilerParams(dimension_semantics=("arbitrary", "arbitrary"),
                                             vmem_limit_bytes=VMEM_LIMIT),
        name="stickbreaking_attention",
    )(q, k, v, _suffix_sum_matrix(tq))


def _post_call(x1, ysb, yc, sbn, wo, n2, wg, wu, wd, fn, *, final):
    n, d = x1.shape
    tm = ROW_TILE
    d_ff = wg.shape[1]
    return pl.pallas_call(
        functools.partial(_post_kernel, final=final),
        grid=(n // tm,),
        in_specs=[_rows(tm, d), _rows(tm, ysb.shape[1]), _rows(tm, yc.shape[1]),
                  _resident((1, ysb.shape[1])), _resident(wo.shape), _resident((1, d)),
                  _resident((d, d_ff)), _resident((d, d_ff)), _resident((d_ff, d)), _resident((1, d))],
        out_specs=_rows(tm, d),
        out_shape=jax.ShapeDtypeStruct((n, d), F32),
        compiler_params=pltpu.CompilerParams(dimension_semantics=("arbitrary",),
                                             vmem_limit_bytes=VMEM_LIMIT),
        name="out_proj_ffn2",
    )(x1, ysb, yc, sbn, wo, n2, wg, wu, wd, fn)


def kernel(x, ffn1_norm, ffn1_w_gate, ffn1_w_up, ffn1_w_down, mix_norm, w_in, conv_w, conv_b,
           sb_out_norm, conv_out_norm, w_out, ffn2_norm, ffn2_w_gate, ffn2_w_up, ffn2_w_down,
           final_norm):
    batch, seq, d = x.shape
    depth = ffn1_norm.shape[0]
    assert seq % ROW_TILE == 0 and seq % ATT_BLOCK == 0 and w_in.shape[2] == 6 * D_SB
    row = lambda t: t.reshape(1, -1)
    xs = x.reshape(batch * seq, d)
    for l in range(depth):
        x1, q, k, v, yc = _pre_call(
            xs, row(ffn1_norm[l]), ffn1_w_gate[l].astype(BF16), ffn1_w_up[l].astype(BF16),
            ffn1_w_down[l].astype(BF16), row(mix_norm[l]), w_in[l].astype(BF16), conv_w[l],
            row(conv_b[l]), row(conv_out_norm[l]), seq=seq)
        ysb = _attn_call(q, k, v, batch=batch, seq=seq)
        xs = _post_call(
            x1, ysb, yc, row(sb_out_norm[l]), w_out[l].astype(BF16), row(ffn2_norm[l]),
            ffn2_w_gate[l].astype(BF16), ffn2_w_up[l].astype(BF16), ffn2_w_down[l].astype(BF16),
            row(final_norm), final=(l == depth - 1))
    return xs.reshape(batch, seq, d)
```

```python
import functools
import math

import jax
import jax.numpy as jnp
from jax import lax
from jax.experimental import pallas as pl
from jax.experimental.pallas import tpu as pltpu

F32 = jnp.float32
BF16 = jnp.bfloat16

EPS = 1e-6
SB_HEADS = 8
SB_HEAD_DIM = 64
D_SB = SB_HEADS * SB_HEAD_DIM
CONV_WIDTH = 3
LANES = 128
N_PAIRS = D_SB // LANES
ROW_TILE = 256
ATT_BLOCK = 128
FIRST_PASS_BLOCKS = 3
CONV_HALO = 8
LOG_WEIGHT_UNDERFLOW = -104.0
VMEM_LIMIT = 56 * 1024 * 1024


def _rms(x, g):
    return x * lax.rsqrt(jnp.mean(x * x, axis=-1, keepdims=True) + EPS) * g


def _swiglu_half_step(x, g, wg_ref, wu_ref, wd_ref):
    h = _rms(x, g).astype(BF16)
    gate = jnp.dot(h, wg_ref[...], preferred_element_type=F32)
    up = jnp.dot(h, wu_ref[...], preferred_element_type=F32)
    act = (gate * jax.nn.sigmoid(gate) * up).astype(BF16)
    return x + 0.5 * jnp.dot(act, wd_ref[...], preferred_element_type=F32)


def _pre_kernel(x_ref, n1_ref, wg_ref, wu_ref, wd_ref, nm_ref, win_ref, cw_ref, cb_ref, cn_ref,
                x1_ref, q_ref, k_ref, v_ref, yc_ref, xc_buf, *, tiles_per_seq, q_scale):
    tm = x_ref.shape[0]
    i = pl.program_id(0)
    x1 = _swiglu_half_step(x_ref[...], n1_ref[...], wg_ref, wu_ref, wd_ref)
    x1_ref[...] = x1
    h = _rms(x1, nm_ref[...]).astype(BF16)
    proj = jnp.dot(h, win_ref[...], preferred_element_type=F32)
    d = D_SB
    q_ref[...] = (proj[:, 0:d] * q_scale).astype(BF16)
    k_ref[...] = proj[:, d:2 * d].astype(BF16)
    v_ref[...] = proj[:, 2 * d:3 * d].astype(BF16)
    gate_b = proj[:, 3 * d:4 * d]
    xc = proj[:, 4 * d:5 * d] * proj[:, 5 * d:6 * d]

    first = i % tiles_per_seq == 0

    @pl.when(first)
    def _():
        xc_buf[0:CONV_HALO, :] = jnp.zeros((CONV_HALO, xc_buf.shape[1]), F32)

    @pl.when(jnp.logical_not(first))
    def _():
        xc_buf[0:CONV_HALO, :] = xc_buf[tm:tm + CONV_HALO, :]

    xc_buf[CONV_HALO:CONV_HALO + tm, :] = xc
    y = cb_ref[...]
    for j in range(CONV_WIDTH):
        off = CONV_HALO - (CONV_WIDTH - 1) + j
        y = y + xc_buf[off:off + tm, :] * cw_ref[j:j + 1, :]
    yc_ref[...] = _rms(gate_b * y, cn_ref[...]).astype(BF16)


def _attn_kernel(q_ref, k_ref, v_ref, tri_ref, o_ref, qs_ref, carry_ref, acc_ref, max_ref):
    tq = q_ref.shape[0]
    rows = N_PAIRS * 2 * tq
    i = pl.program_id(1)
    q0 = pl.multiple_of(i * tq, tq)
    lane = lax.broadcasted_iota(jnp.int32, (tq, LANES), 1)
    low = lane < SB_HEAD_DIM
    row = lax.broadcasted_iota(jnp.int32, (rows, tq), 0)
    col = lax.broadcasted_iota(jnp.int32, (rows, tq), 1)
    strict = col < (row & (tq - 1))
    tri = tri_ref[...]
    zero = jnp.zeros((), BF16)

    for p in range(N_PAIRS):
        q2 = q_ref[:, p * LANES:(p + 1) * LANES]
        qs_ref[p] = jnp.concatenate([jnp.where(low, q2, zero), jnp.where(low, zero, q2)], axis=0)

    def key_pass(kstarts, first):
        zs = [jnp.concatenate(
            [lax.dot_general(qs_ref[p], k_ref[pl.ds(ks, tq), p * LANES:(p + 1) * LANES],
                             (((1,), (1,)), ((), ())), preferred_element_type=F32)
             for p in range(N_PAIRS)], axis=0) for ks in kstarts]
        carry = None if first else carry_ref[...]
        weights = []
        for b, z in enumerate(zs):
            diagonal = first and b == 0
            sp = jnp.maximum(z, 0.0) + jnp.log(1.0 + jnp.exp(-jnp.abs(z)))
            if diagonal:
                sp = jnp.where(strict, sp, 0.0)
            hi = sp.astype(BF16)
            lo = (sp - hi.astype(F32)).astype(BF16)
            sums = jnp.dot(jnp.concatenate([hi, lo], axis=1), tri, preferred_element_type=F32)
            logit = z + sums[:, 0:tq]
            if carry is not None:
                logit = logit + carry
            a = jnp.exp(logit)
            if diagonal:
                a = jnp.where(strict, a, 0.0)
            weights.append(a.astype(BF16))
            carry = sums[:, tq:2 * tq] if carry is None else carry + sums[:, tq:2 * tq]
        carry_ref[...] = carry
        for p in range(N_PAIRS):
            r0 = p * 2 * tq
            ls = slice(p * LANES, (p + 1) * LANES)
            a2 = jnp.concatenate([w[r:r + tq] for w in weights for r in (r0, r0 + tq)], axis=1)
            vbs = [v_ref[pl.ds(ks, tq), ls] for ks in kstarts]
            v2 = jnp.concatenate([jnp.where(m, vb, zero) for vb in vbs for m in (low, ~low)], axis=0)
            out = jnp.dot(a2, v2, preferred_element_type=F32)
            acc_ref[p] = out if first else acc_ref[p] + out
        max_ref[0] = jnp.max(carry)

    full = i >= FIRST_PASS_BLOCKS - 1

    @pl.when(full)
    def _():
        key_pass([pl.multiple_of(q0 - b * tq, tq) for b in range(FIRST_PASS_BLOCKS)], True)

    @pl.when(jnp.logical_not(full))
    def _():
        key_pass([q0], True)

    def cond(state):
        kb, m = state
        return jnp.logical_and(kb >= 0, m > LOG_WEIGHT_UNDERFLOW)

    def body(state):
        kb, _ = state
        key_pass([pl.multiple_of(kb * tq, tq)], False)
        return kb - 1, max_ref[0]

    lax.while_loop(cond, body, (jnp.where(full, i - FIRST_PASS_BLOCKS, i - 1), max_ref[0]))

    for p in range(N_PAIRS):
        o_ref[:, p * LANES:(p + 1) * LANES] = acc_ref[p]


def _post_kernel(x1_ref, ysb_ref, yc_ref, sbn_ref, wo_ref, n2_ref, wg_ref, wu_ref, wd_ref, fn_ref,
                 out_ref, *, final):
    ysn = _rms(ysb_ref[...], sbn_ref[...]).astype(BF16)
    y = jnp.concatenate([ysn, yc_ref[...]], axis=1)
    x2 = x1_ref[...] + jnp.dot(y, wo_ref[...], preferred_element_type=F32)
    x3 = _swiglu_half_step(x2, n2_ref[...], wg_ref, wu_ref, wd_ref)
    out_ref[...] = _rms(x3, fn_ref[...]) if final else x3


def _resident(shape):
    return pl.BlockSpec(shape, lambda *_: (0,) * len(shape), pipeline_mode=pl.Buffered(1))


def _rows(tm, width):
    return pl.BlockSpec((tm, width), lambda i: (i, 0))


def _pre_call(x, n1, wg, wu, wd, nm, win, cw, cb, cn, *, seq):
    n, d = x.shape
    tm = ROW_TILE
    d_ff = wg.shape[1]
    d_conv = cw.shape[1]
    kern = functools.partial(_pre_kernel, tiles_per_seq=seq // tm, q_scale=1.0 / math.sqrt(SB_HEAD_DIM))
    return pl.pallas_call(
        kern,
        grid=(n // tm,),
        in_specs=[_rows(tm, d), _resident((1, d)), _resident((d, d_ff)), _resident((d, d_ff)),
                  _resident((d_ff, d)), _resident((1, d)), _resident(win.shape),
                  _resident(cw.shape), _resident((1, d_conv)), _resident((1, d_conv))],
        out_specs=[_rows(tm, d), _rows(tm, D_SB), _rows(tm, D_SB), _rows(tm, D_SB), _rows(tm, d_conv)],
        out_shape=[jax.ShapeDtypeStruct((n, d), F32)] + [jax.ShapeDtypeStruct((n, D_SB), BF16)] * 3
                  + [jax.ShapeDtypeStruct((n, d_conv), BF16)],
        scratch_shapes=[pltpu.VMEM((tm + 2 * CONV_HALO, d_conv), F32)],
        compiler_params=pltpu.CompilerParams(dimension_semantics=("arbitrary",),
                                             vmem_limit_bytes=VMEM_LIMIT),
        name="ffn1_proj_conv",
    )(x, n1, wg, wu, wd, nm, win, cw, cb, cn)


def _suffix_sum_matrix(tk):
    j = jnp.arange(2 * tk)[:, None] % tk
    s = jnp.arange(2 * tk)[None, :]
    return -jnp.where(s < tk, j >= s, True).astype(BF16)


def _attn_call(q, k, v, *, batch, seq):
    n, d = q.shape
    tq = ATT_BLOCK
    blocks = seq // tq
    return pl.pallas_call(
        _attn_kernel,
        grid=(batch, blocks),
        in_specs=[pl.BlockSpec((tq, d), lambda b, i: (b * blocks + i, 0)),
                  pl.BlockSpec((seq, d), lambda b, i: (b, 0)),
                  pl.BlockSpec((seq, d), lambda b, i: (b, 0)),
                  _resident((2 * tq, 2 * tq))],
        out_specs=pl.BlockSpec((tq, d), lambda b, i: (b * blocks + i, 0)),
        out_shape=jax.ShapeDtypeStruct((n, d), F32),
        scratch_shapes=[pltpu.VMEM((N_PAIRS, 2 * tq, LANES), BF16),
                        pltpu.VMEM((N_PAIRS * 2 * tq, tq), F32),
                        pltpu.VMEM((N_PAIRS, tq, LANES), F32),
                        pltpu.SMEM((1,), F32)],
        compiler_params=pltpu.CompilerParams(dimension_semantics=("arbitrary", "arbitrary"),
                                             vmem_limit_bytes=VMEM_LIMIT),
        name="stickbreaking_attention",
    )(q, k, v, _suffix_sum_matrix(tq))


def _post_call(x1, ysb, yc, sbn, wo, n2, wg, wu, wd, fn, *, final):
    n, d = x1.shape
    tm = ROW_TILE
    d_ff = wg.shape[1]
    return pl.pallas_call(
        functools.partial(_post_kernel, final=final),
        grid=(n // tm,),
        in_specs=[_rows(tm, d), _rows(tm, ysb.shape[1]), _rows(tm, yc.shape[1]),
                  _resident((1, ysb.shape[1])), _resident(wo.shape), _resident((1, d)),
                  _resident((d, d_ff)), _resident((d, d_ff)), _resident((d_ff, d)), _resident((1, d))],
        out_specs=_rows(tm, d),
        out_shape=jax.ShapeDtypeStruct((n, d), F32),
        compiler_params=pltpu.CompilerParams(dimension_semantics=("arbitrary",),
                                             vmem_limit_bytes=VMEM_LIMIT),
        name="out_proj_ffn2",
    )(x1, ysb, yc, sbn, wo, n2, wg, wu, wd, fn)


def kernel(x, ffn1_norm, ffn1_w_gate, ffn1_w_up, ffn1_w_down, mix_norm, w_in, conv_w, conv_b,
           sb_out_norm, conv_out_norm, w_out, ffn2_norm, ffn2_w_gate, ffn2_w_up, ffn2_w_down,
           final_norm):
    batch, seq, d = x.shape
    depth = ffn1_norm.shape[0]
    assert seq % ROW_TILE == 0 and seq % ATT_BLOCK == 0 and w_in.shape[2] == 6 * D_SB
    row = lambda t: t.reshape(1, -1)
    xs = x.reshape(batch * seq, d)
    for l in range(depth):
        x1, q, k, v, yc = _pre_call(
            xs, row(ffn1_norm[l]), ffn1_w_gate[l].astype(BF16), ffn1_w_up[l].astype(BF16),
            ffn1_w_down[l].astype(BF16), row(mix_norm[l]), w_in[l].astype(BF16), conv_w[l],
            row(conv_b[l]), row(conv_out_norm[l]), seq=seq)
        ysb = _attn_call(q, k, v, batch=batch, seq=seq)
        xs = _post_call(
            x1, ysb, yc, row(sb_out_norm[l]), w_out[l].astype(BF16), row(ffn2_norm[l]),
            ffn2_w_gate[l].astype(BF16), ffn2_w_up[l].astype(BF16), ffn2_w_down[l].astype(BF16),
            row(final_norm), final=(l == depth - 1))
    return xs.reshape(batch, seq, d)
```

```python
import functools
import math

import jax
import jax.numpy as jnp
from jax import lax
from jax.experimental import pallas as pl
from jax.experimental.pallas import tpu as pltpu

F32 = jnp.float32
BF16 = jnp.bfloat16

EPS = 1e-6
SB_HEADS = 8
SB_HEAD_DIM = 64
D_SB = SB_HEADS * SB_HEAD_DIM
CONV_WIDTH = 3
LANES = 128
BF16_SUBLANES = 16
N_PAIRS = D_SB // LANES
ROW_TILE = 256
ATT_BLOCK = 128
FIRST_PASS_BLOCKS = 3
CONV_HALO = 8
LOG_WEIGHT_UNDERFLOW = -104.0
VMEM_LIMIT = 56 * 1024 * 1024


def _rms(x, g):
    return x * lax.rsqrt(jnp.mean(x * x, axis=-1, keepdims=True) + EPS) * g


def _swiglu_half_step(x, g, wg_ref, wu_ref, wd_ref):
    h = _rms(x, g).astype(BF16)
    gate = jnp.dot(h, wg_ref[...], preferred_element_type=F32)
    up = jnp.dot(h, wu_ref[...], preferred_element_type=F32)
    act = (gate * jax.nn.sigmoid(gate) * up).astype(BF16)
    return x + 0.5 * jnp.dot(act, wd_ref[...], preferred_element_type=F32)


def _pre_kernel(x_ref, n1_ref, wg_ref, wu_ref, wd_ref, nm_ref, win_ref, cw_ref, cb_ref, cn_ref,
                *rest, tiles_per_seq, q_scale, n_cast):
    cast_src = rest[:n_cast]
    x1_ref, q_ref, k_ref, v_ref, yc_ref = rest[n_cast:n_cast + 5]
    cast_dst = rest[n_cast + 5:2 * n_cast + 5]
    xc_buf = rest[2 * n_cast + 5]
    tm = x_ref.shape[0]
    i = pl.program_id(0)
    for src, dst in zip(cast_src, cast_dst):
        dst[...] = src[...].astype(BF16)
    x1 = _swiglu_half_step(x_ref[...], n1_ref[...], wg_ref, wu_ref, wd_ref)
    x1_ref[...] = x1
    h = _rms(x1, nm_ref[...]).astype(BF16)
    proj = jnp.dot(h, win_ref[...], preferred_element_type=F32)
    d = D_SB
    q_ref[...] = (proj[:, 0:d] * q_scale).astype(BF16)
    k_ref[...] = proj[:, d:2 * d].astype(BF16)
    v_ref[...] = proj[:, 2 * d:3 * d].astype(BF16)
    gate_b = proj[:, 3 * d:4 * d]
    xc = proj[:, 4 * d:5 * d] * proj[:, 5 * d:6 * d]

    first = i % tiles_per_seq == 0

    @pl.when(first)
    def _():
        xc_buf[0:CONV_HALO, :] = jnp.zeros((CONV_HALO, xc_buf.shape[1]), F32)

    @pl.when(jnp.logical_not(first))
    def _():
        xc_buf[0:CONV_HALO, :] = xc_buf[tm:tm + CONV_HALO, :]

    xc_buf[CONV_HALO:CONV_HALO + tm, :] = xc
    y = cb_ref[...]
    for j in range(CONV_WIDTH):
        off = CONV_HALO - (CONV_WIDTH - 1) + j
        y = y + xc_buf[off:off + tm, :] * cw_ref[j:j + 1, :]
    yc_ref[...] = _rms(gate_b * y, cn_ref[...]).astype(BF16)


def _attn_kernel(q_ref, k_ref, v_ref, tri_ref, o_ref, qs_ref, carry_ref, acc_ref, max_ref):
    tq = q_ref.shape[0]
    rows = N_PAIRS * 2 * tq
    i = pl.program_id(1)
    q0 = pl.multiple_of(i * tq, tq)
    lane = lax.broadcasted_iota(jnp.int32, (tq, LANES), 1)
    low = lane < SB_HEAD_DIM
    row = lax.broadcasted_iota(jnp.int32, (rows, tq), 0)
    col = lax.broadcasted_iota(jnp.int32, (rows, tq), 1)
    strict = col < (row & (tq - 1))
    tri = tri_ref[...]
    zero = jnp.zeros((), BF16)

    for p in range(N_PAIRS):
        q2 = q_ref[:, p * LANES:(p + 1) * LANES]
        qs_ref[p] = jnp.concatenate([jnp.where(low, q2, zero), jnp.where(low, zero, q2)], axis=0)

    def key_pass(kstarts, first):
        zs = [jnp.concatenate(
            [lax.dot_general(qs_ref[p], k_ref[pl.ds(ks, tq), p * LANES:(p + 1) * LANES],
                             (((1,), (1,)), ((), ())), preferred_element_type=F32)
             for p in range(N_PAIRS)], axis=0) for ks in kstarts]
        carry = None if first else carry_ref[...]
        weights = []
        for b, z in enumerate(zs):
            diagonal = first and b == 0
            sp = jnp.maximum(z, 0.0) + jnp.log(1.0 + jnp.exp(-jnp.abs(z)))
            if diagonal:
                sp = jnp.where(strict, sp, 0.0)
            hi = sp.astype(BF16)
            lo = (sp - hi.astype(F32)).astype(BF16)
            sums = jnp.dot(jnp.concatenate([hi, lo], axis=1), tri, preferred_element_type=F32)
            logit = z + sums[:, 0:tq]
            if carry is not None:
                logit = logit + carry
            a = jnp.exp(logit)
            if diagonal:
                a = jnp.where(strict, a, 0.0)
            weights.append(a.astype(BF16))
            carry = sums[:, tq:2 * tq] if carry is None else carry + sums[:, tq:2 * tq]
        carry_ref[...] = carry
        for p in range(N_PAIRS):
            r0 = p * 2 * tq
            ls = slice(p * LANES, (p + 1) * LANES)
            a2 = jnp.concatenate([w[r:r + tq] for w in weights for r in (r0, r0 + tq)], axis=1)
            vbs = [v_ref[pl.ds(ks, tq), ls] for ks in kstarts]
            v2 = jnp.concatenate([jnp.where(m, vb, zero) for vb in vbs for m in (low, ~low)], axis=0)
            out = jnp.dot(a2, v2, preferred_element_type=F32)
            acc_ref[p] = out if first else acc_ref[p] + out
        max_ref[0] = jnp.max(carry)

    full = i >= FIRST_PASS_BLOCKS - 1

    @pl.when(full)
    def _():
        key_pass([pl.multiple_of(q0 - b * tq, tq) for b in range(FIRST_PASS_BLOCKS)], True)

    @pl.when(jnp.logical_not(full))
    def _():
        key_pass([q0], True)

    def cond(state):
        kb, m = state
        return jnp.logical_and(kb >= 0, m > LOG_WEIGHT_UNDERFLOW)

    def body(state):
        kb, _ = state
        key_pass([pl.multiple_of(kb * tq, tq)], False)
        return kb - 1, max_ref[0]

    lax.while_loop(cond, body, (jnp.where(full, i - FIRST_PASS_BLOCKS, i - 1), max_ref[0]))

    for p in range(N_PAIRS):
        o_ref[:, p * LANES:(p + 1) * LANES] = acc_ref[p]


def _post_kernel(x1_ref, ysb_ref, yc_ref, sbn_ref, wo_ref, n2_ref, wg_ref, wu_ref, wd_ref, fn_ref,
                 out_ref, *, final):
    ysn = _rms(ysb_ref[...], sbn_ref[...]).astype(BF16)
    y = jnp.concatenate([ysn, yc_ref[...]], axis=1)
    x2 = x1_ref[...] + jnp.dot(y, wo_ref[...], preferred_element_type=F32)
    x3 = _swiglu_half_step(x2, n2_ref[...], wg_ref, wu_ref, wd_ref)
    out_ref[...] = _rms(x3, fn_ref[...]) if final else x3


def _resident(shape):
    return pl.BlockSpec(shape, lambda *_: (0,) * len(shape), pipeline_mode=pl.Buffered(1))


def _rows(tm, width):
    return pl.BlockSpec((tm, width), lambda i: (i, 0))


def _slab_spec(shape, steps):
    rows, cols = shape
    slab = next(r for r in range(BF16_SUBLANES, rows + 1, BF16_SUBLANES) if rows % r == 0 and r * steps >= rows)
    last = rows // slab - 1
    return pl.BlockSpec((slab, cols), lambda i: (jnp.minimum(i, last), 0))


def _pre_call(x, n1, wg, wu, wd, nm, win, cw, cb, cn, cast, *, seq):
    n, d = x.shape
    tm = ROW_TILE
    steps = n // tm
    d_ff = wg.shape[1]
    d_conv = cw.shape[1]
    kern = functools.partial(_pre_kernel, tiles_per_seq=seq // tm, q_scale=1.0 / math.sqrt(SB_HEAD_DIM),
                             n_cast=len(cast))
    cast_specs = [_slab_spec(w.shape, steps) for w in cast]
    return pl.pallas_call(
        kern,
        grid=(steps,),
        in_specs=[_rows(tm, d), _resident((1, d)), _resident((d, d_ff)), _resident((d, d_ff)),
                  _resident((d_ff, d)), _resident((1, d)), _resident(win.shape),
                  _resident(cw.shape), _resident((1, d_conv)), _resident((1, d_conv))] + cast_specs,
        out_specs=[_rows(tm, d), _rows(tm, D_SB), _rows(tm, D_SB), _rows(tm, D_SB), _rows(tm, d_conv)]
                  + cast_specs,
        out_shape=[jax.ShapeDtypeStruct((n, d), F32)] + [jax.ShapeDtypeStruct((n, D_SB), BF16)] * 3
                  + [jax.ShapeDtypeStruct((n, d_conv), BF16)]
                  + [jax.ShapeDtypeStruct(w.shape, BF16) for w in cast],
        scratch_shapes=[pltpu.VMEM((tm + 2 * CONV_HALO, d_conv), F32)],
        compiler_params=pltpu.CompilerParams(dimension_semantics=("arbitrary",),
                                             vmem_limit_bytes=VMEM_LIMIT),
        name="ffn1_proj_conv",
    )(x, n1, wg, wu, wd, nm, win, cw, cb, cn, *cast)


def _suffix_sum_matrix(tk):
    j = jnp.arange(2 * tk)[:, None] % tk
    s = jnp.arange(2 * tk)[None, :]
    return -jnp.where(s < tk, j >= s, True).astype(BF16)


def _attn_call(q, k, v, *, batch, seq):
    n, d = q.shape
    tq = ATT_BLOCK
    blocks = seq // tq
    return pl.pallas_call(
        _attn_kernel,
        grid=(batch, blocks),
        in_specs=[pl.BlockSpec((tq, d), lambda b, i: (b * blocks + i, 0)),
                  pl.BlockSpec((seq, d), lambda b, i: (b, 0)),
                  pl.BlockSpec((seq, d), lambda b, i: (b, 0)),
                  _resident((2 * tq, 2 * tq))],
        out_specs=pl.BlockSpec((tq, d), lambda b, i: (b * blocks + i, 0)),
        out_shape=jax.ShapeDtypeStruct((n, d), F32),
        scratch_shapes=[pltpu.VMEM((N_PAIRS, 2 * tq, LANES), BF16),
                        pltpu.VMEM((N_PAIRS * 2 * tq, tq), F32),
                        pltpu.VMEM((N_PAIRS, tq, LANES), F32),
                        pltpu.SMEM((1,), F32)],
        compiler_params=pltpu.CompilerParams(dimension_semantics=("arbitrary", "arbitrary"),
                                             vmem_limit_bytes=VMEM_LIMIT),
        name="stickbreaking_attention",
    )(q, k, v, _suffix_sum_matrix(tq))


def _post_call(x1, ysb, yc, sbn, wo, n2, wg, wu, wd, fn, *, final):
    n, d = x1.shape
    tm = ROW_TILE
    d_ff = wg.shape[1]
    return pl.pallas_call(
        functools.partial(_post_kernel, final=final),
        grid=(n // tm,),
        in_specs=[_rows(tm, d), _rows(tm, ysb.shape[1]), _rows(tm, yc.shape[1]),
                  _resident((1, ysb.shape[1])), _resident(wo.shape), _resident((1, d)),
                  _resident((d, d_ff)), _resident((d, d_ff)), _resident((d_ff, d)), _resident((1, d))],
        out_specs=_rows(tm, d),
        out_shape=jax.ShapeDtypeStruct((n, d), F32),
        compiler_params=pltpu.CompilerParams(dimension_semantics=("arbitrary",),
                                             vmem_limit_bytes=VMEM_LIMIT),
        name="out_proj_ffn2",
    )(x1, ysb, yc, sbn, wo, n2, wg, wu, wd, fn)


def kernel(x, ffn1_norm, ffn1_w_gate, ffn1_w_up, ffn1_w_down, mix_norm, w_in, conv_w, conv_b,
           sb_out_norm, conv_out_norm, w_out, ffn2_norm, ffn2_w_gate, ffn2_w_up, ffn2_w_down,
           final_norm):
    batch, seq, d = x.shape
    depth = ffn1_norm.shape[0]
    assert seq % ROW_TILE == 0 and seq % ATT_BLOCK == 0 and w_in.shape[2] == 6 * D_SB
    row = lambda t: t.reshape(1, -1)
    xs = x.reshape(batch * seq, d)
    for l in range(depth):
        x1, q, k, v, yc, wo, wg2, wu2, wd2 = _pre_call(
            xs, row(ffn1_norm[l]), ffn1_w_gate[l].astype(BF16), ffn1_w_up[l].astype(BF16),
            ffn1_w_down[l].astype(BF16), row(mix_norm[l]), w_in[l].astype(BF16), conv_w[l],
            row(conv_b[l]), row(conv_out_norm[l]),
            (w_out[l], ffn2_w_gate[l], ffn2_w_up[l], ffn2_w_down[l]), seq=seq)
        ysb = _attn_call(q, k, v, batch=batch, seq=seq)
        xs = _post_call(
            x1, ysb, yc, row(sb_out_norm[l]), wo, row(ffn2_norm[l]), wg2, wu2, wd2,
            row(final_norm), final=(l == depth - 1))
    return xs.reshape(batch, seq, d)
```

```python
import functools
import math

import jax
import jax.numpy as jnp
from jax import lax
from jax.experimental import pallas as pl
from jax.experimental.pallas import tpu as pltpu

F32 = jnp.float32
BF16 = jnp.bfloat16

EPS = 1e-6
SB_HEADS = 8
SB_HEAD_DIM = 64
D_SB = SB_HEADS * SB_HEAD_DIM
CONV_WIDTH = 3
LANES = 128
BF16_SUBLANES = 16
N_PAIRS = D_SB // LANES
ROW_TILE = 256
ATT_BLOCK = 128
FIRST_PASS_BLOCKS = 3
CONV_HALO = 8
LOG_WEIGHT_UNDERFLOW = -104.0
VMEM_LIMIT = 56 * 1024 * 1024


def _rms(x, g):
    return x * lax.rsqrt(jnp.mean(x * x, axis=-1, keepdims=True) + EPS) * g


def _swiglu_half_step(x, g, wg_ref, wu_ref, wd_ref):
    h = _rms(x, g).astype(BF16)
    gate = jnp.dot(h, wg_ref[...], preferred_element_type=F32)
    up = jnp.dot(h, wu_ref[...], preferred_element_type=F32)
    act = (gate * jax.nn.sigmoid(gate) * up).astype(BF16)
    return x + 0.5 * jnp.dot(act, wd_ref[...], preferred_element_type=F32)


def _pre_kernel(x_ref, n1_ref, wg_ref, wu_ref, wd_ref, nm_ref, win_ref, cw_ref, cb_ref, cn_ref,
                *rest, tiles_per_seq, q_scale, n_cast):
    cast_src = rest[:n_cast]
    x1_ref, q_ref, k_ref, v_ref, yc_ref = rest[n_cast:n_cast + 5]
    cast_dst = rest[n_cast + 5:2 * n_cast + 5]
    xc_buf = rest[2 * n_cast + 5]
    tm = x_ref.shape[0]
    i = pl.program_id(0)
    for src, dst in zip(cast_src, cast_dst):
        dst[...] = src[...].astype(BF16)
    x1 = _swiglu_half_step(x_ref[...], n1_ref[...], wg_ref, wu_ref, wd_ref)
    x1_ref[...] = x1
    h = _rms(x1, nm_ref[...]).astype(BF16)
    proj = jnp.dot(h, win_ref[...], preferred_element_type=F32)
    d = D_SB
    q_ref[...] = (proj[:, 0:d] * q_scale).astype(BF16)
    k_ref[...] = proj[:, d:2 * d].astype(BF16)
    v_ref[...] = proj[:, 2 * d:3 * d].astype(BF16)
    gate_b = proj[:, 3 * d:4 * d]
    xc = proj[:, 4 * d:5 * d] * proj[:, 5 * d:6 * d]

    first = i % tiles_per_seq == 0

    @pl.when(first)
    def _():
        xc_buf[0:CONV_HALO, :] = jnp.zeros((CONV_HALO, xc_buf.shape[1]), F32)

    @pl.when(jnp.logical_not(first))
    def _():
        xc_buf[0:CONV_HALO, :] = xc_buf[tm:tm + CONV_HALO, :]

    xc_buf[CONV_HALO:CONV_HALO + tm, :] = xc
    y = cb_ref[...]
    for j in range(CONV_WIDTH):
        off = CONV_HALO - (CONV_WIDTH - 1) + j
        y = y + xc_buf[off:off + tm, :] * cw_ref[j:j + 1, :]
    yc_ref[...] = _rms(gate_b * y, cn_ref[...]).astype(BF16)


def _mix_post_kernel(q_ref, k_ref, v_ref, tri_ref, x1_ref, yc_ref, sbn_ref, wo_ref, n2_ref,
                     wg_ref, wu_ref, wd_ref, fn_ref, out_ref,
                     qs_ref, carry_ref, acc_ref, max_ref, ysb_ref, *, tiles_per_seq, n_tiles, final):
    tm = q_ref.shape[0]
    tq = ATT_BLOCK
    rows = N_PAIRS * 2 * tq
    s = pl.program_id(0)
    slot = s % 2
    first_block = (jnp.minimum(s, n_tiles - 1) % tiles_per_seq) * (tm // tq)

    @pl.when(s == 0)
    def _():
        ysb_ref[1] = jnp.zeros(ysb_ref.shape[1:], F32)

    lane = lax.broadcasted_iota(jnp.int32, (tq, LANES), 1)
    row = lax.broadcasted_iota(jnp.int32, (rows, tq), 0)
    col = lax.broadcasted_iota(jnp.int32, (rows, tq), 1)
    strict = col < (row & (tq - 1))
    tri_col = lax.broadcasted_iota(jnp.int32, tri_ref.shape, 1)
    zero = jnp.zeros((), BF16)

    def key_pass(j, blocks, first):
        k_lanes = [lane + sh for _, sh in blocks]
        starts = [pl.multiple_of(jnp.maximum(kb, 0) * tq, tq) for kb, _ in blocks]
        zs = [jnp.concatenate(
            [lax.dot_general(qs_ref[j, p],
                             jnp.where(kl < LANES, k_ref[pl.ds(ks, tq), p * LANES:(p + 1) * LANES], zero),
                             (((1,), (1,)), ((), ())), preferred_element_type=F32)
             for p in range(N_PAIRS)], axis=0) for ks, kl in zip(starts, k_lanes)]
        carry = None if first else carry_ref[j]
        weights = []
        for b, z in enumerate(zs):
            diagonal = first and b == 0
            sp = jnp.maximum(z, 0.0) + jnp.log(1.0 + jnp.exp(-jnp.abs(z)))
            if diagonal:
                sp = jnp.where(strict, sp, 0.0)
            hi = sp.astype(BF16)
            lo = (sp - hi.astype(F32)).astype(BF16)
            tri = jnp.where(tri_col + blocks[b][1] < 2 * tq, tri_ref[...], zero)
            sums = jnp.dot(jnp.concatenate([hi, lo], axis=1), tri, preferred_element_type=F32)
            logit = z + sums[:, 0:tq]
            if carry is not None:
                logit = logit + carry
            a = jnp.exp(logit)
            if diagonal:
                a = jnp.where(strict, a, 0.0)
            weights.append(a.astype(BF16))
            carry = sums[:, tq:2 * tq] if carry is None else carry + sums[:, tq:2 * tq]
        carry_ref[j] = carry
        for p in range(N_PAIRS):
            r0 = p * 2 * tq
            ls = slice(p * LANES, (p + 1) * LANES)
            a2 = jnp.concatenate([w[r:r + tq] for w in weights for r in (r0, r0 + tq)], axis=1)
            v2 = jnp.concatenate(
                [jnp.where(m, v_ref[pl.ds(ks, tq), ls], zero)
                 for ks, kl in zip(starts, k_lanes)
                 for m in (kl < SB_HEAD_DIM, (kl >= SB_HEAD_DIM) & (kl < LANES))], axis=0)
            out = jnp.dot(a2, v2, preferred_element_type=F32)
            acc_ref[j, p] = out if first else acc_ref[j, p] + out
        max_ref[j] = jnp.max(carry)

    n_q = tm // tq
    for j in range(n_q):
        low = lane < SB_HEAD_DIM
        for p in range(N_PAIRS):
            q2 = q_ref[j * tq:(j + 1) * tq, p * LANES:(p + 1) * LANES]
            qs_ref[j, p] = jnp.concatenate([jnp.where(low, q2, zero), jnp.where(low, zero, q2)], axis=0)
        g = first_block + j
        key_pass(j, [(g - b, jnp.where(g >= b, 0, 2 * tq)) for b in range(FIRST_PASS_BLOCKS)], True)

    ysn = _rms(ysb_ref[1 - slot], sbn_ref[...]).astype(BF16)
    y = jnp.concatenate([ysn, yc_ref[...]], axis=1)
    x2 = x1_ref[...] + jnp.dot(y, wo_ref[...], preferred_element_type=F32)
    x3 = _swiglu_half_step(x2, n2_ref[...], wg_ref, wu_ref, wd_ref)
    out_ref[...] = _rms(x3, fn_ref[...]) if final else x3

    for j in range(n_q):
        def cond(state):
            kb, m = state
            return jnp.logical_and(kb >= 0, m > LOG_WEIGHT_UNDERFLOW)

        def body(state, j=j):
            kb, _ = state
            key_pass(j, [(kb, 0)], False)
            return kb - 1, max_ref[j]

        lax.while_loop(cond, body, (first_block + j - FIRST_PASS_BLOCKS, max_ref[j]))
        for p in range(N_PAIRS):
            ysb_ref[slot, j * tq:(j + 1) * tq, p * LANES:(p + 1) * LANES] = acc_ref[j, p]


def _resident(shape):
    return pl.BlockSpec(shape, lambda *_: (0,) * len(shape), pipeline_mode=pl.Buffered(1))


def _rows(tm, width):
    return pl.BlockSpec((tm, width), lambda i: (i, 0))


def _slab_spec(shape, steps):
    rows, cols = shape
    slab = next(r for r in range(BF16_SUBLANES, rows + 1, BF16_SUBLANES) if rows % r == 0 and r * steps >= rows)
    last = rows // slab - 1
    return pl.BlockSpec((slab, cols), lambda i: (jnp.minimum(i, last), 0))


def _pre_call(x, n1, wg, wu, wd, nm, win, cw, cb, cn, cast, *, seq):
    n, d = x.shape
    tm = ROW_TILE
    steps = n // tm
    d_ff = wg.shape[1]
    d_conv = cw.shape[1]
    kern = functools.partial(_pre_kernel, tiles_per_seq=seq // tm, q_scale=1.0 / math.sqrt(SB_HEAD_DIM),
                             n_cast=len(cast))
    cast_specs = [_slab_spec(w.shape, steps) for w in cast]
    return pl.pallas_call(
        kern,
        grid=(steps,),
        in_specs=[_rows(tm, d), _resident((1, d)), _resident((d, d_ff)), _resident((d, d_ff)),
                  _resident((d_ff, d)), _resident((1, d)), _resident(win.shape),
                  _resident(cw.shape), _resident((1, d_conv)), _resident((1, d_conv))] + cast_specs,
        out_specs=[_rows(tm, d), _rows(tm, D_SB), _rows(tm, D_SB), _rows(tm, D_SB), _rows(tm, d_conv)]
                  + cast_specs,
        out_shape=[jax.ShapeDtypeStruct((n, d), F32)] + [jax.ShapeDtypeStruct((n, D_SB), BF16)] * 3
                  + [jax.ShapeDtypeStruct((n, d_conv), BF16)]
                  + [jax.ShapeDtypeStruct(w.shape, BF16) for w in cast],
        scratch_shapes=[pltpu.VMEM((tm + 2 * CONV_HALO, d_conv), F32)],
        compiler_params=pltpu.CompilerParams(dimension_semantics=("arbitrary",),
                                             vmem_limit_bytes=VMEM_LIMIT),
        name="ffn1_proj_conv",
    )(x, n1, wg, wu, wd, nm, win, cw, cb, cn, *cast)


def _suffix_sum_matrix(tk):
    j = jnp.arange(2 * tk)[:, None] % tk
    s = jnp.arange(2 * tk)[None, :]
    return -jnp.where(s < tk, j >= s, True).astype(BF16)


def _mix_post_call(q, k, v, x1, yc, sbn, wo, n2, wg, wu, wd, fn, *, seq, final):
    n, d = x1.shape
    tm = ROW_TILE
    tq = ATT_BLOCK
    n_tiles = n // tm
    tiles_per_seq = seq // tm
    d_ff = wg.shape[1]
    cur = lambda s: (jnp.minimum(s, n_tiles - 1), 0)
    prev = lambda s: (jnp.maximum(s - 1, 0), 0)
    seq_of_cur = lambda s: (jnp.minimum(s, n_tiles - 1) // tiles_per_seq, 0)
    kern = functools.partial(_mix_post_kernel, tiles_per_seq=tiles_per_seq, n_tiles=n_tiles, final=final)
    return pl.pallas_call(
        kern,
        grid=(n_tiles + 1,),
        in_specs=[pl.BlockSpec((tm, D_SB), cur),
                  pl.BlockSpec((seq, D_SB), seq_of_cur, pipeline_mode=pl.Buffered(1)),
                  pl.BlockSpec((seq, D_SB), seq_of_cur, pipeline_mode=pl.Buffered(1)),
                  _resident((2 * tq, 2 * tq)),
                  pl.BlockSpec((tm, d), prev), pl.BlockSpec((tm, yc.shape[1]), prev),
                  _resident((1, D_SB)), _resident(wo.shape), _resident((1, d)),
                  _resident((d, d_ff)), _resident((d, d_ff)), _resident((d_ff, d)), _resident((1, d))],
        out_specs=pl.BlockSpec((tm, d), prev),
        out_shape=jax.ShapeDtypeStruct((n, d), F32),
        scratch_shapes=[pltpu.VMEM((tm // tq, N_PAIRS, 2 * tq, LANES), BF16),
                        pltpu.VMEM((tm // tq, N_PAIRS * 2 * tq, tq), F32),
                        pltpu.VMEM((tm // tq, N_PAIRS, tq, LANES), F32),
                        pltpu.SMEM((tm // tq,), F32),
                        pltpu.VMEM((2, tm, D_SB), F32)],
        compiler_params=pltpu.CompilerParams(dimension_semantics=("arbitrary",),
                                             vmem_limit_bytes=VMEM_LIMIT),
        name="attention_out_proj_ffn2",
    )(q, k, v, _suffix_sum_matrix(tq), x1, yc, sbn, wo, n2, wg, wu, wd, fn)


def kernel(x, ffn1_norm, ffn1_w_gate, ffn1_w_up, ffn1_w_down, mix_norm, w_in, conv_w, conv_b,
           sb_out_norm, conv_out_norm, w_out, ffn2_norm, ffn2_w_gate, ffn2_w_up, ffn2_w_down,
           final_norm):
    batch, seq, d = x.shape
    depth = ffn1_norm.shape[0]
    assert seq % ROW_TILE == 0 and seq % ATT_BLOCK == 0 and w_in.shape[2] == 6 * D_SB
    row = lambda t: t.reshape(1, -1)
    xs = x.reshape(batch * seq, d)
    for l in range(depth):
        x1, q, k, v, yc, wo, wg2, wu2, wd2 = _pre_call(
            xs, row(ffn1_norm[l]), ffn1_w_gate[l].astype(BF16), ffn1_w_up[l].astype(BF16),
            ffn1_w_down[l].astype(BF16), row(mix_norm[l]), w_in[l].astype(BF16), conv_w[l],
            row(conv_b[l]), row(conv_out_norm[l]),
            (w_out[l], ffn2_w_gate[l], ffn2_w_up[l], ffn2_w_down[l]), seq=seq)
        xs = _mix_post_call(
            q, k, v, x1, yc, row(sb_out_norm[l]), wo, row(ffn2_norm[l]), wg2, wu2, wd2,
            row(final_norm), seq=seq, final=(l == depth - 1))
    return xs.reshape(batch, seq, d)
```

```python
import functools
import math

import jax
import jax.numpy as jnp
from jax import lax
from jax.experimental import pallas as pl
from jax.experimental.pallas import tpu as pltpu

F32 = jnp.float32
BF16 = jnp.bfloat16

EPS = 1e-6
SB_HEADS = 8
SB_HEAD_DIM = 64
D_SB = SB_HEADS * SB_HEAD_DIM
CONV_WIDTH = 3
LANES = 128
BF16_SUBLANES = 16
N_PAIRS = D_SB // LANES
ROW_TILE = 256
PRE_ROW_TILE = 512
SUB_TILE = 256
ATT_BLOCK = 128
FIRST_PASS_BLOCKS = 3
CONV_HALO = 8
LOG_WEIGHT_UNDERFLOW = -104.0
VMEM_LIMIT = 56 * 1024 * 1024


def _rms(x, g):
    return x * lax.rsqrt(jnp.mean(x * x, axis=-1, keepdims=True) + EPS) * g


def _swiglu_half_step(x, g, wg_ref, wu_ref, wd_ref):
    h = _rms(x, g).astype(BF16)
    gate = jnp.dot(h, wg_ref[...], preferred_element_type=F32)
    up = jnp.dot(h, wu_ref[...], preferred_element_type=F32)
    act = (gate * jax.nn.sigmoid(gate) * up).astype(BF16)
    return x + 0.5 * jnp.dot(act, wd_ref[...], preferred_element_type=F32)


def _pre_kernel(x_ref, n1_ref, wg_ref, wu_ref, wd_ref, nm_ref, win_ref, cw_ref, cb_ref, cn_ref,
                *rest, tiles_per_seq, q_scale, n_cast):
    cast_src = rest[:n_cast]
    x1_ref, q_ref, k_ref, v_ref, yc_ref = rest[n_cast:n_cast + 5]
    cast_dst = rest[n_cast + 5:2 * n_cast + 5]
    xc_buf = rest[2 * n_cast + 5]
    tm = x_ref.shape[0]
    i = pl.program_id(0)
    for src, dst in zip(cast_src, cast_dst):
        dst[...] = src[...].astype(BF16)
    first = i % tiles_per_seq == 0

    @pl.when(first)
    def _():
        xc_buf[0:CONV_HALO, :] = jnp.zeros((CONV_HALO, xc_buf.shape[1]), F32)

    @pl.when(jnp.logical_not(first))
    def _():
        xc_buf[0:CONV_HALO, :] = xc_buf[tm:tm + CONV_HALO, :]

    d = D_SB
    subs = [slice(r, r + SUB_TILE) for r in range(0, tm, SUB_TILE)]
    xs = [x_ref[r, :] for r in subs]
    hs = [_rms(x, n1_ref[...]).astype(BF16) for x in xs]
    acts = []
    for h in hs:
        gate = jnp.dot(h, wg_ref[...], preferred_element_type=F32)
        up = jnp.dot(h, wu_ref[...], preferred_element_type=F32)
        acts.append((gate * jax.nn.sigmoid(gate) * up).astype(BF16))
    hs = []
    for r, x, act in zip(subs, xs, acts):
        x1 = x + 0.5 * jnp.dot(act, wd_ref[...], preferred_element_type=F32)
        x1_ref[r, :] = x1
        hs.append(_rms(x1, nm_ref[...]).astype(BF16))
    gates = []
    for r, h in zip(subs, hs):
        proj = jnp.dot(h, win_ref[...], preferred_element_type=F32)
        q_ref[r, :] = (proj[:, 0:d] * q_scale).astype(BF16)
        k_ref[r, :] = proj[:, d:2 * d].astype(BF16)
        v_ref[r, :] = proj[:, 2 * d:3 * d].astype(BF16)
        gates.append(proj[:, 3 * d:4 * d])
        xc_buf[CONV_HALO + r.start:CONV_HALO + r.stop, :] = proj[:, 4 * d:5 * d] * proj[:, 5 * d:6 * d]
    for r, gate_b in zip(subs, gates):
        y = cb_ref[...]
        for j in range(CONV_WIDTH):
            off = CONV_HALO - (CONV_WIDTH - 1) + j + r.start
            y = y + xc_buf[off:off + SUB_TILE, :] * cw_ref[j:j + 1, :]
        yc_ref[r, :] = _rms(gate_b * y, cn_ref[...]).astype(BF16)


def _mix_post_kernel(q_ref, k_ref, v_ref, tri_ref, x1_ref, yc_ref, sbn_ref, wo_ref, n2_ref,
                     wg_ref, wu_ref, wd_ref, fn_ref, out_ref,
                     qs_ref, carry_ref, acc_ref, max_ref, ysb_ref, *, tiles_per_seq, n_tiles, final):
    tm = q_ref.shape[0]
    tq = ATT_BLOCK
    rows = N_PAIRS * 2 * tq
    s = pl.program_id(0)
    slot = s % 2
    first_block = (jnp.minimum(s, n_tiles - 1) % tiles_per_seq) * (tm // tq)

    @pl.when(s == 0)
    def _():
        ysb_ref[1] = jnp.zeros(ysb_ref.shape[1:], F32)

    lane = lax.broadcasted_iota(jnp.int32, (tq, LANES), 1)
    row = lax.broadcasted_iota(jnp.int32, (rows, tq), 0)
    col = lax.broadcasted_iota(jnp.int32, (rows, tq), 1)
    strict = col < (row & (tq - 1))
    tri_col = lax.broadcasted_iota(jnp.int32, tri_ref.shape, 1)
    zero = jnp.zeros((), BF16)

    def key_pass(j, blocks, first):
        k_lanes = [lane + sh for _, sh in blocks]
        starts = [pl.multiple_of(jnp.maximum(kb, 0) * tq, tq) for kb, _ in blocks]
        zs = [jnp.concatenate(
            [lax.dot_general(qs_ref[j, p],
                             jnp.where(kl < LANES, k_ref[pl.ds(ks, tq), p * LANES:(p + 1) * LANES], zero),
                             (((1,), (1,)), ((), ())), preferred_element_type=F32)
             for p in range(N_PAIRS)], axis=0) for ks, kl in zip(starts, k_lanes)]
        carry = None if first else carry_ref[j]
        weights = []
        for b, z in enumerate(zs):
            diagonal = first and b == 0
            sp = jnp.maximum(z, 0.0) + jnp.log(1.0 + jnp.exp(-jnp.abs(z)))
            if diagonal:
                sp = jnp.where(strict, sp, 0.0)
            hi = sp.astype(BF16)
            lo = (sp - hi.astype(F32)).astype(BF16)
            tri = jnp.where(tri_col + blocks[b][1] < 2 * tq, tri_ref[...], zero)
            sums = jnp.dot(jnp.concatenate([hi, lo], axis=1), tri, preferred_element_type=F32)
            logit = z + sums[:, 0:tq]
            if carry is not None:
                logit = logit + carry
            a = jnp.exp(logit)
            if diagonal:
                a = jnp.where(strict, a, 0.0)
            weights.append(a.astype(BF16))
            carry = sums[:, tq:2 * tq] if carry is None else carry + sums[:, tq:2 * tq]
        carry_ref[j] = carry
        for p in range(N_PAIRS):
            r0 = p * 2 * tq
            ls = slice(p * LANES, (p + 1) * LANES)
            a2 = jnp.concatenate([w[r:r + tq] for w in weights for r in (r0, r0 + tq)], axis=1)
            v2 = jnp.concatenate(
                [jnp.where(m, v_ref[pl.ds(ks, tq), ls], zero)
                 for ks, kl in zip(starts, k_lanes)
                 for m in (kl < SB_HEAD_DIM, (kl >= SB_HEAD_DIM) & (kl < LANES))], axis=0)
            out = jnp.dot(a2, v2, preferred_element_type=F32)
            acc_ref[j, p] = out if first else acc_ref[j, p] + out
        max_ref[j] = jnp.max(carry)

    n_q = tm // tq
    for j in range(n_q):
        low = lane < SB_HEAD_DIM
        for p in range(N_PAIRS):
            q2 = q_ref[j * tq:(j + 1) * tq, p * LANES:(p + 1) * LANES]
            qs_ref[j, p] = jnp.concatenate([jnp.where(low, q2, zero), jnp.where(low, zero, q2)], axis=0)
        g = first_block + j
        key_pass(j, [(g - b, jnp.where(g >= b, 0, 2 * tq)) for b in range(FIRST_PASS_BLOCKS)], True)

    ysn = _rms(ysb_ref[1 - slot], sbn_ref[...]).astype(BF16)
    y = jnp.concatenate([ysn, yc_ref[...]], axis=1)
    x2 = x1_ref[...] + jnp.dot(y, wo_ref[...], preferred_element_type=F32)
    x3 = _swiglu_half_step(x2, n2_ref[...], wg_ref, wu_ref, wd_ref)
    out_ref[...] = _rms(x3, fn_ref[...]) if final else x3

    for j in range(n_q):
        def cond(state):
            kb, m = state
            return jnp.logical_and(kb >= 0, m > LOG_WEIGHT_UNDERFLOW)

        def body(state, j=j):
            kb, _ = state
            key_pass(j, [(kb, 0)], False)
            return kb - 1, max_ref[j]

        lax.while_loop(cond, body, (first_block + j - FIRST_PASS_BLOCKS, max_ref[j]))
        for p in range(N_PAIRS):
            ysb_ref[slot, j * tq:(j + 1) * tq, p * LANES:(p + 1) * LANES] = acc_ref[j, p]


def _resident(shape):
    return pl.BlockSpec(shape, lambda *_: (0,) * len(shape), pipeline_mode=pl.Buffered(1))


def _rows(tm, width):
    return pl.BlockSpec((tm, width), lambda i: (i, 0))


def _slab_spec(shape, steps):
    rows, cols = shape
    slab = next(r for r in range(BF16_SUBLANES, rows + 1, BF16_SUBLANES) if rows % r == 0 and r * steps >= rows)
    last = rows // slab - 1
    return pl.BlockSpec((slab, cols), lambda i: (jnp.minimum(i, last), 0))


def _pre_call(x, n1, wg, wu, wd, nm, win, cw, cb, cn, cast, *, seq):
    n, d = x.shape
    tm = PRE_ROW_TILE
    steps = n // tm
    d_ff = wg.shape[1]
    d_conv = cw.shape[1]
    kern = functools.partial(_pre_kernel, tiles_per_seq=seq // tm, q_scale=1.0 / math.sqrt(SB_HEAD_DIM),
                             n_cast=len(cast))
    cast_specs = [_slab_spec(w.shape, steps) for w in cast]
    return pl.pallas_call(
        kern,
        grid=(steps,),
        in_specs=[_rows(tm, d), _resident((1, d)), _resident((d, d_ff)), _resident((d, d_ff)),
                  _resident((d_ff, d)), _resident((1, d)), _resident(win.shape),
                  _resident(cw.shape), _resident((1, d_conv)), _resident((1, d_conv))] + cast_specs,
        out_specs=[_rows(tm, d), _rows(tm, D_SB), _rows(tm, D_SB), _rows(tm, D_SB), _rows(tm, d_conv)]
                  + cast_specs,
        out_shape=[jax.ShapeDtypeStruct((n, d), F32)] + [jax.ShapeDtypeStruct((n, D_SB), BF16)] * 3
                  + [jax.ShapeDtypeStruct((n, d_conv), BF16)]
                  + [jax.ShapeDtypeStruct(w.shape, BF16) for w in cast],
        scratch_shapes=[pltpu.VMEM((tm + 2 * CONV_HALO, d_conv), F32)],
        compiler_params=pltpu.CompilerParams(dimension_semantics=("arbitrary",),
                                             vmem_limit_bytes=VMEM_LIMIT),
        name="ffn1_proj_conv",
    )(x, n1, wg, wu, wd, nm, win, cw, cb, cn, *cast)


def _suffix_sum_matrix(tk):
    j = jnp.arange(2 * tk)[:, None] % tk
    s = jnp.arange(2 * tk)[None, :]
    return -jnp.where(s < tk, j >= s, True).astype(BF16)


def _mix_post_call(q, k, v, x1, yc, sbn, wo, n2, wg, wu, wd, fn, *, seq, final):
    n, d = x1.shape
    tm = ROW_TILE
    tq = ATT_BLOCK
    n_tiles = n // tm
    tiles_per_seq = seq // tm
    d_ff = wg.shape[1]
    cur = lambda s: (jnp.minimum(s, n_tiles - 1), 0)
    prev = lambda s: (jnp.maximum(s - 1, 0), 0)
    seq_of_cur = lambda s: (jnp.minimum(s, n_tiles - 1) // tiles_per_seq, 0)
    kern = functools.partial(_mix_post_kernel, tiles_per_seq=tiles_per_seq, n_tiles=n_tiles, final=final)
    return pl.pallas_call(
        kern,
        grid=(n_tiles + 1,),
        in_specs=[pl.BlockSpec((tm, D_SB), cur),
                  pl.BlockSpec((seq, D_SB), seq_of_cur, pipeline_mode=pl.Buffered(1)),
                  pl.BlockSpec((seq, D_SB), seq_of_cur, pipeline_mode=pl.Buffered(1)),
                  _resident((2 * tq, 2 * tq)),
                  pl.BlockSpec((tm, d), prev), pl.BlockSpec((tm, yc.shape[1]), prev),
                  _resident((1, D_SB)), _resident(wo.shape), _resident((1, d)),
                  _resident((d, d_ff)), _resident((d, d_ff)), _resident((d_ff, d)), _resident((1, d))],
        out_specs=pl.BlockSpec((tm, d), prev),
        out_shape=jax.ShapeDtypeStruct((n, d), F32),
        scratch_shapes=[pltpu.VMEM((tm // tq, N_PAIRS, 2 * tq, LANES), BF16),
                        pltpu.VMEM((tm // tq, N_PAIRS * 2 * tq, tq), F32),
                        pltpu.VMEM((tm // tq, N_PAIRS, tq, LANES), F32),
                        pltpu.SMEM((tm // tq,), F32),
                        pltpu.VMEM((2, tm, D_SB), F32)],
        compiler_params=pltpu.CompilerParams(dimension_semantics=("arbitrary",),
                                             vmem_limit_bytes=VMEM_LIMIT),
        name="attention_out_proj_ffn2",
    )(q, k, v, _suffix_sum_matrix(tq), x1, yc, sbn, wo, n2, wg, wu, wd, fn)


def kernel(x, ffn1_norm, ffn1_w_gate, ffn1_w_up, ffn1_w_down, mix_norm, w_in, conv_w, conv_b,
           sb_out_norm, conv_out_norm, w_out, ffn2_norm, ffn2_w_gate, ffn2_w_up, ffn2_w_down,
           final_norm):
    batch, seq, d = x.shape
    depth = ffn1_norm.shape[0]
    assert seq % ROW_TILE == 0 and seq % PRE_ROW_TILE == 0 and w_in.shape[2] == 6 * D_SB
    row = lambda t: t.reshape(1, -1)
    xs = x.reshape(batch * seq, d)
    for l in range(depth):
        x1, q, k, v, yc, wo, wg2, wu2, wd2 = _pre_call(
            xs, row(ffn1_norm[l]), ffn1_w_gate[l].astype(BF16), ffn1_w_up[l].astype(BF16),
            ffn1_w_down[l].astype(BF16), row(mix_norm[l]), w_in[l].astype(BF16), conv_w[l],
            row(conv_b[l]), row(conv_out_norm[l]),
            (w_out[l], ffn2_w_gate[l], ffn2_w_up[l], ffn2_w_down[l]), seq=seq)
        xs = _mix_post_call(
            q, k, v, x1, yc, row(sb_out_norm[l]), wo, row(ffn2_norm[l]), wg2, wu2, wd2,
            row(final_norm), seq=seq, final=(l == depth - 1))
    return xs.reshape(batch, seq, d)
```

```python
import functools
import math

import jax
import jax.numpy as jnp
from jax import lax
from jax.experimental import pallas as pl
from jax.experimental.pallas import tpu as pltpu

F32 = jnp.float32
BF16 = jnp.bfloat16

EPS = 1e-6
SB_HEADS = 8
SB_HEAD_DIM = 64
D_SB = SB_HEADS * SB_HEAD_DIM
CONV_WIDTH = 3
LANES = 128
BF16_SUBLANES = 16
N_PAIRS = D_SB // LANES
ROW_TILE = 256
PRE_ROW_TILE = 512
SUB_TILE = 256
FF_CHUNK = 256
ATT_BLOCK = 128
FIRST_PASS_BLOCKS = 3
CONV_HALO = 8
LOG_WEIGHT_UNDERFLOW = -104.0
VMEM_LIMIT = 56 * 1024 * 1024


def _rms(x, g):
    return x * lax.rsqrt(jnp.mean(x * x, axis=-1, keepdims=True) + EPS) * g


def _swiglu_half_step(x, g, wg_ref, wu_ref, wd_ref):
    h = _rms(x, g).astype(BF16)
    gate = jnp.dot(h, wg_ref[...], preferred_element_type=F32)
    up = jnp.dot(h, wu_ref[...], preferred_element_type=F32)
    act = (gate * jax.nn.sigmoid(gate) * up).astype(BF16)
    return x + 0.5 * jnp.dot(act, wd_ref[...], preferred_element_type=F32)


def _pre_kernel(x_ref, n1_ref, wg_ref, wu_ref, wd_ref, nm_ref, win_ref, cw_ref, cb_ref, cn_ref,
                *rest, tiles_per_seq, q_scale, n_cast):
    cast_src = rest[:n_cast]
    x1_ref, q_ref, k_ref, v_ref, yc_ref = rest[n_cast:n_cast + 5]
    cast_dst = rest[n_cast + 5:2 * n_cast + 5]
    xc_buf = rest[2 * n_cast + 5]
    tm = x_ref.shape[0]
    i = pl.program_id(0)
    for src, dst in zip(cast_src, cast_dst):
        dst[...] = src[...].astype(BF16)
    first = i % tiles_per_seq == 0

    @pl.when(first)
    def _():
        xc_buf[0:CONV_HALO, :] = jnp.zeros((CONV_HALO, xc_buf.shape[1]), F32)

    @pl.when(jnp.logical_not(first))
    def _():
        xc_buf[0:CONV_HALO, :] = xc_buf[tm:tm + CONV_HALO, :]

    d = D_SB
    subs = [slice(r, r + SUB_TILE) for r in range(0, tm, SUB_TILE)]
    xs = [x_ref[r, :] for r in subs]
    hs = [_rms(x, n1_ref[...]).astype(BF16) for x in xs]
    acts = []
    for h in hs:
        gate = jnp.dot(h, wg_ref[...], preferred_element_type=F32)
        up = jnp.dot(h, wu_ref[...], preferred_element_type=F32)
        acts.append((gate * jax.nn.sigmoid(gate) * up).astype(BF16))
    hs = []
    for r, x, act in zip(subs, xs, acts):
        x1 = x + 0.5 * jnp.dot(act, wd_ref[...], preferred_element_type=F32)
        x1_ref[r, :] = x1
        hs.append(_rms(x1, nm_ref[...]).astype(BF16))
    gates = []
    for r, h in zip(subs, hs):
        proj = jnp.dot(h, win_ref[...], preferred_element_type=F32)
        q_ref[r, :] = (proj[:, 0:d] * q_scale).astype(BF16)
        k_ref[r, :] = proj[:, d:2 * d].astype(BF16)
        v_ref[r, :] = proj[:, 2 * d:3 * d].astype(BF16)
        gates.append(proj[:, 3 * d:4 * d])
        xc_buf[CONV_HALO + r.start:CONV_HALO + r.stop, :] = proj[:, 4 * d:5 * d] * proj[:, 5 * d:6 * d]
    for r, gate_b in zip(subs, gates):
        y = cb_ref[...]
        for j in range(CONV_WIDTH):
            off = CONV_HALO - (CONV_WIDTH - 1) + j + r.start
            y = y + xc_buf[off:off + SUB_TILE, :] * cw_ref[j:j + 1, :]
        yc_ref[r, :] = _rms(gate_b * y, cn_ref[...]).astype(BF16)


def _mix_post_kernel(q_ref, k_ref, v_ref, tri_ref, x1_ref, yc_ref, sbn_ref, wo_ref, n2_ref,
                     wg_ref, wu_ref, wd_ref, fn_ref, out_ref,
                     qs_ref, carry_ref, acc_ref, max_ref, ysb_ref, *, tiles_per_seq, n_tiles, final):
    tm = q_ref.shape[0]
    tq = ATT_BLOCK
    rows = N_PAIRS * 2 * tq
    s = pl.program_id(0)
    slot = s % 2
    first_block = (jnp.minimum(s, n_tiles - 1) % tiles_per_seq) * (tm // tq)

    @pl.when(s == 0)
    def _():
        ysb_ref[1] = jnp.zeros(ysb_ref.shape[1:], F32)

    lane = lax.broadcasted_iota(jnp.int32, (tq, LANES), 1)
    row = lax.broadcasted_iota(jnp.int32, (rows, tq), 0)
    col = lax.broadcasted_iota(jnp.int32, (rows, tq), 1)
    strict = col < (row & (tq - 1))
    tri_col = lax.broadcasted_iota(jnp.int32, tri_ref.shape, 1)
    zero = jnp.zeros((), BF16)

    def block_weights(j, kb, shift, carry, diagonal):
        k_lane = lane + shift
        ks = pl.multiple_of(jnp.maximum(kb, 0) * tq, tq)
        z = jnp.concatenate(
            [lax.dot_general(qs_ref[j, p],
                             jnp.where(k_lane < LANES, k_ref[pl.ds(ks, tq), p * LANES:(p + 1) * LANES], zero),
                             (((1,), (1,)), ((), ())), preferred_element_type=F32)
             for p in range(N_PAIRS)], axis=0)
        sp = jnp.maximum(z, 0.0) + jnp.log(1.0 + jnp.exp(-jnp.abs(z)))
        if diagonal:
            sp = jnp.where(strict, sp, 0.0)
        hi = sp.astype(BF16)
        lo = (sp - hi.astype(F32)).astype(BF16)
        tri = jnp.where(tri_col + shift < 2 * tq, tri_ref[...], zero)
        sums = jnp.dot(jnp.concatenate([hi, lo], axis=1), tri, preferred_element_type=F32)
        logit = z + sums[:, 0:tq]
        if carry is not None:
            logit = logit + carry
        a = jnp.exp(logit)
        if diagonal:
            a = jnp.where(strict, a, 0.0)
        carry = sums[:, tq:2 * tq] if carry is None else carry + sums[:, tq:2 * tq]
        v2 = [jnp.concatenate(
            [jnp.where(m, v_ref[pl.ds(ks, tq), p * LANES:(p + 1) * LANES], zero)
             for m in (k_lane < SB_HEAD_DIM, (k_lane >= SB_HEAD_DIM) & (k_lane < LANES))], axis=0)
            for p in range(N_PAIRS)]
        return a.astype(BF16), carry, v2

    def apply_values(j, weights, values, carry, first):
        carry_ref[j] = carry
        for p in range(N_PAIRS):
            r0 = p * 2 * tq
            a2 = jnp.concatenate([w[r:r + tq] for w in weights for r in (r0, r0 + tq)], axis=1)
            v2 = jnp.concatenate([v[p] for v in values], axis=0)
            out = jnp.dot(a2, v2, preferred_element_type=F32)
            acc_ref[j, p] = out if first else acc_ref[j, p] + out
        max_ref[j] = jnp.max(carry)

    n_q = tm // tq
    low = lane < SB_HEAD_DIM
    for j in range(n_q):
        for p in range(N_PAIRS):
            q2 = q_ref[j * tq:(j + 1) * tq, p * LANES:(p + 1) * LANES]
            qs_ref[j, p] = jnp.concatenate([jnp.where(low, q2, zero), jnp.where(low, zero, q2)], axis=0)

    state = {}

    def attention_item(j, b):
        weights, values, carry = state.get(j, ([], [], None))
        g = first_block + j
        a, carry, v2 = block_weights(j, g - b, jnp.where(g >= b, 0, 2 * tq), carry, diagonal=(b == 0))
        state[j] = (weights + [a], values + [v2], carry)
        if b == FIRST_PASS_BLOCKS - 1:
            apply_values(j, *state[j], first=True)

    items = [(j, b) for j in range(n_q) for b in range(FIRST_PASS_BLOCKS)]
    d_ff = wg_ref.shape[1]
    chunks = [slice(c, c + FF_CHUNK) for c in range(0, d_ff, FF_CHUNK)]
    after_chunk = {}
    for n, item in enumerate(items):
        after_chunk.setdefault(n * len(chunks) // len(items), []).append(item)

    ysn = _rms(ysb_ref[1 - slot], sbn_ref[...]).astype(BF16)
    y = jnp.concatenate([ysn, yc_ref[...]], axis=1)
    x2 = x1_ref[...] + jnp.dot(y, wo_ref[...], preferred_element_type=F32)
    h = _rms(x2, n2_ref[...]).astype(BF16)
    acts = []
    for ci, c in enumerate(chunks):
        gate = jnp.dot(h, wg_ref[:, c], preferred_element_type=F32)
        up = jnp.dot(h, wu_ref[:, c], preferred_element_type=F32)
        acts.append((gate * jax.nn.sigmoid(gate) * up).astype(BF16))
        for item in after_chunk.get(ci, []):
            attention_item(*item)
    x3 = x2 + 0.5 * jnp.dot(jnp.concatenate(acts, axis=1), wd_ref[...], preferred_element_type=F32)
    out_ref[...] = _rms(x3, fn_ref[...]) if final else x3

    for j in range(n_q):
        def cond(state):
            kb, m = state
            return jnp.logical_and(kb >= 0, m > LOG_WEIGHT_UNDERFLOW)

        def body(state, j=j):
            kb, _ = state
            a, carry, v2 = block_weights(j, kb, 0, carry_ref[j], diagonal=False)
            apply_values(j, [a], [v2], carry, first=False)
            return kb - 1, max_ref[j]

        lax.while_loop(cond, body, (first_block + j - FIRST_PASS_BLOCKS, max_ref[j]))
        for p in range(N_PAIRS):
            ysb_ref[slot, j * tq:(j + 1) * tq, p * LANES:(p + 1) * LANES] = acc_ref[j, p]


def _resident(shape):
    return pl.BlockSpec(shape, lambda *_: (0,) * len(shape), pipeline_mode=pl.Buffered(1))


def _rows(tm, width):
    return pl.BlockSpec((tm, width), lambda i: (i, 0))


def _slab_spec(shape, steps):
    rows, cols = shape
    slab = next(r for r in range(BF16_SUBLANES, rows + 1, BF16_SUBLANES) if rows % r == 0 and r * steps >= rows)
    last = rows // slab - 1
    return pl.BlockSpec((slab, cols), lambda i: (jnp.minimum(i, last), 0))


def _pre_call(x, n1, wg, wu, wd, nm, win, cw, cb, cn, cast, *, seq):
    n, d = x.shape
    tm = PRE_ROW_TILE
    steps = n // tm
    d_ff = wg.shape[1]
    d_conv = cw.shape[1]
    kern = functools.partial(_pre_kernel, tiles_per_seq=seq // tm, q_scale=1.0 / math.sqrt(SB_HEAD_DIM),
                             n_cast=len(cast))
    cast_specs = [_slab_spec(w.shape, steps) for w in cast]
    return pl.pallas_call(
        kern,
        grid=(steps,),
        in_specs=[_rows(tm, d), _resident((1, d)), _resident((d, d_ff)), _resident((d, d_ff)),
                  _resident((d_ff, d)), _resident((1, d)), _resident(win.shape),
                  _resident(cw.shape), _resident((1, d_conv)), _resident((1, d_conv))] + cast_specs,
        out_specs=[_rows(tm, d), _rows(tm, D_SB), _rows(tm, D_SB), _rows(tm, D_SB), _rows(tm, d_conv)]
                  + cast_specs,
        out_shape=[jax.ShapeDtypeStruct((n, d), F32)] + [jax.ShapeDtypeStruct((n, D_SB), BF16)] * 3
                  + [jax.ShapeDtypeStruct((n, d_conv), BF16)]
                  + [jax.ShapeDtypeStruct(w.shape, BF16) for w in cast],
        scratch_shapes=[pltpu.VMEM((tm + 2 * CONV_HALO, d_conv), F32)],
        compiler_params=pltpu.CompilerParams(dimension_semantics=("arbitrary",),
                                             vmem_limit_bytes=VMEM_LIMIT),
        name="ffn1_proj_conv",
    )(x, n1, wg, wu, wd, nm, win, cw, cb, cn, *cast)


def _suffix_sum_matrix(tk):
    j = jnp.arange(2 * tk)[:, None] % tk
    s = jnp.arange(2 * tk)[None, :]
    return -jnp.where(s < tk, j >= s, True).astype(BF16)


def _mix_post_call(q, k, v, x1, yc, sbn, wo, n2, wg, wu, wd, fn, *, seq, final):
    n, d = x1.shape
    tm = ROW_TILE
    tq = ATT_BLOCK
    n_tiles = n // tm
    tiles_per_seq = seq // tm
    d_ff = wg.shape[1]
    cur = lambda s: (jnp.minimum(s, n_tiles - 1), 0)
    prev = lambda s: (jnp.maximum(s - 1, 0), 0)
    seq_of_cur = lambda s: (jnp.minimum(s, n_tiles - 1) // tiles_per_seq, 0)
    kern = functools.partial(_mix_post_kernel, tiles_per_seq=tiles_per_seq, n_tiles=n_tiles, final=final)
    return pl.pallas_call(
        kern,
        grid=(n_tiles + 1,),
        in_specs=[pl.BlockSpec((tm, D_SB), cur),
                  pl.BlockSpec((seq, D_SB), seq_of_cur, pipeline_mode=pl.Buffered(1)),
                  pl.BlockSpec((seq, D_SB), seq_of_cur, pipeline_mode=pl.Buffered(1)),
                  _resident((2 * tq, 2 * tq)),
                  pl.BlockSpec((tm, d), prev), pl.BlockSpec((tm, yc.shape[1]), prev),
                  _resident((1, D_SB)), _resident(wo.shape), _resident((1, d)),
                  _resident((d, d_ff)), _resident((d, d_ff)), _resident((d_ff, d)), _resident((1, d))],
        out_specs=pl.BlockSpec((tm, d), prev),
        out_shape=jax.ShapeDtypeStruct((n, d), F32),
        scratch_shapes=[pltpu.VMEM((tm // tq, N_PAIRS, 2 * tq, LANES), BF16),
                        pltpu.VMEM((tm // tq, N_PAIRS * 2 * tq, tq), F32),
                        pltpu.VMEM((tm // tq, N_PAIRS, tq, LANES), F32),
                        pltpu.SMEM((tm // tq,), F32),
                        pltpu.VMEM((2, tm, D_SB), F32)],
        compiler_params=pltpu.CompilerParams(dimension_semantics=("arbitrary",),
                                             vmem_limit_bytes=VMEM_LIMIT),
        name="attention_out_proj_ffn2",
    )(q, k, v, _suffix_sum_matrix(tq), x1, yc, sbn, wo, n2, wg, wu, wd, fn)


def kernel(x, ffn1_norm, ffn1_w_gate, ffn1_w_up, ffn1_w_down, mix_norm, w_in, conv_w, conv_b,
           sb_out_norm, conv_out_norm, w_out, ffn2_norm, ffn2_w_gate, ffn2_w_up, ffn2_w_down,
           final_norm):
    batch, seq, d = x.shape
    depth = ffn1_norm.shape[0]
    assert seq % ROW_TILE == 0 and seq % PRE_ROW_TILE == 0 and w_in.shape[2] == 6 * D_SB
    row = lambda t: t.reshape(1, -1)
    xs = x.reshape(batch * seq, d)
    for l in range(depth):
        x1, q, k, v, yc, wo, wg2, wu2, wd2 = _pre_call(
            xs, row(ffn1_norm[l]), ffn1_w_gate[l].astype(BF16), ffn1_w_up[l].astype(BF16),
            ffn1_w_down[l].astype(BF16), row(mix_norm[l]), w_in[l].astype(BF16), conv_w[l],
            row(conv_b[l]), row(conv_out_norm[l]),
            (w_out[l], ffn2_w_gate[l], ffn2_w_up[l], ffn2_w_down[l]), seq=seq)
        xs = _mix_post_call(
            q, k, v, x1, yc, row(sb_out_norm[l]), wo, row(ffn2_norm[l]), wg2, wu2, wd2,
            row(final_norm), seq=seq, final=(l == depth - 1))
    return xs.reshape(batch, seq, d)
```

```python
import functools
import math

import jax
import jax.numpy as jnp
from jax import lax
from jax.experimental import pallas as pl
from jax.experimental.pallas import tpu as pltpu

F32 = jnp.float32
BF16 = jnp.bfloat16

EPS = 1e-6
SB_HEADS = 8
SB_HEAD_DIM = 64
D_SB = SB_HEADS * SB_HEAD_DIM
CONV_WIDTH = 3
LANES = 128
BF16_SUBLANES = 16
N_PAIRS = D_SB // LANES
PRE_ROW_TILE = 512
POST_ROW_TILE = 512
SUB_TILE = 256
FF_CHUNK = 256
ATT_BLOCK = 128
FIRST_PASS_BLOCKS = 3
CONV_HALO = 8
LOG_WEIGHT_UNDERFLOW = -104.0
VMEM_LIMIT = 56 * 1024 * 1024


def _rms(x, g):
    return x * lax.rsqrt(jnp.mean(x * x, axis=-1, keepdims=True) + EPS) * g


def _pre_kernel(x_ref, n1_ref, wg_ref, wu_ref, wd_ref, nm_ref, win_ref, cw_ref, cb_ref, cn_ref,
                *rest, tiles_per_seq, q_scale, n_cast):
    cast_src = rest[:n_cast]
    x1_ref, q_ref, k_ref, v_ref, yc_ref = rest[n_cast:n_cast + 5]
    cast_dst = rest[n_cast + 5:2 * n_cast + 5]
    xc_buf = rest[2 * n_cast + 5]
    tm = x_ref.shape[0]
    i = pl.program_id(0)
    for src, dst in zip(cast_src, cast_dst):
        dst[...] = src[...].astype(BF16)
    first = i % tiles_per_seq == 0

    @pl.when(first)
    def _():
        xc_buf[0:CONV_HALO, :] = jnp.zeros((CONV_HALO, xc_buf.shape[1]), F32)

    @pl.when(jnp.logical_not(first))
    def _():
        xc_buf[0:CONV_HALO, :] = xc_buf[tm:tm + CONV_HALO, :]

    d = D_SB
    subs = [slice(r, r + SUB_TILE) for r in range(0, tm, SUB_TILE)]
    xs = [x_ref[r, :] for r in subs]
    hs = [_rms(x, n1_ref[...]).astype(BF16) for x in xs]
    acts = []
    for h in hs:
        gate = jnp.dot(h, wg_ref[...], preferred_element_type=F32)
        up = jnp.dot(h, wu_ref[...], preferred_element_type=F32)
        acts.append((gate * jax.nn.sigmoid(gate) * up).astype(BF16))
    hs = []
    for r, x, act in zip(subs, xs, acts):
        x1 = x + 0.5 * jnp.dot(act, wd_ref[...], preferred_element_type=F32)
        x1_ref[r, :] = x1
        hs.append(_rms(x1, nm_ref[...]).astype(BF16))
    gates = []
    for r, h in zip(subs, hs):
        proj = jnp.dot(h, win_ref[...], preferred_element_type=F32)
        q_ref[r, :] = (proj[:, 0:d] * q_scale).astype(BF16)
        k_ref[r, :] = proj[:, d:2 * d].astype(BF16)
        v_ref[r, :] = proj[:, 2 * d:3 * d].astype(BF16)
        gates.append(proj[:, 3 * d:4 * d])
        xc_buf[CONV_HALO + r.start:CONV_HALO + r.stop, :] = proj[:, 4 * d:5 * d] * proj[:, 5 * d:6 * d]
    for r, gate_b in zip(subs, gates):
        y = cb_ref[...]
        for j in range(CONV_WIDTH):
            off = CONV_HALO - (CONV_WIDTH - 1) + j + r.start
            y = y + xc_buf[off:off + SUB_TILE, :] * cw_ref[j:j + 1, :]
        yc_ref[r, :] = _rms(gate_b * y, cn_ref[...]).astype(BF16)


def _mix_post_kernel(q_ref, kc_ref, kp_ref, vc_ref, vp_ref, k_hbm, v_hbm, tri_ref, x1_ref, yc_ref,
                     sbn_ref, wo_ref, n2_ref, wg_ref, wu_ref, wd_ref, fn_ref, out_ref,
                     qs_ref, carry_ref, acc_ref, max_ref, ysb_ref, kbuf, vbuf, sem,
                     *, tiles_per_seq, n_tiles, final):
    tm = q_ref.shape[0]
    tq = ATT_BLOCK
    n_q = tm // tq
    rows = N_PAIRS * 2 * tq
    s = pl.program_id(0)
    slot = s % 2
    tile = jnp.minimum(s, n_tiles - 1)
    tile_in_seq = tile % tiles_per_seq
    first_block = tile_in_seq * n_q
    seq_row0 = (tile - tile_in_seq) * tm

    @pl.when(s == 0)
    def _():
        ysb_ref[1] = jnp.zeros(ysb_ref.shape[1:], F32)

    lane = lax.broadcasted_iota(jnp.int32, (tq, LANES), 1)
    low = lane < SB_HEAD_DIM
    row = lax.broadcasted_iota(jnp.int32, (rows, tq), 0)
    col = lax.broadcasted_iota(jnp.int32, (rows, tq), 1)
    strict = col < (row & (tq - 1))
    tri_col = lax.broadcasted_iota(jnp.int32, tri_ref.shape, 1)
    zero = jnp.zeros((), BF16)
    prev_shift = jnp.where(tile_in_seq == 0, 2 * tq, 0)

    def block_weights(j, k_blk, v_blk, shift, carry, diagonal):
        if shift is None:
            keep_k = lo_half = hi_half = tri = None
        else:
            k_lane = lane + shift
            keep_k, lo_half = k_lane < LANES, k_lane < SB_HEAD_DIM
            hi_half = (k_lane >= SB_HEAD_DIM) & keep_k
            tri = jnp.where(tri_col + shift < 2 * tq, tri_ref[...], zero)
        z = jnp.concatenate(
            [lax.dot_general(qs_ref[j, p], k_blk(p) if shift is None else jnp.where(keep_k, k_blk(p), zero),
                             (((1,), (1,)), ((), ())), preferred_element_type=F32)
             for p in range(N_PAIRS)], axis=0)
        sp = jnp.maximum(z, 0.0) + jnp.log(1.0 + jnp.exp(-jnp.abs(z)))
        if diagonal:
            sp = jnp.where(strict, sp, 0.0)
        hi = sp.astype(BF16)
        lo = (sp - hi.astype(F32)).astype(BF16)
        sums = jnp.dot(jnp.concatenate([hi, lo], axis=1), tri_ref[...] if shift is None else tri,
                       preferred_element_type=F32)
        logit = z + sums[:, 0:tq]
        if carry is not None:
            logit = logit + carry
        a = jnp.exp(logit)
        if diagonal:
            a = jnp.where(strict, a, 0.0)
        carry = sums[:, tq:2 * tq] if carry is None else carry + sums[:, tq:2 * tq]
        halves = (low, ~low) if shift is None else (lo_half, hi_half)
        v2 = [jnp.concatenate([jnp.where(m, v_blk(p), zero) for m in halves], axis=0)
              for p in range(N_PAIRS)]
        return a.astype(BF16), carry, v2

    def apply_values(j, weights, values, carry, first):
        carry_ref[j] = carry
        for p in range(N_PAIRS):
            r0 = p * 2 * tq
            a2 = jnp.concatenate([w[r:r + tq] for w in weights for r in (r0, r0 + tq)], axis=1)
            v2 = jnp.concatenate([v[p] for v in values], axis=0)
            out = jnp.dot(a2, v2, preferred_element_type=F32)
            acc_ref[j, p] = out if first else acc_ref[j, p] + out
        max_ref[j] = jnp.max(carry)

    for j in range(n_q):
        for p in range(N_PAIRS):
            q2 = q_ref[j * tq:(j + 1) * tq, p * LANES:(p + 1) * LANES]
            qs_ref[j, p] = jnp.concatenate([jnp.where(low, q2, zero), jnp.where(low, zero, q2)], axis=0)

    state = {}

    def attention_item(j, b):
        weights, values, carry = state.get(j, ([], [], None))
        if j >= b:
            k_src, v_src, r0, shift = kc_ref, vc_ref, (j - b) * tq, None
        else:
            k_src, v_src, r0, shift = kp_ref, vp_ref, (n_q + j - b) * tq, prev_shift
        a, carry, v2 = block_weights(
            j, lambda p: k_src[r0:r0 + tq, p * LANES:(p + 1) * LANES],
            lambda p: v_src[r0:r0 + tq, p * LANES:(p + 1) * LANES], shift, carry, diagonal=(b == 0))
        state[j] = (weights + [a], values + [v2], carry)
        if b == FIRST_PASS_BLOCKS - 1:
            apply_values(j, *state[j], first=True)

    items = [(j, b) for j in range(n_q) for b in range(FIRST_PASS_BLOCKS)]
    d_ff = wg_ref.shape[1]
    chunks = [slice(c, c + FF_CHUNK) for c in range(0, d_ff, FF_CHUNK)]
    subs = [slice(r, r + SUB_TILE) for r in range(0, tm, SUB_TILE)]
    slots = [(t, c) for t in range(len(subs)) for c in range(len(chunks))]
    after_slot = {}
    for n, item in enumerate(items):
        after_slot.setdefault(slots[n * len(slots) // len(items)], []).append(item)

    x2s, hs = [], []
    for r in subs:
        ysn = _rms(ysb_ref[1 - slot, r, :], sbn_ref[...]).astype(BF16)
        y = jnp.concatenate([ysn, yc_ref[r, :]], axis=1)
        x2s.append(x1_ref[r, :] + jnp.dot(y, wo_ref[...], preferred_element_type=F32))
        hs.append(_rms(x2s[-1], n2_ref[...]).astype(BF16))
    acts = [[] for _ in subs]
    for t, h in enumerate(hs):
        for ci, c in enumerate(chunks):
            gate = jnp.dot(h, wg_ref[:, c], preferred_element_type=F32)
            up = jnp.dot(h, wu_ref[:, c], preferred_element_type=F32)
            acts[t].append((gate * jax.nn.sigmoid(gate) * up).astype(BF16))
            for item in after_slot.get((t, ci), []):
                attention_item(*item)
    for r, x2, act in zip(subs, x2s, acts):
        x3 = x2 + 0.5 * jnp.dot(jnp.concatenate(act, axis=1), wd_ref[...], preferred_element_type=F32)
        out_ref[r, :] = _rms(x3, fn_ref[...]) if final else x3

    for j in range(n_q):
        def cond(state):
            kb, m = state
            return jnp.logical_and(kb >= 0, m > LOG_WEIGHT_UNDERFLOW)

        def body(state, j=j):
            kb, _ = state
            r0 = pl.multiple_of(seq_row0 + kb * tq, tq)
            copies = [pltpu.make_async_copy(src.at[pl.ds(r0, tq), :], dst, sem.at[n])
                      for n, (src, dst) in enumerate(((k_hbm, kbuf), (v_hbm, vbuf)))]
            for cp in copies:
                cp.start()
            for cp in copies:
                cp.wait()
            a, carry, v2 = block_weights(
                j, lambda p: kbuf[:, p * LANES:(p + 1) * LANES], lambda p: vbuf[:, p * LANES:(p + 1) * LANES],
                None, carry_ref[j], diagonal=False)
            apply_values(j, [a], [v2], carry, first=False)
            return kb - 1, max_ref[j]

        lax.while_loop(cond, body, (first_block + j - FIRST_PASS_BLOCKS, max_ref[j]))
        for p in range(N_PAIRS):
            ysb_ref[slot, j * tq:(j + 1) * tq, p * LANES:(p + 1) * LANES] = acc_ref[j, p]


def _resident(shape):
    return pl.BlockSpec(shape, lambda *_: (0,) * len(shape), pipeline_mode=pl.Buffered(1))


def _rows(tm, width):
    return pl.BlockSpec((tm, width), lambda i: (i, 0))


def _slab_spec(shape, steps):
    rows, cols = shape
    slab = next(r for r in range(BF16_SUBLANES, rows + 1, BF16_SUBLANES) if rows % r == 0 and r * steps >= rows)
    last = rows // slab - 1
    return pl.BlockSpec((slab, cols), lambda i: (jnp.minimum(i, last), 0))


def _pre_call(x, n1, wg, wu, wd, nm, win, cw, cb, cn, cast, *, seq):
    n, d = x.shape
    tm = PRE_ROW_TILE
    steps = n // tm
    d_ff = wg.shape[1]
    d_conv = cw.shape[1]
    kern = functools.partial(_pre_kernel, tiles_per_seq=seq // tm, q_scale=1.0 / math.sqrt(SB_HEAD_DIM),
                             n_cast=len(cast))
    cast_specs = [_slab_spec(w.shape, steps) for w in cast]
    return pl.pallas_call(
        kern,
        grid=(steps,),
        in_specs=[_rows(tm, d), _resident((1, d)), _resident((d, d_ff)), _resident((d, d_ff)),
                  _resident((d_ff, d)), _resident((1, d)), _resident(win.shape),
                  _resident(cw.shape), _resident((1, d_conv)), _resident((1, d_conv))] + cast_specs,
        out_specs=[_rows(tm, d), _rows(tm, D_SB), _rows(tm, D_SB), _rows(tm, D_SB), _rows(tm, d_conv)]
                  + cast_specs,
        out_shape=[jax.ShapeDtypeStruct((n, d), F32)] + [jax.ShapeDtypeStruct((n, D_SB), BF16)] * 3
                  + [jax.ShapeDtypeStruct((n, d_conv), BF16)]
                  + [jax.ShapeDtypeStruct(w.shape, BF16) for w in cast],
        scratch_shapes=[pltpu.VMEM((tm + 2 * CONV_HALO, d_conv), F32)],
        compiler_params=pltpu.CompilerParams(dimension_semantics=("arbitrary",),
                                             vmem_limit_bytes=VMEM_LIMIT),
        name="ffn1_proj_conv",
    )(x, n1, wg, wu, wd, nm, win, cw, cb, cn, *cast)


def _suffix_sum_matrix(tk):
    j = jnp.arange(2 * tk)[:, None] % tk
    s = jnp.arange(2 * tk)[None, :]
    return -jnp.where(s < tk, j >= s, True).astype(BF16)


def _mix_post_call(q, k, v, x1, yc, sbn, wo, n2, wg, wu, wd, fn, *, seq, final):
    n, d = x1.shape
    tm = POST_ROW_TILE
    tq = ATT_BLOCK
    n_tiles = n // tm
    tiles_per_seq = seq // tm
    d_ff = wg.shape[1]
    cur = lambda s: (jnp.minimum(s, n_tiles - 1), 0)
    before_cur = lambda s: (jnp.maximum(jnp.minimum(s, n_tiles - 1) - 1, 0), 0)
    prev = lambda s: (jnp.maximum(s - 1, 0), 0)
    kern = functools.partial(_mix_post_kernel, tiles_per_seq=tiles_per_seq, n_tiles=n_tiles, final=final)
    any_space = pl.BlockSpec(memory_space=pl.ANY)
    return pl.pallas_call(
        kern,
        grid=(n_tiles + 1,),
        in_specs=[pl.BlockSpec((tm, D_SB), cur),
                  pl.BlockSpec((tm, D_SB), cur), pl.BlockSpec((tm, D_SB), before_cur),
                  pl.BlockSpec((tm, D_SB), cur), pl.BlockSpec((tm, D_SB), before_cur),
                  any_space, any_space,
                  _resident((2 * tq, 2 * tq)),
                  pl.BlockSpec((tm, d), prev), pl.BlockSpec((tm, yc.shape[1]), prev),
                  _resident((1, D_SB)), _resident(wo.shape), _resident((1, d)),
                  _resident((d, d_ff)), _resident((d, d_ff)), _resident((d_ff, d)), _resident((1, d))],
        out_specs=pl.BlockSpec((tm, d), prev),
        out_shape=jax.ShapeDtypeStruct((n, d), F32),
        scratch_shapes=[pltpu.VMEM((tm // tq, N_PAIRS, 2 * tq, LANES), BF16),
                        pltpu.VMEM((tm // tq, N_PAIRS * 2 * tq, tq), F32),
                        pltpu.VMEM((tm // tq, N_PAIRS, tq, LANES), F32),
                        pltpu.SMEM((tm // tq,), F32),
                        pltpu.VMEM((2, tm, D_SB), F32),
                        pltpu.VMEM((tq, D_SB), BF16), pltpu.VMEM((tq, D_SB), BF16),
                        pltpu.SemaphoreType.DMA((2,))],
        compiler_params=pltpu.CompilerParams(dimension_semantics=("arbitrary",),
                                             vmem_limit_bytes=VMEM_LIMIT),
        name="attention_out_proj_ffn2",
    )(q, k, k, v, v, k, v, _suffix_sum_matrix(tq), x1, yc, sbn, wo, n2, wg, wu, wd, fn)


def kernel(x, ffn1_norm, ffn1_w_gate, ffn1_w_up, ffn1_w_down, mix_norm, w_in, conv_w, conv_b,
           sb_out_norm, conv_out_norm, w_out, ffn2_norm, ffn2_w_gate, ffn2_w_up, ffn2_w_down,
           final_norm):
    batch, seq, d = x.shape
    depth = ffn1_norm.shape[0]
    assert seq % POST_ROW_TILE == 0 and seq % PRE_ROW_TILE == 0 and w_in.shape[2] == 6 * D_SB
    row = lambda t: t.reshape(1, -1)
    xs = x.reshape(batch * seq, d)
    for l in range(depth):
        x1, q, k, v, yc, wo, wg2, wu2, wd2 = _pre_call(
            xs, row(ffn1_norm[l]), ffn1_w_gate[l].astype(BF16), ffn1_w_up[l].astype(BF16),
            ffn1_w_down[l].astype(BF16), row(mix_norm[l]), w_in[l].astype(BF16), conv_w[l],
            row(conv_b[l]), row(conv_out_norm[l]),
            (w_out[l], ffn2_w_gate[l], ffn2_w_up[l], ffn2_w_down[l]), seq=seq)
        xs = _mix_post_call(
            q, k, v, x1, yc, row(sb_out_norm[l]), wo, row(ffn2_norm[l]), wg2, wu2, wd2,
            row(final_norm), seq=seq, final=(l == depth - 1))
    return xs.reshape(batch, seq, d)
```

```python
import functools
import math

import jax
import jax.numpy as jnp
from jax import lax
from jax.experimental import pallas as pl
from jax.experimental.pallas import tpu as pltpu

F32 = jnp.float32
BF16 = jnp.bfloat16

EPS = 1e-6
SB_HEADS = 8
SB_HEAD_DIM = 64
D_SB = SB_HEADS * SB_HEAD_DIM
CONV_WIDTH = 3
LANES = 128
BF16_SUBLANES = 16
N_PAIRS = D_SB // LANES
PRE_ROW_TILE = 512
POST_ROW_TILE = 512
SUB_TILE = 256
FF_CHUNK = 256
ATT_BLOCK = 128
FIRST_PASS_BLOCKS = 3
STAGE_ROWS = 128
CONV_HALO = 8
LOG_WEIGHT_UNDERFLOW = -104.0
VMEM_LIMIT = 56 * 1024 * 1024


def _rms(x, g):
    return x * lax.rsqrt(jnp.mean(x * x, axis=-1, keepdims=True) + EPS) * g


def _load_as_bf16(pairs, stage, sem):
    slab = stage.shape[1]
    jobs = [(src, dst, r) for src, dst in pairs for r in range(0, src.shape[0], slab)]

    def copy(n):
        src, _, r = jobs[n]
        return pltpu.make_async_copy(src.at[r:r + slab, :], stage.at[n % 2, :, 0:src.shape[1]], sem.at[n % 2])

    copy(0).start()
    for n, (src, dst, r) in enumerate(jobs):
        if n + 1 < len(jobs):
            copy(n + 1).start()
        copy(n).wait()
        dst[r:r + slab, :] = stage[n % 2, :, 0:src.shape[1]].astype(BF16)


def _pre_kernel(x_ref, n1_ref, wg_hbm, wu_hbm, wd_hbm, nm_ref, win_hbm, cw_ref, cb_ref, cn_ref,
                *rest, tiles_per_seq, q_scale, n_cast):
    cast_src = rest[:n_cast]
    x1_ref, q_ref, k_ref, v_ref, yc_ref = rest[n_cast:n_cast + 5]
    cast_dst = rest[n_cast + 5:2 * n_cast + 5]
    xc_buf, wg_ref, wu_ref, wd_ref, win_ref, stage, sem = rest[2 * n_cast + 5:]
    tm = x_ref.shape[0]
    i = pl.program_id(0)

    @pl.when(i == 0)
    def _():
        _load_as_bf16([(wg_hbm, wg_ref), (wu_hbm, wu_ref), (wd_hbm, wd_ref), (win_hbm, win_ref)], stage, sem)

    for src, dst in zip(cast_src, cast_dst):
        dst[...] = src[...].astype(BF16)
    first = i % tiles_per_seq == 0

    @pl.when(first)
    def _():
        xc_buf[0:CONV_HALO, :] = jnp.zeros((CONV_HALO, xc_buf.shape[1]), F32)

    @pl.when(jnp.logical_not(first))
    def _():
        xc_buf[0:CONV_HALO, :] = xc_buf[tm:tm + CONV_HALO, :]

    d = D_SB
    subs = [slice(r, r + SUB_TILE) for r in range(0, tm, SUB_TILE)]
    xs = [x_ref[r, :] for r in subs]
    hs = [_rms(x, n1_ref[...]).astype(BF16) for x in xs]
    acts = []
    for h in hs:
        gate = jnp.dot(h, wg_ref[...], preferred_element_type=F32)
        up = jnp.dot(h, wu_ref[...], preferred_element_type=F32)
        acts.append((gate * jax.nn.sigmoid(gate) * up).astype(BF16))
    hs = []
    for r, x, act in zip(subs, xs, acts):
        x1 = x + 0.5 * jnp.dot(act, wd_ref[...], preferred_element_type=F32)
        x1_ref[r, :] = x1
        hs.append(_rms(x1, nm_ref[...]).astype(BF16))
    gates = []
    for r, h in zip(subs, hs):
        proj = jnp.dot(h, win_ref[...], preferred_element_type=F32)
        q_ref[r, :] = (proj[:, 0:d] * q_scale).astype(BF16)
        k_ref[r, :] = proj[:, d:2 * d].astype(BF16)
        v_ref[r, :] = proj[:, 2 * d:3 * d].astype(BF16)
        gates.append(proj[:, 3 * d:4 * d])
        xc_buf[CONV_HALO + r.start:CONV_HALO + r.stop, :] = proj[:, 4 * d:5 * d] * proj[:, 5 * d:6 * d]
    for r, gate_b in zip(subs, gates):
        y = cb_ref[...]
        for j in range(CONV_WIDTH):
            off = CONV_HALO - (CONV_WIDTH - 1) + j + r.start
            y = y + xc_buf[off:off + SUB_TILE, :] * cw_ref[j:j + 1, :]
        yc_ref[r, :] = _rms(gate_b * y, cn_ref[...]).astype(BF16)


def _mix_post_kernel(q_ref, kc_ref, kp_ref, vc_ref, vp_ref, k_hbm, v_hbm, tri_ref, x1_ref, yc_ref,
                     sbn_ref, wo_ref, n2_ref, wg_ref, wu_ref, wd_ref, fn_ref, out_ref,
                     qs_ref, carry_ref, acc_ref, max_ref, ysb_ref, kbuf, vbuf, sem,
                     *, tiles_per_seq, n_tiles, final):
    tm = q_ref.shape[0]
    tq = ATT_BLOCK
    n_q = tm // tq
    rows = N_PAIRS * 2 * tq
    s = pl.program_id(0)
    slot = s % 2
    tile = jnp.minimum(s, n_tiles - 1)
    tile_in_seq = tile % tiles_per_seq
    first_block = tile_in_seq * n_q
    seq_row0 = (tile - tile_in_seq) * tm

    @pl.when(s == 0)
    def _():
        ysb_ref[1] = jnp.zeros(ysb_ref.shape[1:], F32)

    lane = lax.broadcasted_iota(jnp.int32, (tq, LANES), 1)
    low = lane < SB_HEAD_DIM
    row = lax.broadcasted_iota(jnp.int32, (rows, tq), 0)
    col = lax.broadcasted_iota(jnp.int32, (rows, tq), 1)
    strict = col < (row & (tq - 1))
    tri_col = lax.broadcasted_iota(jnp.int32, tri_ref.shape, 1)
    zero = jnp.zeros((), BF16)
    prev_shift = jnp.where(tile_in_seq == 0, 2 * tq, 0)

    def block_weights(j, k_blk, v_blk, shift, carry, diagonal):
        if shift is None:
            keep_k = lo_half = hi_half = tri = None
        else:
            k_lane = lane + shift
            keep_k, lo_half = k_lane < LANES, k_lane < SB_HEAD_DIM
            hi_half = (k_lane >= SB_HEAD_DIM) & keep_k
            tri = jnp.where(tri_col + shift < 2 * tq, tri_ref[...], zero)
        z = jnp.concatenate(
            [lax.dot_general(qs_ref[j, p], k_blk(p) if shift is None else jnp.where(keep_k, k_blk(p), zero),
                             (((1,), (1,)), ((), ())), preferred_element_type=F32)
             for p in range(N_PAIRS)], axis=0)
        sp = jnp.maximum(z, 0.0) + jnp.log(1.0 + jnp.exp(-jnp.abs(z)))
        if diagonal:
            sp = jnp.where(strict, sp, 0.0)
        hi = sp.astype(BF16)
        lo = (sp - hi.astype(F32)).astype(BF16)
        sums = jnp.dot(jnp.concatenate([hi, lo], axis=1), tri_ref[...] if shift is None else tri,
                       preferred_element_type=F32)
        logit = z + sums[:, 0:tq]
        if carry is not None:
            logit = logit + carry
        a = jnp.exp(logit)
        if diagonal:
            a = jnp.where(strict, a, 0.0)
        carry = sums[:, tq:2 * tq] if carry is None else carry + sums[:, tq:2 * tq]
        halves = (low, ~low) if shift is None else (lo_half, hi_half)
        v2 = [jnp.concatenate([jnp.where(m, v_blk(p), zero) for m in halves], axis=0)
              for p in range(N_PAIRS)]
        return a.astype(BF16), carry, v2

    def apply_values(j, weights, values, carry, first):
        carry_ref[j] = carry
        for p in range(N_PAIRS):
            r0 = p * 2 * tq
            a2 = jnp.concatenate([w[r:r + tq] for w in weights for r in (r0, r0 + tq)], axis=1)
            v2 = jnp.concatenate([v[p] for v in values], axis=0)
            out = jnp.dot(a2, v2, preferred_element_type=F32)
            acc_ref[j, p] = out if first else acc_ref[j, p] + out
        max_ref[j] = jnp.max(carry)

    for j in range(n_q):
        for p in range(N_PAIRS):
            q2 = q_ref[j * tq:(j + 1) * tq, p * LANES:(p + 1) * LANES]
            qs_ref[j, p] = jnp.concatenate([jnp.where(low, q2, zero), jnp.where(low, zero, q2)], axis=0)

    state = {}

    def attention_item(j, b):
        weights, values, carry = state.get(j, ([], [], None))
        if j >= b:
            k_src, v_src, r0, shift = kc_ref, vc_ref, (j - b) * tq, None
        else:
            k_src, v_src, r0, shift = kp_ref, vp_ref, (n_q + j - b) * tq, prev_shift
        a, carry, v2 = block_weights(
            j, lambda p: k_src[r0:r0 + tq, p * LANES:(p + 1) * LANES],
            lambda p: v_src[r0:r0 + tq, p * LANES:(p + 1) * LANES], shift, carry, diagonal=(b == 0))
        state[j] = (weights + [a], values + [v2], carry)
        if b == FIRST_PASS_BLOCKS - 1:
            apply_values(j, *state[j], first=True)

    items = [(j, b) for j in range(n_q) for b in range(FIRST_PASS_BLOCKS)]
    d_ff = wg_ref.shape[1]
    chunks = [slice(c, c + FF_CHUNK) for c in range(0, d_ff, FF_CHUNK)]
    subs = [slice(r, r + SUB_TILE) for r in range(0, tm, SUB_TILE)]
    slots = [(t, c) for t in range(len(subs)) for c in range(len(chunks))]
    after_slot = {}
    for n, item in enumerate(items):
        after_slot.setdefault(slots[n * len(slots) // len(items)], []).append(item)

    x2s, hs = [], []
    for r in subs:
        ysn = _rms(ysb_ref[1 - slot, r, :], sbn_ref[...]).astype(BF16)
        y = jnp.concatenate([ysn, yc_ref[r, :]], axis=1)
        x2s.append(x1_ref[r, :] + jnp.dot(y, wo_ref[...], preferred_element_type=F32))
        hs.append(_rms(x2s[-1], n2_ref[...]).astype(BF16))
    acts = [[] for _ in subs]
    for t, h in enumerate(hs):
        for ci, c in enumerate(chunks):
            gate = jnp.dot(h, wg_ref[:, c], preferred_element_type=F32)
            up = jnp.dot(h, wu_ref[:, c], preferred_element_type=F32)
            acts[t].append((gate * jax.nn.sigmoid(gate) * up).astype(BF16))
            for item in after_slot.get((t, ci), []):
                attention_item(*item)
    for r, x2, act in zip(subs, x2s, acts):
        x3 = x2 + 0.5 * jnp.dot(jnp.concatenate(act, axis=1), wd_ref[...], preferred_element_type=F32)
        out_ref[r, :] = _rms(x3, fn_ref[...]) if final else x3

    for j in range(n_q):
        def cond(state):
            kb, m = state
            return jnp.logical_and(kb >= 0, m > LOG_WEIGHT_UNDERFLOW)

        def body(state, j=j):
            kb, _ = state
            r0 = pl.multiple_of(seq_row0 + kb * tq, tq)
            copies = [pltpu.make_async_copy(src.at[pl.ds(r0, tq), :], dst, sem.at[n])
                      for n, (src, dst) in enumerate(((k_hbm, kbuf), (v_hbm, vbuf)))]
            for cp in copies:
                cp.start()
            for cp in copies:
                cp.wait()
            a, carry, v2 = block_weights(
                j, lambda p: kbuf[:, p * LANES:(p + 1) * LANES], lambda p: vbuf[:, p * LANES:(p + 1) * LANES],
                None, carry_ref[j], diagonal=False)
            apply_values(j, [a], [v2], carry, first=False)
            return kb - 1, max_ref[j]

        lax.while_loop(cond, body, (first_block + j - FIRST_PASS_BLOCKS, max_ref[j]))
        for p in range(N_PAIRS):
            ysb_ref[slot, j * tq:(j + 1) * tq, p * LANES:(p + 1) * LANES] = acc_ref[j, p]


def _resident(shape):
    return pl.BlockSpec(shape, lambda *_: (0,) * len(shape), pipeline_mode=pl.Buffered(1))


def _rows(tm, width):
    return pl.BlockSpec((tm, width), lambda i: (i, 0))


def _slab_spec(shape, steps):
    rows, cols = shape
    slab = next(r for r in range(BF16_SUBLANES, rows + 1, BF16_SUBLANES) if rows % r == 0 and r * steps >= rows)
    last = rows // slab - 1
    return pl.BlockSpec((slab, cols), lambda i: (jnp.minimum(i, last), 0))


def _pre_call(x, n1, wg, wu, wd, nm, win, cw, cb, cn, cast, *, seq):
    n, d = x.shape
    tm = PRE_ROW_TILE
    steps = n // tm
    d_ff = wg.shape[1]
    d_conv = cw.shape[1]
    kern = functools.partial(_pre_kernel, tiles_per_seq=seq // tm, q_scale=1.0 / math.sqrt(SB_HEAD_DIM),
                             n_cast=len(cast))
    cast_specs = [_slab_spec(w.shape, steps) for w in cast]
    hbm = pl.BlockSpec(memory_space=pl.ANY)
    own = (wg, wu, wd, win)
    assert all(w.shape[0] % STAGE_ROWS == 0 for w in own)
    stage_cols = max(w.shape[1] for w in own)
    return pl.pallas_call(
        kern,
        grid=(steps,),
        in_specs=[_rows(tm, d), _resident((1, d)), hbm, hbm, hbm, _resident((1, d)), hbm,
                  _resident(cw.shape), _resident((1, d_conv)), _resident((1, d_conv))] + cast_specs,
        out_specs=[_rows(tm, d), _rows(tm, D_SB), _rows(tm, D_SB), _rows(tm, D_SB), _rows(tm, d_conv)]
                  + cast_specs,
        out_shape=[jax.ShapeDtypeStruct((n, d), F32)] + [jax.ShapeDtypeStruct((n, D_SB), BF16)] * 3
                  + [jax.ShapeDtypeStruct((n, d_conv), BF16)]
                  + [jax.ShapeDtypeStruct(w.shape, BF16) for w in cast],
        scratch_shapes=[pltpu.VMEM((tm + 2 * CONV_HALO, d_conv), F32)]
                       + [pltpu.VMEM(w.shape, BF16) for w in own]
                       + [pltpu.VMEM((2, STAGE_ROWS, stage_cols), F32), pltpu.SemaphoreType.DMA((2,))],
        compiler_params=pltpu.CompilerParams(dimension_semantics=("arbitrary",),
                                             vmem_limit_bytes=VMEM_LIMIT),
        name="ffn1_proj_conv",
    )(x, n1, wg, wu, wd, nm, win, cw, cb, cn, *cast)


def _suffix_sum_matrix(tk):
    j = jnp.arange(2 * tk)[:, None] % tk
    s = jnp.arange(2 * tk)[None, :]
    return -jnp.where(s < tk, j >= s, True).astype(BF16)


def _mix_post_call(q, k, v, x1, yc, sbn, wo, n2, wg, wu, wd, fn, *, seq, final):
    n, d = x1.shape
    tm = POST_ROW_TILE
    tq = ATT_BLOCK
    n_tiles = n // tm
    tiles_per_seq = seq // tm
    d_ff = wg.shape[1]
    cur = lambda s: (jnp.minimum(s, n_tiles - 1), 0)
    before_cur = lambda s: (jnp.maximum(jnp.minimum(s, n_tiles - 1) - 1, 0), 0)
    prev = lambda s: (jnp.maximum(s - 1, 0), 0)
    kern = functools.partial(_mix_post_kernel, tiles_per_seq=tiles_per_seq, n_tiles=n_tiles, final=final)
    any_space = pl.BlockSpec(memory_space=pl.ANY)
    return pl.pallas_call(
        kern,
        grid=(n_tiles + 1,),
        in_specs=[pl.BlockSpec((tm, D_SB), cur),
                  pl.BlockSpec((tm, D_SB), cur), pl.BlockSpec((tm, D_SB), before_cur),
                  pl.BlockSpec((tm, D_SB), cur), pl.BlockSpec((tm, D_SB), before_cur),
                  any_space, any_space,
                  _resident((2 * tq, 2 * tq)),
                  pl.BlockSpec((tm, d), prev), pl.BlockSpec((tm, yc.shape[1]), prev),
                  _resident((1, D_SB)), _resident(wo.shape), _resident((1, d)),
                  _resident((d, d_ff)), _resident((d, d_ff)), _resident((d_ff, d)), _resident((1, d))],
        out_specs=pl.BlockSpec((tm, d), prev),
        out_shape=jax.ShapeDtypeStruct((n, d), F32),
        scratch_shapes=[pltpu.VMEM((tm // tq, N_PAIRS, 2 * tq, LANES), BF16),
                        pltpu.VMEM((tm // tq, N_PAIRS * 2 * tq, tq), F32),
                        pltpu.VMEM((tm // tq, N_PAIRS, tq, LANES), F32),
                        pltpu.SMEM((tm // tq,), F32),
                        pltpu.VMEM((2, tm, D_SB), F32),
                        pltpu.VMEM((tq, D_SB), BF16), pltpu.VMEM((tq, D_SB), BF16),
                        pltpu.SemaphoreType.DMA((2,))],
        compiler_params=pltpu.CompilerParams(dimension_semantics=("arbitrary",),
                                             vmem_limit_bytes=VMEM_LIMIT),
        name="attention_out_proj_ffn2",
    )(q, k, k, v, v, k, v, _suffix_sum_matrix(tq), x1, yc, sbn, wo, n2, wg, wu, wd, fn)


def kernel(x, ffn1_norm, ffn1_w_gate, ffn1_w_up, ffn1_w_down, mix_norm, w_in, conv_w, conv_b,
           sb_out_norm, conv_out_norm, w_out, ffn2_norm, ffn2_w_gate, ffn2_w_up, ffn2_w_down,
           final_norm):
    batch, seq, d = x.shape
    depth = ffn1_norm.shape[0]
    assert seq % POST_ROW_TILE == 0 and seq % PRE_ROW_TILE == 0 and w_in.shape[2] == 6 * D_SB
    row = lambda t: t.reshape(1, -1)
    xs = x.reshape(batch * seq, d)
    for l in range(depth):
        x1, q, k, v, yc, wo, wg2, wu2, wd2 = _pre_call(
            xs, row(ffn1_norm[l]), ffn1_w_gate[l], ffn1_w_up[l], ffn1_w_down[l],
            row(mix_norm[l]), w_in[l], conv_w[l],
            row(conv_b[l]), row(conv_out_norm[l]),
            (w_out[l], ffn2_w_gate[l], ffn2_w_up[l], ffn2_w_down[l]), seq=seq)
        xs = _mix_post_call(
            q, k, v, x1, yc, row(sb_out_norm[l]), wo, row(ffn2_norm[l]), wg2, wu2, wd2,
            row(final_norm), seq=seq, final=(l == depth - 1))
    return xs.reshape(batch, seq, d)
```

```python
import functools
import math

import jax
import jax.numpy as jnp
from jax import lax
from jax.experimental import pallas as pl
from jax.experimental.pallas import tpu as pltpu

F32 = jnp.float32
BF16 = jnp.bfloat16

EPS = 1e-6
SB_HEADS = 8
SB_HEAD_DIM = 64
D_SB = SB_HEADS * SB_HEAD_DIM
CONV_WIDTH = 3
LANES = 128
BF16_SUBLANES = 16
N_PAIRS = D_SB // LANES
PRE_ROW_TILE = 512
POST_ROW_TILE = 512
SUB_TILE = 256
FF_CHUNK = 256
ATT_BLOCK = 128
FIRST_PASS_BLOCKS = 3
STAGE_ROWS = 128
STAGE_SLOTS = 4
CONV_HALO = 8
LOG_WEIGHT_UNDERFLOW = -104.0
VMEM_LIMIT = 56 * 1024 * 1024


def _rms(x, g):
    return x * lax.rsqrt(jnp.mean(x * x, axis=-1, keepdims=True) + EPS) * g


def _load_as_bf16(pairs, stage, sem):
    slots, slab = stage.shape[0], stage.shape[1]
    jobs = [(src, dst, r) for src, dst in pairs for r in range(0, src.shape[0], slab)]

    def copy(n):
        src, _, r = jobs[n]
        k = n % slots
        return pltpu.make_async_copy(src.at[r:r + slab, :], stage.at[k, :, 0:src.shape[1]], sem.at[k])

    for n in range(min(slots - 1, len(jobs))):
        copy(n).start()
    for n, (src, dst, r) in enumerate(jobs):
        if n + slots - 1 < len(jobs):
            copy(n + slots - 1).start()
        copy(n).wait()
        dst[r:r + slab, :] = stage[n % slots, :, 0:src.shape[1]].astype(BF16)


def _pre_kernel(x_ref, n1_ref, wg_hbm, wu_hbm, wd_hbm, nm_ref, win_hbm, cw_ref, cb_ref, cn_ref,
                *rest, tiles_per_seq, q_scale, n_cast):
    cast_src = rest[:n_cast]
    x1_ref, q_ref, k_ref, v_ref, yc_ref = rest[n_cast:n_cast + 5]
    cast_dst = rest[n_cast + 5:2 * n_cast + 5]
    xc_buf, wg_ref, wu_ref, wd_ref, win_ref, stage, sem = rest[2 * n_cast + 5:]
    tm = x_ref.shape[0]
    i = pl.program_id(0)

    @pl.when(i == 0)
    def _():
        _load_as_bf16([(wg_hbm, wg_ref), (wu_hbm, wu_ref), (wd_hbm, wd_ref), (win_hbm, win_ref)], stage, sem)

    for src, dst in zip(cast_src, cast_dst):
        dst[...] = src[...].astype(BF16)
    first = i % tiles_per_seq == 0

    @pl.when(first)
    def _():
        xc_buf[0:CONV_HALO, :] = jnp.zeros((CONV_HALO, xc_buf.shape[1]), F32)

    @pl.when(jnp.logical_not(first))
    def _():
        xc_buf[0:CONV_HALO, :] = xc_buf[tm:tm + CONV_HALO, :]

    d = D_SB
    subs = [slice(r, r + SUB_TILE) for r in range(0, tm, SUB_TILE)]
    xs = [x_ref[r, :] for r in subs]
    hs = [_rms(x, n1_ref[...]).astype(BF16) for x in xs]
    acts = []
    for h in hs:
        gate = jnp.dot(h, wg_ref[...], preferred_element_type=F32)
        up = jnp.dot(h, wu_ref[...], preferred_element_type=F32)
        acts.append((gate * jax.nn.sigmoid(gate) * up).astype(BF16))
    hs = []
    for r, x, act in zip(subs, xs, acts):
        x1 = x + 0.5 * jnp.dot(act, wd_ref[...], preferred_element_type=F32)
        x1_ref[r, :] = x1
        hs.append(_rms(x1, nm_ref[...]).astype(BF16))
    gates = []
    for r, h in zip(subs, hs):
        proj = jnp.dot(h, win_ref[...], preferred_element_type=F32)
        q_ref[r, :] = (proj[:, 0:d] * q_scale).astype(BF16)
        k_ref[r, :] = proj[:, d:2 * d].astype(BF16)
        v_ref[r, :] = proj[:, 2 * d:3 * d].astype(BF16)
        gates.append(proj[:, 3 * d:4 * d])
        xc_buf[CONV_HALO + r.start:CONV_HALO + r.stop, :] = proj[:, 4 * d:5 * d] * proj[:, 5 * d:6 * d]
    for r, gate_b in zip(subs, gates):
        y = cb_ref[...]
        for j in range(CONV_WIDTH):
            off = CONV_HALO - (CONV_WIDTH - 1) + j + r.start
            y = y + xc_buf[off:off + SUB_TILE, :] * cw_ref[j:j + 1, :]
        yc_ref[r, :] = _rms(gate_b * y, cn_ref[...]).astype(BF16)


def _mix_post_kernel(q_ref, kc_ref, kp_ref, vc_ref, vp_ref, k_hbm, v_hbm, tri_ref, x1_ref, yc_ref,
                     sbn_ref, wo_ref, n2_ref, wg_ref, wu_ref, wd_ref, fn_ref, out_ref,
                     qs_ref, carry_ref, acc_ref, max_ref, ysb_ref, kbuf, vbuf, sem,
                     *, tiles_per_seq, n_tiles, final):
    tm = q_ref.shape[0]
    tq = ATT_BLOCK
    n_q = tm // tq
    rows = N_PAIRS * 2 * tq
    s = pl.program_id(0)
    slot = s % 2
    tile = jnp.minimum(s, n_tiles - 1)
    tile_in_seq = tile % tiles_per_seq
    first_block = tile_in_seq * n_q
    seq_row0 = (tile - tile_in_seq) * tm

    @pl.when(s == 0)
    def _():
        ysb_ref[1] = jnp.zeros(ysb_ref.shape[1:], F32)

    lane = lax.broadcasted_iota(jnp.int32, (tq, LANES), 1)
    low = lane < SB_HEAD_DIM
    row = lax.broadcasted_iota(jnp.int32, (rows, tq), 0)
    col = lax.broadcasted_iota(jnp.int32, (rows, tq), 1)
    strict = col < (row & (tq - 1))
    tri_col = lax.broadcasted_iota(jnp.int32, tri_ref.shape, 1)
    zero = jnp.zeros((), BF16)
    prev_shift = jnp.where(tile_in_seq == 0, 2 * tq, 0)

    def block_weights(j, k_blk, v_blk, shift, carry, diagonal):
        if shift is None:
            keep_k = lo_half = hi_half = tri = None
        else:
            k_lane = lane + shift
            keep_k, lo_half = k_lane < LANES, k_lane < SB_HEAD_DIM
            hi_half = (k_lane >= SB_HEAD_DIM) & keep_k
            tri = jnp.where(tri_col + shift < 2 * tq, tri_ref[...], zero)
        z = jnp.concatenate(
            [lax.dot_general(qs_ref[j, p], k_blk(p) if shift is None else jnp.where(keep_k, k_blk(p), zero),
                             (((1,), (1,)), ((), ())), preferred_element_type=F32)
             for p in range(N_PAIRS)], axis=0)
        sp = jnp.maximum(z, 0.0) + jnp.log(1.0 + jnp.exp(-jnp.abs(z)))
        if diagonal:
            sp = jnp.where(strict, sp, 0.0)
        hi = sp.astype(BF16)
        lo = (sp - hi.astype(F32)).astype(BF16)
        sums = jnp.dot(jnp.concatenate([hi, lo], axis=1), tri_ref[...] if shift is None else tri,
                       preferred_element_type=F32)
        logit = z + sums[:, 0:tq]
        if carry is not None:
            logit = logit + carry
        a = jnp.exp(logit)
        if diagonal:
            a = jnp.where(strict, a, 0.0)
        carry = sums[:, tq:2 * tq] if carry is None else carry + sums[:, tq:2 * tq]
        halves = (low, ~low) if shift is None else (lo_half, hi_half)
        v2 = [jnp.concatenate([jnp.where(m, v_blk(p), zero) for m in halves], axis=0)
              for p in range(N_PAIRS)]
        return a.astype(BF16), carry, v2

    def apply_values(j, weights, values, carry, first):
        carry_ref[j] = carry
        for p in range(N_PAIRS):
            r0 = p * 2 * tq
            a2 = jnp.concatenate([w[r:r + tq] for w in weights for r in (r0, r0 + tq)], axis=1)
            v2 = jnp.concatenate([v[p] for v in values], axis=0)
            out = jnp.dot(a2, v2, preferred_element_type=F32)
            acc_ref[j, p] = out if first else acc_ref[j, p] + out
        max_ref[j] = jnp.max(carry)

    for j in range(n_q):
        for p in range(N_PAIRS):
            q2 = q_ref[j * tq:(j + 1) * tq, p * LANES:(p + 1) * LANES]
            qs_ref[j, p] = jnp.concatenate([jnp.where(low, q2, zero), jnp.where(low, zero, q2)], axis=0)

    state = {}

    def attention_item(j, b):
        weights, values, carry = state.get(j, ([], [], None))
        if j >= b:
            k_src, v_src, r0, shift = kc_ref, vc_ref, (j - b) * tq, None
        else:
            k_src, v_src, r0, shift = kp_ref, vp_ref, (n_q + j - b) * tq, prev_shift
        a, carry, v2 = block_weights(
            j, lambda p: k_src[r0:r0 + tq, p * LANES:(p + 1) * LANES],
            lambda p: v_src[r0:r0 + tq, p * LANES:(p + 1) * LANES], shift, carry, diagonal=(b == 0))
        state[j] = (weights + [a], values + [v2], carry)
        if b == FIRST_PASS_BLOCKS - 1:
            apply_values(j, *state[j], first=True)

    items = [(j, b) for j in range(n_q) for b in range(FIRST_PASS_BLOCKS)]
    d_ff = wg_ref.shape[1]
    chunks = [slice(c, c + FF_CHUNK) for c in range(0, d_ff, FF_CHUNK)]
    subs = [slice(r, r + SUB_TILE) for r in range(0, tm, SUB_TILE)]
    slots = [(t, c) for t in range(len(subs)) for c in range(len(chunks))]
    after_slot = {}
    for n, item in enumerate(items):
        after_slot.setdefault(slots[n * len(slots) // len(items)], []).append(item)

    x2s, hs = [], []
    for r in subs:
        ysn = _rms(ysb_ref[1 - slot, r, :], sbn_ref[...]).astype(BF16)
        y = jnp.concatenate([ysn, yc_ref[r, :]], axis=1)
        x2s.append(x1_ref[r, :] + jnp.dot(y, wo_ref[...], preferred_element_type=F32))
        hs.append(_rms(x2s[-1], n2_ref[...]).astype(BF16))
    acts = [[] for _ in subs]
    for t, h in enumerate(hs):
        for ci, c in enumerate(chunks):
            gate = jnp.dot(h, wg_ref[:, c], preferred_element_type=F32)
            up = jnp.dot(h, wu_ref[:, c], preferred_element_type=F32)
            acts[t].append((gate * jax.nn.sigmoid(gate) * up).astype(BF16))
            for item in after_slot.get((t, ci), []):
                attention_item(*item)
    for r, x2, act in zip(subs, x2s, acts):
        x3 = x2 + 0.5 * jnp.dot(jnp.concatenate(act, axis=1), wd_ref[...], preferred_element_type=F32)
        out_ref[r, :] = _rms(x3, fn_ref[...]) if final else x3

    for j in range(n_q):
        def cond(state):
            kb, m = state
            return jnp.logical_and(kb >= 0, m > LOG_WEIGHT_UNDERFLOW)

        def body(state, j=j):
            kb, _ = state
            r0 = pl.multiple_of(seq_row0 + kb * tq, tq)
            copies = [pltpu.make_async_copy(src.at[pl.ds(r0, tq), :], dst, sem.at[n])
                      for n, (src, dst) in enumerate(((k_hbm, kbuf), (v_hbm, vbuf)))]
            for cp in copies:
                cp.start()
            for cp in copies:
                cp.wait()
            a, carry, v2 = block_weights(
                j, lambda p: kbuf[:, p * LANES:(p + 1) * LANES], lambda p: vbuf[:, p * LANES:(p + 1) * LANES],
                None, carry_ref[j], diagonal=False)
            apply_values(j, [a], [v2], carry, first=False)
            return kb - 1, max_ref[j]

        lax.while_loop(cond, body, (first_block + j - FIRST_PASS_BLOCKS, max_ref[j]))
        for p in range(N_PAIRS):
            ysb_ref[slot, j * tq:(j + 1) * tq, p * LANES:(p + 1) * LANES] = acc_ref[j, p]


def _resident(shape):
    return pl.BlockSpec(shape, lambda *_: (0,) * len(shape), pipeline_mode=pl.Buffered(1))


def _rows(tm, width):
    return pl.BlockSpec((tm, width), lambda i: (i, 0))


def _slab_spec(shape, steps):
    rows, cols = shape
    slab = next(r for r in range(BF16_SUBLANES, rows + 1, BF16_SUBLANES) if rows % r == 0 and r * steps >= rows)
    last = rows // slab - 1
    return pl.BlockSpec((slab, cols), lambda i: (jnp.minimum(i, last), 0))


def _pre_call(x, n1, wg, wu, wd, nm, win, cw, cb, cn, cast, *, seq):
    n, d = x.shape
    tm = PRE_ROW_TILE
    steps = n // tm
    d_ff = wg.shape[1]
    d_conv = cw.shape[1]
    kern = functools.partial(_pre_kernel, tiles_per_seq=seq // tm, q_scale=1.0 / math.sqrt(SB_HEAD_DIM),
                             n_cast=len(cast))
    cast_specs = [_slab_spec(w.shape, steps) for w in cast]
    hbm = pl.BlockSpec(memory_space=pl.ANY)
    own = (wg, wu, wd, win)
    assert all(w.shape[0] % STAGE_ROWS == 0 for w in own)
    stage_cols = max(w.shape[1] for w in own)
    return pl.pallas_call(
        kern,
        grid=(steps,),
        in_specs=[_rows(tm, d), _resident((1, d)), hbm, hbm, hbm, _resident((1, d)), hbm,
                  _resident(cw.shape), _resident((1, d_conv)), _resident((1, d_conv))] + cast_specs,
        out_specs=[_rows(tm, d), _rows(tm, D_SB), _rows(tm, D_SB), _rows(tm, D_SB), _rows(tm, d_conv)]
                  + cast_specs,
        out_shape=[jax.ShapeDtypeStruct((n, d), F32)] + [jax.ShapeDtypeStruct((n, D_SB), BF16)] * 3
                  + [jax.ShapeDtypeStruct((n, d_conv), BF16)]
                  + [jax.ShapeDtypeStruct(w.shape, BF16) for w in cast],
        scratch_shapes=[pltpu.VMEM((tm + 2 * CONV_HALO, d_conv), F32)]
                       + [pltpu.VMEM(w.shape, BF16) for w in own]
                       + [pltpu.VMEM((STAGE_SLOTS, STAGE_ROWS, stage_cols), F32),
                          pltpu.SemaphoreType.DMA((STAGE_SLOTS,))],
        compiler_params=pltpu.CompilerParams(dimension_semantics=("arbitrary",),
                                             vmem_limit_bytes=VMEM_LIMIT),
        name="ffn1_proj_conv",
    )(x, n1, wg, wu, wd, nm, win, cw, cb, cn, *cast)


def _suffix_sum_matrix(tk):
    j = jnp.arange(2 * tk)[:, None] % tk
    s = jnp.arange(2 * tk)[None, :]
    return -jnp.where(s < tk, j >= s, True).astype(BF16)


def _mix_post_call(q, k, v, x1, yc, sbn, wo, n2, wg, wu, wd, fn, *, seq, final):
    n, d = x1.shape
    tm = POST_ROW_TILE
    tq = ATT_BLOCK
    n_tiles = n // tm
    tiles_per_seq = seq // tm
    d_ff = wg.shape[1]
    cur = lambda s: (jnp.minimum(s, n_tiles - 1), 0)
    before_cur = lambda s: (jnp.maximum(jnp.minimum(s, n_tiles - 1) - 1, 0), 0)
    prev = lambda s: (jnp.maximum(s - 1, 0), 0)
    kern = functools.partial(_mix_post_kernel, tiles_per_seq=tiles_per_seq, n_tiles=n_tiles, final=final)
    any_space = pl.BlockSpec(memory_space=pl.ANY)
    return pl.pallas_call(
        kern,
        grid=(n_tiles + 1,),
        in_specs=[pl.BlockSpec((tm, D_SB), cur),
                  pl.BlockSpec((tm, D_SB), cur), pl.BlockSpec((tm, D_SB), before_cur),
                  pl.BlockSpec((tm, D_SB), cur), pl.BlockSpec((tm, D_SB), before_cur),
                  any_space, any_space,
                  _resident((2 * tq, 2 * tq)),
                  pl.BlockSpec((tm, d), prev), pl.BlockSpec((tm, yc.shape[1]), prev),
                  _resident((1, D_SB)), _resident(wo.shape), _resident((1, d)),
                  _resident((d, d_ff)), _resident((d, d_ff)), _resident((d_ff, d)), _resident((1, d))],
        out_specs=pl.BlockSpec((tm, d), prev),
        out_shape=jax.ShapeDtypeStruct((n, d), F32),
        scratch_shapes=[pltpu.VMEM((tm // tq, N_PAIRS, 2 * tq, LANES), BF16),
                        pltpu.VMEM((tm // tq, N_PAIRS * 2 * tq, tq), F32),
                        pltpu.VMEM((tm // tq, N_PAIRS, tq, LANES), F32),
                        pltpu.SMEM((tm // tq,), F32),
                        pltpu.VMEM((2, tm, D_SB), F32),
                        pltpu.VMEM((tq, D_SB), BF16), pltpu.VMEM((tq, D_SB), BF16),
                        pltpu.SemaphoreType.DMA((2,))],
        compiler_params=pltpu.CompilerParams(dimension_semantics=("arbitrary",),
                                             vmem_limit_bytes=VMEM_LIMIT),
        name="attention_out_proj_ffn2",
    )(q, k, k, v, v, k, v, _suffix_sum_matrix(tq), x1, yc, sbn, wo, n2, wg, wu, wd, fn)


def kernel(x, ffn1_norm, ffn1_w_gate, ffn1_w_up, ffn1_w_down, mix_norm, w_in, conv_w, conv_b,
           sb_out_norm, conv_out_norm, w_out, ffn2_norm, ffn2_w_gate, ffn2_w_up, ffn2_w_down,
           final_norm):
    batch, seq, d = x.shape
    depth = ffn1_norm.shape[0]
    assert seq % POST_ROW_TILE == 0 and seq % PRE_ROW_TILE == 0 and w_in.shape[2] == 6 * D_SB
    row = lambda t: t.reshape(1, -1)
    xs = x.reshape(batch * seq, d)
    for l in range(depth):
        x1, q, k, v, yc, wo, wg2, wu2, wd2 = _pre_call(
            xs, row(ffn1_norm[l]), ffn1_w_gate[l], ffn1_w_up[l], ffn1_w_down[l],
            row(mix_norm[l]), w_in[l], conv_w[l],
            row(conv_b[l]), row(conv_out_norm[l]),
            (w_out[l], ffn2_w_gate[l], ffn2_w_up[l], ffn2_w_down[l]), seq=seq)
        xs = _mix_post_call(
            q, k, v, x1, yc, row(sb_out_norm[l]), wo, row(ffn2_norm[l]), wg2, wu2, wd2,
            row(final_norm), seq=seq, final=(l == depth - 1))
    return xs.reshape(batch, seq, d)
```

```python
import functools
import math

import jax
import jax.numpy as jnp
from jax import lax
from jax.experimental import pallas as pl
from jax.experimental.pallas import tpu as pltpu

F32 = jnp.float32
BF16 = jnp.bfloat16

EPS = 1e-6
SB_HEADS = 8
SB_HEAD_DIM = 64
D_SB = SB_HEADS * SB_HEAD_DIM
CONV_WIDTH = 3
LANES = 128
BF16_SUBLANES = 16
N_PAIRS = D_SB // LANES
PRE_ROW_TILE = 512
POST_ROW_TILE = 512
SUB_TILE = 256
FF_CHUNK = 256
ATT_BLOCK = 128
FIRST_PASS_BLOCKS = 3
STAGE_ROWS = 128
STAGE_SLOTS = 4
CONV_HALO = 8
LOG_WEIGHT_UNDERFLOW = -104.0
VMEM_LIMIT = 56 * 1024 * 1024


def _rms(x, g):
    return x * lax.rsqrt(jnp.mean(x * x, axis=-1, keepdims=True) + EPS) * g


def _load_as_bf16(pairs, stage, sem):
    slots, slab = stage.shape[0], stage.shape[1]
    jobs = [(src, dst, r) for src, dst in pairs for r in range(0, src.shape[0], slab)]

    def copy(n):
        src, _, r = jobs[n]
        k = n % slots
        return pltpu.make_async_copy(src.at[r:r + slab, :], stage.at[k, :, 0:src.shape[1]], sem.at[k])

    for n in range(min(slots - 1, len(jobs))):
        copy(n).start()
    for n, (src, dst, r) in enumerate(jobs):
        if n + slots - 1 < len(jobs):
            copy(n + slots - 1).start()
        copy(n).wait()
        dst[r:r + slab, :] = stage[n % slots, :, 0:src.shape[1]].astype(BF16)


def _pre_kernel(x_ref, n1_ref, wg_hbm, wu_hbm, wd_hbm, nm_ref, win_hbm, cw_ref, cb_ref, cn_ref,
                *rest, tiles_per_seq, q_scale, n_cast):
    cast_src = rest[:n_cast]
    x1_ref, q_ref, k_ref, v_ref, yc_ref = rest[n_cast:n_cast + 5]
    cast_dst = rest[n_cast + 5:2 * n_cast + 5]
    xc_buf, wg_ref, wu_ref, wd_ref, win_ref, stage, sem = rest[2 * n_cast + 5:]
    tm = x_ref.shape[0]
    i = pl.program_id(0)

    @pl.when(i == 0)
    def _():
        _load_as_bf16([(wg_hbm, wg_ref), (wu_hbm, wu_ref), (wd_hbm, wd_ref), (win_hbm, win_ref)], stage, sem)

    for src, dst in zip(cast_src, cast_dst):
        dst[...] = src[...].astype(BF16)
    first = i % tiles_per_seq == 0

    @pl.when(first)
    def _():
        xc_buf[0:CONV_HALO, :] = jnp.zeros((CONV_HALO, xc_buf.shape[1]), F32)

    @pl.when(jnp.logical_not(first))
    def _():
        xc_buf[0:CONV_HALO, :] = xc_buf[tm:tm + CONV_HALO, :]

    d = D_SB
    subs = [slice(r, r + SUB_TILE) for r in range(0, tm, SUB_TILE)]
    xs = [x_ref[r, :] for r in subs]
    hs = [_rms(x, n1_ref[...]).astype(BF16) for x in xs]
    acts = []
    for h in hs:
        gate = jnp.dot(h, wg_ref[...], preferred_element_type=F32)
        up = jnp.dot(h, wu_ref[...], preferred_element_type=F32)
        acts.append((gate * jax.nn.sigmoid(gate) * up).astype(BF16))
    hs = []
    for r, x, act in zip(subs, xs, acts):
        x1 = x + 0.5 * jnp.dot(act, wd_ref[...], preferred_element_type=F32)
        x1_ref[r, :] = x1
        hs.append(_rms(x1, nm_ref[...]).astype(BF16))
    gates = []
    for r, h in zip(subs, hs):
        proj = jnp.dot(h, win_ref[...], preferred_element_type=F32)
        q_ref[r, :] = (proj[:, 0:d] * q_scale).astype(BF16)
        k_ref[r, :] = proj[:, d:2 * d].astype(BF16)
        v_ref[r, :] = proj[:, 2 * d:3 * d].astype(BF16)
        gates.append(proj[:, 3 * d:4 * d])
        xc_buf[CONV_HALO + r.start:CONV_HALO + r.stop, :] = proj[:, 4 * d:5 * d] * proj[:, 5 * d:6 * d]
    for r, gate_b in zip(subs, gates):
        y = cb_ref[...]
        for j in range(CONV_WIDTH):
            off = CONV_HALO - (CONV_WIDTH - 1) + j + r.start
            y = y + xc_buf[off:off + SUB_TILE, :] * cw_ref[j:j + 1, :]
        yc_ref[r, :] = _rms(gate_b * y, cn_ref[...]).astype(BF16)


def _mix_post_kernel(q_ref, kc_ref, kp_ref, vc_ref, vp_ref, k_hbm, v_hbm, tri_ref, x1_ref, yc_ref,
                     sbn_ref, wo_ref, n2_ref, wg_ref, wu_ref, wd_ref, fn_ref, out_ref,
                     qs_ref, carry_ref, acc_ref, max_ref, ysb_ref, kbuf, vbuf, sem,
                     *, tiles_per_seq, n_tiles, final):
    tm = q_ref.shape[0]
    tq = ATT_BLOCK
    n_q = tm // tq
    rows = N_PAIRS * 2 * tq
    s = pl.program_id(0)
    slot = s % 2
    tile = jnp.minimum(s, n_tiles - 1)
    tile_in_seq = tile % tiles_per_seq
    first_block = tile_in_seq * n_q
    seq_row0 = (tile - tile_in_seq) * tm

    @pl.when(s == 0)
    def _():
        ysb_ref[1] = jnp.zeros(ysb_ref.shape[1:], F32)

    lane = lax.broadcasted_iota(jnp.int32, (tq, LANES), 1)
    low = lane < SB_HEAD_DIM
    row = lax.broadcasted_iota(jnp.int32, (rows, tq), 0)
    col = lax.broadcasted_iota(jnp.int32, (rows, tq), 1)
    strict = col < (row & (tq - 1))
    tri_col = lax.broadcasted_iota(jnp.int32, tri_ref.shape, 1)
    zero = jnp.zeros((), BF16)
    prev_shift = jnp.where(tile_in_seq == 0, 2 * tq, 0)

    early = tq // 2
    head_rows = [p * 2 * tq + h * tq for p in range(N_PAIRS) for h in range(2)]

    def early_of(x):
        return jnp.concatenate([x[r:r + early] for r in head_rows], axis=0)

    def with_early(full, part, pad=False):
        pieces = []
        for n, r in enumerate(head_rows):
            rest = jnp.zeros((tq - early, part.shape[1]), part.dtype) if pad else full[r + early:r + tq]
            pieces += [part[n * early:(n + 1) * early], rest]
        return jnp.concatenate(pieces, axis=0)

    def block_weights(j, k_blk, v_blk, shift, carry, diagonal, only_early=False, skip_early=None):
        if shift is None:
            keep_k = lo_half = hi_half = tri = None
        else:
            k_lane = lane + shift
            keep_k, lo_half = k_lane < LANES, k_lane < SB_HEAD_DIM
            hi_half = (k_lane >= SB_HEAD_DIM) & keep_k
            tri = jnp.where(tri_col + shift < 2 * tq, tri_ref[...], zero)
        if only_early:
            q_stack = lambda p: jnp.concatenate([qs_ref[j, p, 0:early], qs_ref[j, p, tq:tq + early]], axis=0)
        else:
            q_stack = lambda p: qs_ref[j, p]
        z = jnp.concatenate(
            [lax.dot_general(q_stack(p), k_blk(p) if shift is None else jnp.where(keep_k, k_blk(p), zero),
                             (((1,), (1,)), ((), ())), preferred_element_type=F32)
             for p in range(N_PAIRS)], axis=0)
        sp = jnp.maximum(z, 0.0) + jnp.log(1.0 + jnp.exp(-jnp.abs(z)))
        if diagonal:
            sp = jnp.where(strict, sp, 0.0)
        if skip_early is not None:
            taking_part = (row & (tq - 1)) + jnp.where(skip_early, 0, tq) >= early
            sp = jnp.where(taking_part, sp, 0.0)
        hi = sp.astype(BF16)
        lo = (sp - hi.astype(F32)).astype(BF16)
        sums = jnp.dot(jnp.concatenate([hi, lo], axis=1), tri_ref[...] if shift is None else tri,
                       preferred_element_type=F32)
        logit = z + sums[:, 0:tq]
        if carry is not None:
            logit = logit + carry
        a = jnp.exp(logit)
        if diagonal:
            a = jnp.where(strict, a, 0.0)
        if skip_early is not None:
            a = jnp.where(taking_part, a, 0.0)
        carry = sums[:, tq:2 * tq] if carry is None else carry + sums[:, tq:2 * tq]
        halves = (low, ~low) if shift is None else (lo_half, hi_half)
        v2 = [jnp.concatenate([jnp.where(m, v_blk(p), zero) for m in halves], axis=0)
              for p in range(N_PAIRS)]
        return a.astype(BF16), carry, v2

    def apply_values(j, weights, values, carry, first):
        carry_ref[j] = carry
        for p in range(N_PAIRS):
            r0 = p * 2 * tq
            a2 = jnp.concatenate([w[r:r + tq] for w in weights for r in (r0, r0 + tq)], axis=1)
            v2 = jnp.concatenate([v[p] for v in values], axis=0)
            out = jnp.dot(a2, v2, preferred_element_type=F32)
            acc_ref[j, p] = out if first else acc_ref[j, p] + out
        max_ref[j] = jnp.max(carry)

    for j in range(n_q):
        for p in range(N_PAIRS):
            q2 = q_ref[j * tq:(j + 1) * tq, p * LANES:(p + 1) * LANES]
            qs_ref[j, p] = jnp.concatenate([jnp.where(low, q2, zero), jnp.where(low, zero, q2)], axis=0)

    state = {}

    def attention_item(j, b):
        weights, values, carry = state.get(j, ([], [], None))
        if j >= b:
            k_src, v_src, r0, shift = kc_ref, vc_ref, (j - b) * tq, None
        else:
            k_src, v_src, r0, shift = kp_ref, vp_ref, (n_q + j - b) * tq, prev_shift
        k_blk = lambda p: k_src[r0:r0 + tq, p * LANES:(p + 1) * LANES]
        v_blk = lambda p: v_src[r0:r0 + tq, p * LANES:(p + 1) * LANES]
        if b < FIRST_PASS_BLOCKS - 1:
            a, carry, v2 = block_weights(j, k_blk, v_blk, shift, carry, diagonal=(b == 0))
        else:
            a, part, v2 = block_weights(j, k_blk, v_blk, shift, early_of(carry), False, only_early=True)
            a, carry = with_early(None, a, pad=True), with_early(carry, part)
        state[j] = (weights + [a], values + [v2], carry)
        if b == FIRST_PASS_BLOCKS - 1:
            apply_values(j, *state[j], first=True)

    items = [(j, b) for j in range(n_q) for b in range(FIRST_PASS_BLOCKS)]
    d_ff = wg_ref.shape[1]
    chunks = [slice(c, c + FF_CHUNK) for c in range(0, d_ff, FF_CHUNK)]
    subs = [slice(r, r + SUB_TILE) for r in range(0, tm, SUB_TILE)]
    slots = [(t, c) for t in range(len(subs)) for c in range(len(chunks))]
    after_slot = {}
    for n, item in enumerate(items):
        after_slot.setdefault(slots[n * len(slots) // len(items)], []).append(item)

    x2s, hs = [], []
    for r in subs:
        ysn = _rms(ysb_ref[1 - slot, r, :], sbn_ref[...]).astype(BF16)
        y = jnp.concatenate([ysn, yc_ref[r, :]], axis=1)
        x2s.append(x1_ref[r, :] + jnp.dot(y, wo_ref[...], preferred_element_type=F32))
        hs.append(_rms(x2s[-1], n2_ref[...]).astype(BF16))
    acts = [[] for _ in subs]
    for t, h in enumerate(hs):
        for ci, c in enumerate(chunks):
            gate = jnp.dot(h, wg_ref[:, c], preferred_element_type=F32)
            up = jnp.dot(h, wu_ref[:, c], preferred_element_type=F32)
            acts[t].append((gate * jax.nn.sigmoid(gate) * up).astype(BF16))
            for item in after_slot.get((t, ci), []):
                attention_item(*item)
    for r, x2, act in zip(subs, x2s, acts):
        x3 = x2 + 0.5 * jnp.dot(jnp.concatenate(act, axis=1), wd_ref[...], preferred_element_type=F32)
        out_ref[r, :] = _rms(x3, fn_ref[...]) if final else x3

    for j in range(n_q):
        def cond(state):
            kb, m = state
            return jnp.logical_and(kb >= 0, m > LOG_WEIGHT_UNDERFLOW)

        def body(state, j=j):
            kb, _ = state
            r0 = pl.multiple_of(seq_row0 + kb * tq, tq)
            copies = [pltpu.make_async_copy(src.at[pl.ds(r0, tq), :], dst, sem.at[n])
                      for n, (src, dst) in enumerate(((k_hbm, kbuf), (v_hbm, vbuf)))]
            for cp in copies:
                cp.start()
            for cp in copies:
                cp.wait()
            a, carry, v2 = block_weights(
                j, lambda p: kbuf[:, p * LANES:(p + 1) * LANES], lambda p: vbuf[:, p * LANES:(p + 1) * LANES],
                None, carry_ref[j], diagonal=False, skip_early=(kb == resume))
            apply_values(j, [a], [v2], carry, first=False)
            return kb - 1, max_ref[j]

        resume = first_block + j - (FIRST_PASS_BLOCKS - 1)
        lax.while_loop(cond, body, (resume, max_ref[j]))
        for p in range(N_PAIRS):
            ysb_ref[slot, j * tq:(j + 1) * tq, p * LANES:(p + 1) * LANES] = acc_ref[j, p]


def _resident(shape):
    return pl.BlockSpec(shape, lambda *_: (0,) * len(shape), pipeline_mode=pl.Buffered(1))


def _rows(tm, width):
    return pl.BlockSpec((tm, width), lambda i: (i, 0))


def _slab_spec(shape, steps):
    rows, cols = shape
    slab = next(r for r in range(BF16_SUBLANES, rows + 1, BF16_SUBLANES) if rows % r == 0 and r * steps >= rows)
    last = rows // slab - 1
    return pl.BlockSpec((slab, cols), lambda i: (jnp.minimum(i, last), 0))


def _pre_call(x, n1, wg, wu, wd, nm, win, cw, cb, cn, cast, *, seq):
    n, d = x.shape
    tm = PRE_ROW_TILE
    steps = n // tm
    d_ff = wg.shape[1]
    d_conv = cw.shape[1]
    kern = functools.partial(_pre_kernel, tiles_per_seq=seq // tm, q_scale=1.0 / math.sqrt(SB_HEAD_DIM),
                             n_cast=len(cast))
    cast_specs = [_slab_spec(w.shape, steps) for w in cast]
    hbm = pl.BlockSpec(memory_space=pl.ANY)
    own = (wg, wu, wd, win)
    assert all(w.shape[0] % STAGE_ROWS == 0 for w in own)
    stage_cols = max(w.shape[1] for w in own)
    return pl.pallas_call(
        kern,
        grid=(steps,),
        in_specs=[_rows(tm, d), _resident((1, d)), hbm, hbm, hbm, _resident((1, d)), hbm,
                  _resident(cw.shape), _resident((1, d_conv)), _resident((1, d_conv))] + cast_specs,
        out_specs=[_rows(tm, d), _rows(tm, D_SB), _rows(tm, D_SB), _rows(tm, D_SB), _rows(tm, d_conv)]
                  + cast_specs,
        out_shape=[jax.ShapeDtypeStruct((n, d), F32)] + [jax.ShapeDtypeStruct((n, D_SB), BF16)] * 3
                  + [jax.ShapeDtypeStruct((n, d_conv), BF16)]
                  + [jax.ShapeDtypeStruct(w.shape, BF16) for w in cast],
        scratch_shapes=[pltpu.VMEM((tm + 2 * CONV_HALO, d_conv), F32)]
                       + [pltpu.VMEM(w.shape, BF16) for w in own]
                       + [pltpu.VMEM((STAGE_SLOTS, STAGE_ROWS, stage_cols), F32),
                          pltpu.SemaphoreType.DMA((STAGE_SLOTS,))],
        compiler_params=pltpu.CompilerParams(dimension_semantics=("arbitrary",),
                                             vmem_limit_bytes=VMEM_LIMIT),
        name="ffn1_proj_conv",
    )(x, n1, wg, wu, wd, nm, win, cw, cb, cn, *cast)


def _suffix_sum_matrix(tk):
    j = jnp.arange(2 * tk)[:, None] % tk
    s = jnp.arange(2 * tk)[None, :]
    return -jnp.where(s < tk, j >= s, True).astype(BF16)


def _mix_post_call(q, k, v, x1, yc, sbn, wo, n2, wg, wu, wd, fn, *, seq, final):
    n, d = x1.shape
    tm = POST_ROW_TILE
    tq = ATT_BLOCK
    n_tiles = n // tm
    tiles_per_seq = seq // tm
    d_ff = wg.shape[1]
    cur = lambda s: (jnp.minimum(s, n_tiles - 1), 0)
    before_cur = lambda s: (jnp.maximum(jnp.minimum(s, n_tiles - 1) - 1, 0), 0)
    prev = lambda s: (jnp.maximum(s - 1, 0), 0)
    kern = functools.partial(_mix_post_kernel, tiles_per_seq=tiles_per_seq, n_tiles=n_tiles, final=final)
    any_space = pl.BlockSpec(memory_space=pl.ANY)
    return pl.pallas_call(
        kern,
        grid=(n_tiles + 1,),
        in_specs=[pl.BlockSpec((tm, D_SB), cur),
                  pl.BlockSpec((tm, D_SB), cur), pl.BlockSpec((tm, D_SB), before_cur),
                  pl.BlockSpec((tm, D_SB), cur), pl.BlockSpec((tm, D_SB), before_cur),
                  any_space, any_space,
                  _resident((2 * tq, 2 * tq)),
                  pl.BlockSpec((tm, d), prev), pl.BlockSpec((tm, yc.shape[1]), prev),
                  _resident((1, D_SB)), _resident(wo.shape), _resident((1, d)),
                  _resident((d, d_ff)), _resident((d, d_ff)), _resident((d_ff, d)), _resident((1, d))],
        out_specs=pl.BlockSpec((tm, d), prev),
        out_shape=jax.ShapeDtypeStruct((n, d), F32),
        scratch_shapes=[pltpu.VMEM((tm // tq, N_PAIRS, 2 * tq, LANES), BF16),
                        pltpu.VMEM((tm // tq, N_PAIRS * 2 * tq, tq), F32),
                        pltpu.VMEM((tm // tq, N_PAIRS, tq, LANES), F32),
                        pltpu.SMEM((tm // tq,), F32),
                        pltpu.VMEM((2, tm, D_SB), F32),
                        pltpu.VMEM((tq, D_SB), BF16), pltpu.VMEM((tq, D_SB), BF16),
                        pltpu.SemaphoreType.DMA((2,))],
        compiler_params=pltpu.CompilerParams(dimension_semantics=("arbitrary",),
                                             vmem_limit_bytes=VMEM_LIMIT),
        name="attention_out_proj_ffn2",
    )(q, k, k, v, v, k, v, _suffix_sum_matrix(tq), x1, yc, sbn, wo, n2, wg, wu, wd, fn)


def kernel(x, ffn1_norm, ffn1_w_gate, ffn1_w_up, ffn1_w_down, mix_norm, w_in, conv_w, conv_b,
           sb_out_norm, conv_out_norm, w_out, ffn2_norm, ffn2_w_gate, ffn2_w_up, ffn2_w_down,
           final_norm):
    batch, seq, d = x.shape
    depth = ffn1_norm.shape[0]
    assert seq % POST_ROW_TILE == 0 and seq % PRE_ROW_TILE == 0 and w_in.shape[2] == 6 * D_SB
    row = lambda t: t.reshape(1, -1)
    xs = x.reshape(batch * seq, d)
    for l in range(depth):
        x1, q, k, v, yc, wo, wg2, wu2, wd2 = _pre_call(
            xs, row(ffn1_norm[l]), ffn1_w_gate[l], ffn1_w_up[l], ffn1_w_down[l],
            row(mix_norm[l]), w_in[l], conv_w[l],
            row(conv_b[l]), row(conv_out_norm[l]),
            (w_out[l], ffn2_w_gate[l], ffn2_w_up[l], ffn2_w_down[l]), seq=seq)
        xs = _mix_post_call(
            q, k, v, x1, yc, row(sb_out_norm[l]), wo, row(ffn2_norm[l]), wg2, wu2, wd2,
            row(final_norm), seq=seq, final=(l == depth - 1))
    return xs.reshape(batch, seq, d)
```

```python
import functools
import math

import jax
import jax.numpy as jnp
from jax import lax
from jax.experimental import pallas as pl
from jax.experimental.pallas import tpu as pltpu

F32 = jnp.float32
BF16 = jnp.bfloat16

EPS = 1e-6
SB_HEADS = 8
SB_HEAD_DIM = 64
D_SB = SB_HEADS * SB_HEAD_DIM
CONV_WIDTH = 3
LANES = 128
BF16_SUBLANES = 16
N_PAIRS = D_SB // LANES
PRE_ROW_TILE = 512
POST_ROW_TILE = 512
SUB_TILE = 256
FF_CHUNK = 256
ATT_BLOCK = 128
FIRST_PASS_BLOCKS = 3
STAGE_ROWS = 128
STAGE_SLOTS = 4
CONV_HALO = 8
LOG_WEIGHT_UNDERFLOW = -104.0
VMEM_LIMIT = 56 * 1024 * 1024


def _rms(x, g):
    return x * lax.rsqrt(jnp.mean(x * x, axis=-1, keepdims=True) + EPS) * g


def _load_as_bf16(pairs, stage, sem):
    slots, slab = stage.shape[0], stage.shape[1]
    jobs = [(src, dst, r) for src, dst in pairs for r in range(0, src.shape[0], slab)]

    def copy(n):
        src, _, r = jobs[n]
        k = n % slots
        return pltpu.make_async_copy(src.at[r:r + slab, :], stage.at[k, :, 0:src.shape[1]], sem.at[k])

    for n in range(min(slots - 1, len(jobs))):
        copy(n).start()
    for n, (src, dst, r) in enumerate(jobs):
        if n + slots - 1 < len(jobs):
            copy(n + slots - 1).start()
        copy(n).wait()
        dst[r:r + slab, :] = stage[n % slots, :, 0:src.shape[1]].astype(BF16)


def _pre_kernel(x_ref, n1_ref, wg_hbm, wu_hbm, wd_hbm, nm_ref, win_hbm, cw_ref, cb_ref, cn_ref,
                *rest, tiles_per_seq, q_scale, n_cast):
    cast_src = rest[:n_cast]
    x1_ref, q_ref, k_ref, v_ref, yc_ref = rest[n_cast:n_cast + 5]
    cast_dst = rest[n_cast + 5:2 * n_cast + 5]
    xc_buf, wg_ref, wu_ref, wd_ref, win_ref, stage, sem = rest[2 * n_cast + 5:]
    tm = x_ref.shape[0]
    i = pl.program_id(0)

    @pl.when(i == 0)
    def _():
        _load_as_bf16([(wg_hbm, wg_ref), (wu_hbm, wu_ref), (wd_hbm, wd_ref), (win_hbm, win_ref)], stage, sem)

    for src, dst in zip(cast_src, cast_dst):
        dst[...] = src[...].astype(BF16)
    first = i % tiles_per_seq == 0

    @pl.when(first)
    def _():
        xc_buf[0:CONV_HALO, :] = jnp.zeros((CONV_HALO, xc_buf.shape[1]), F32)

    @pl.when(jnp.logical_not(first))
    def _():
        xc_buf[0:CONV_HALO, :] = xc_buf[tm:tm + CONV_HALO, :]

    d = D_SB
    subs = [slice(r, r + SUB_TILE) for r in range(0, tm, SUB_TILE)]
    xs = [x_ref[r, :] for r in subs]
    hs = [_rms(x, n1_ref[...]).astype(BF16) for x in xs]
    acts = []
    for h in hs:
        gate = jnp.dot(h, wg_ref[...], preferred_element_type=F32)
        up = jnp.dot(h, wu_ref[...], preferred_element_type=F32)
        acts.append((gate * jax.nn.sigmoid(gate) * up).astype(BF16))
    hs = []
    for r, x, act in zip(subs, xs, acts):
        x1 = x + 0.5 * jnp.dot(act, wd_ref[...], preferred_element_type=F32)
        x1_ref[r, :] = x1
        hs.append(_rms(x1, nm_ref[...]).astype(BF16))
    gates = []
    for r, h in zip(subs, hs):
        proj = jnp.dot(h, win_ref[...], preferred_element_type=F32)
        q_ref[r, :] = (proj[:, 0:d] * q_scale).astype(BF16)
        k_ref[r, :] = proj[:, d:2 * d].astype(BF16)
        v_ref[r, :] = proj[:, 2 * d:3 * d].astype(BF16)
        gates.append(proj[:, 3 * d:4 * d])
        xc_buf[CONV_HALO + r.start:CONV_HALO + r.stop, :] = proj[:, 4 * d:5 * d] * proj[:, 5 * d:6 * d]
    for r, gate_b in zip(subs, gates):
        y = cb_ref[...]
        for j in range(CONV_WIDTH):
            off = CONV_HALO - (CONV_WIDTH - 1) + j + r.start
            y = y + xc_buf[off:off + SUB_TILE, :] * cw_ref[j:j + 1, :]
        yc_ref[r, :] = _rms(gate_b * y, cn_ref[...]).astype(BF16)


def _mix_post_kernel(q_ref, kc_ref, kp_ref, vc_ref, vp_ref, k_hbm, v_hbm, tri_ref, x1_ref, yc_ref,
                     sbn_ref, wo_ref, n2_ref, wg_ref, wu_ref, wd_ref, fn_ref, out_ref,
                     qs_ref, carry_ref, acc_ref, max_ref, ysb_ref, kbuf, vbuf, sem,
                     *, tiles_per_seq, n_tiles, final):
    tm = q_ref.shape[0]
    tq = ATT_BLOCK
    n_q = tm // tq
    rows = N_PAIRS * 2 * tq
    s = pl.program_id(0)
    slot = s % 2
    tile = jnp.minimum(s, n_tiles - 1)
    tile_in_seq = tile % tiles_per_seq
    first_block = tile_in_seq * n_q
    seq_row0 = (tile - tile_in_seq) * tm

    lane = lax.broadcasted_iota(jnp.int32, (tq, LANES), 1)
    low = lane < SB_HEAD_DIM
    row = lax.broadcasted_iota(jnp.int32, (rows, tq), 0)
    col = lax.broadcasted_iota(jnp.int32, (rows, tq), 1)
    strict = col < (row & (tq - 1))
    tri_col = lax.broadcasted_iota(jnp.int32, tri_ref.shape, 1)
    zero = jnp.zeros((), BF16)
    prev_shift = jnp.where(tile_in_seq == 0, 2 * tq, 0)

    early = tq // 2
    head_rows = [p * 2 * tq + h * tq for p in range(N_PAIRS) for h in range(2)]

    def early_of(x):
        return jnp.concatenate([x[r:r + early] for r in head_rows], axis=0)

    def with_early(full, part, pad=False):
        pieces = []
        for n, r in enumerate(head_rows):
            rest = jnp.zeros((tq - early, part.shape[1]), part.dtype) if pad else full[r + early:r + tq]
            pieces += [part[n * early:(n + 1) * early], rest]
        return jnp.concatenate(pieces, axis=0)

    def block_weights(j, k_blk, v_blk, shift, carry, diagonal, only_early=False, skip_early=None):
        if shift is None:
            keep_k = lo_half = hi_half = tri = None
        else:
            k_lane = lane + shift
            keep_k, lo_half = k_lane < LANES, k_lane < SB_HEAD_DIM
            hi_half = (k_lane >= SB_HEAD_DIM) & keep_k
            tri = jnp.where(tri_col + shift < 2 * tq, tri_ref[...], zero)
        if only_early:
            q_stack = lambda p: jnp.concatenate([qs_ref[j, p, 0:early], qs_ref[j, p, tq:tq + early]], axis=0)
        else:
            q_stack = lambda p: qs_ref[j, p]
        z = jnp.concatenate(
            [lax.dot_general(q_stack(p), k_blk(p) if shift is None else jnp.where(keep_k, k_blk(p), zero),
                             (((1,), (1,)), ((), ())), preferred_element_type=F32)
             for p in range(N_PAIRS)], axis=0)
        sp = jnp.maximum(z, 0.0) + jnp.log(1.0 + jnp.exp(-jnp.abs(z)))
        if diagonal:
            sp = jnp.where(strict, sp, 0.0)
        if skip_early is not None:
            taking_part = (row & (tq - 1)) + jnp.where(skip_early, 0, tq) >= early
            sp = jnp.where(taking_part, sp, 0.0)
        hi = sp.astype(BF16)
        lo = (sp - hi.astype(F32)).astype(BF16)
        sums = jnp.dot(jnp.concatenate([hi, lo], axis=1), tri_ref[...] if shift is None else tri,
                       preferred_element_type=F32)
        logit = z + sums[:, 0:tq]
        if carry is not None:
            logit = logit + carry
        a = jnp.exp(logit)
        if diagonal:
            a = jnp.where(strict, a, 0.0)
        if skip_early is not None:
            a = jnp.where(taking_part, a, 0.0)
        carry = sums[:, tq:2 * tq] if carry is None else carry + sums[:, tq:2 * tq]
        halves = (low, ~low) if shift is None else (lo_half, hi_half)
        v2 = [jnp.concatenate([jnp.where(m, v_blk(p), zero) for m in halves], axis=0)
              for p in range(N_PAIRS)]
        return a.astype(BF16), carry, v2

    def apply_values(j, weights, values, carry, first):
        carry_ref[j] = carry
        for p in range(N_PAIRS):
            r0 = p * 2 * tq
            a2 = jnp.concatenate([w[r:r + tq] for w in weights for r in (r0, r0 + tq)], axis=1)
            v2 = jnp.concatenate([v[p] for v in values], axis=0)
            out = jnp.dot(a2, v2, preferred_element_type=F32)
            acc_ref[j, p] = out if first else acc_ref[j, p] + out
        max_ref[j] = jnp.max(carry)

    def main_block(attention, dense):
        state = {}

        def attention_item(j, b):
            weights, values, carry = state.get(j, ([], [], None))
            if j >= b:
                k_src, v_src, r0, shift = kc_ref, vc_ref, (j - b) * tq, None
            else:
                k_src, v_src, r0, shift = kp_ref, vp_ref, (n_q + j - b) * tq, prev_shift
            k_blk = lambda p: k_src[r0:r0 + tq, p * LANES:(p + 1) * LANES]
            v_blk = lambda p: v_src[r0:r0 + tq, p * LANES:(p + 1) * LANES]
            if b < FIRST_PASS_BLOCKS - 1:
                a, carry, v2 = block_weights(j, k_blk, v_blk, shift, carry, diagonal=(b == 0))
            else:
                a, part, v2 = block_weights(j, k_blk, v_blk, shift, early_of(carry), False, only_early=True)
                a, carry = with_early(None, a, pad=True), with_early(carry, part)
            state[j] = (weights + [a], values + [v2], carry)
            if b == FIRST_PASS_BLOCKS - 1:
                apply_values(j, *state[j], first=True)

        items = [(j, b) for j in range(n_q) for b in range(FIRST_PASS_BLOCKS)] if attention else []
        if attention:
            for j in range(n_q):
                for p in range(N_PAIRS):
                    q2 = q_ref[j * tq:(j + 1) * tq, p * LANES:(p + 1) * LANES]
                    qs_ref[j, p] = jnp.concatenate([jnp.where(low, q2, zero), jnp.where(low, zero, q2)], axis=0)
        if not dense:
            for item in items:
                attention_item(*item)
            return
        d_ff = wg_ref.shape[1]
        chunks = [slice(c, c + FF_CHUNK) for c in range(0, d_ff, FF_CHUNK)]
        subs = [slice(r, r + SUB_TILE) for r in range(0, tm, SUB_TILE)]
        slots = [(t, c) for t in range(len(subs)) for c in range(len(chunks))]
        after_slot = {}
        for n, item in enumerate(items):
            after_slot.setdefault(slots[n * len(slots) // len(items)], []).append(item)

        x2s, hs = [], []
        for r in subs:
            ysn = _rms(ysb_ref[1 - slot, r, :], sbn_ref[...]).astype(BF16)
            y = jnp.concatenate([ysn, yc_ref[r, :]], axis=1)
            x2s.append(x1_ref[r, :] + jnp.dot(y, wo_ref[...], preferred_element_type=F32))
            hs.append(_rms(x2s[-1], n2_ref[...]).astype(BF16))
        acts = [[] for _ in subs]
        for t, h in enumerate(hs):
            for ci, c in enumerate(chunks):
                gate = jnp.dot(h, wg_ref[:, c], preferred_element_type=F32)
                up = jnp.dot(h, wu_ref[:, c], preferred_element_type=F32)
                acts[t].append((gate * jax.nn.sigmoid(gate) * up).astype(BF16))
                for item in after_slot.get((t, ci), []):
                    attention_item(*item)
        for r, x2, act in zip(subs, x2s, acts):
            x3 = x2 + 0.5 * jnp.dot(jnp.concatenate(act, axis=1), wd_ref[...], preferred_element_type=F32)
            out_ref[r, :] = _rms(x3, fn_ref[...]) if final else x3

    pl.when(s == 0)(lambda: main_block(True, False))
    pl.when(jnp.logical_and(s > 0, s < n_tiles))(lambda: main_block(True, True))
    pl.when(s == n_tiles)(lambda: main_block(False, True))

    for j in range(n_q):
        def cond(state):
            kb, m = state
            return jnp.logical_and(jnp.logical_and(kb >= 0, m > LOG_WEIGHT_UNDERFLOW), s < n_tiles)

        def body(state, j=j):
            kb, _ = state
            r0 = pl.multiple_of(seq_row0 + kb * tq, tq)
            copies = [pltpu.make_async_copy(src.at[pl.ds(r0, tq), :], dst, sem.at[n])
                      for n, (src, dst) in enumerate(((k_hbm, kbuf), (v_hbm, vbuf)))]
            for cp in copies:
                cp.start()
            for cp in copies:
                cp.wait()
            a, carry, v2 = block_weights(
                j, lambda p: kbuf[:, p * LANES:(p + 1) * LANES], lambda p: vbuf[:, p * LANES:(p + 1) * LANES],
                None, carry_ref[j], diagonal=False, skip_early=(kb == resume))
            apply_values(j, [a], [v2], carry, first=False)
            return kb - 1, max_ref[j]

        resume = first_block + j - (FIRST_PASS_BLOCKS - 1)
        lax.while_loop(cond, body, (resume, max_ref[j]))
        for p in range(N_PAIRS):
            ysb_ref[slot, j * tq:(j + 1) * tq, p * LANES:(p + 1) * LANES] = acc_ref[j, p]


def _resident(shape):
    return pl.BlockSpec(shape, lambda *_: (0,) * len(shape), pipeline_mode=pl.Buffered(1))


def _rows(tm, width):
    return pl.BlockSpec((tm, width), lambda i: (i, 0))


def _slab_spec(shape, steps):
    rows, cols = shape
    slab = next(r for r in range(BF16_SUBLANES, rows + 1, BF16_SUBLANES) if rows % r == 0 and r * steps >= rows)
    last = rows // slab - 1
    return pl.BlockSpec((slab, cols), lambda i: (jnp.minimum(i, last), 0))


def _pre_call(x, n1, wg, wu, wd, nm, win, cw, cb, cn, cast, *, seq):
    n, d = x.shape
    tm = PRE_ROW_TILE
    steps = n // tm
    d_ff = wg.shape[1]
    d_conv = cw.shape[1]
    kern = functools.partial(_pre_kernel, tiles_per_seq=seq // tm, q_scale=1.0 / math.sqrt(SB_HEAD_DIM),
                             n_cast=len(cast))
    cast_specs = [_slab_spec(w.shape, steps) for w in cast]
    hbm = pl.BlockSpec(memory_space=pl.ANY)
    own = (wg, wu, wd, win)
    assert all(w.shape[0] % STAGE_ROWS == 0 for w in own)
    stage_cols = max(w.shape[1] for w in own)
    return pl.pallas_call(
        kern,
        grid=(steps,),
        in_specs=[_rows(tm, d), _resident((1, d)), hbm, hbm, hbm, _resident((1, d)), hbm,
                  _resident(cw.shape), _resident((1, d_conv)), _resident((1, d_conv))] + cast_specs,
        out_specs=[_rows(tm, d), _rows(tm, D_SB), _rows(tm, D_SB), _rows(tm, D_SB), _rows(tm, d_conv)]
                  + cast_specs,
        out_shape=[jax.ShapeDtypeStruct((n, d), F32)] + [jax.ShapeDtypeStruct((n, D_SB), BF16)] * 3
                  + [jax.ShapeDtypeStruct((n, d_conv), BF16)]
                  + [jax.ShapeDtypeStruct(w.shape, BF16) for w in cast],
        scratch_shapes=[pltpu.VMEM((tm + 2 * CONV_HALO, d_conv), F32)]
                       + [pltpu.VMEM(w.shape, BF16) for w in own]
                       + [pltpu.VMEM((STAGE_SLOTS, STAGE_ROWS, stage_cols), F32),
                          pltpu.SemaphoreType.DMA((STAGE_SLOTS,))],
        compiler_params=pltpu.CompilerParams(dimension_semantics=("arbitrary",),
                                             vmem_limit_bytes=VMEM_LIMIT),
        name="ffn1_proj_conv",
    )(x, n1, wg, wu, wd, nm, win, cw, cb, cn, *cast)


def _suffix_sum_matrix(tk):
    j = jnp.arange(2 * tk)[:, None] % tk
    s = jnp.arange(2 * tk)[None, :]
    return -jnp.where(s < tk, j >= s, True).astype(BF16)


def _mix_post_call(q, k, v, x1, yc, sbn, wo, n2, wg, wu, wd, fn, *, seq, final):
    n, d = x1.shape
    tm = POST_ROW_TILE
    tq = ATT_BLOCK
    n_tiles = n // tm
    tiles_per_seq = seq // tm
    d_ff = wg.shape[1]
    cur = lambda s: (jnp.minimum(s, n_tiles - 1), 0)
    before_cur = lambda s: (jnp.maximum(jnp.minimum(s, n_tiles - 1) - 1, 0), 0)
    prev = lambda s: (jnp.maximum(s - 1, 0), 0)
    kern = functools.partial(_mix_post_kernel, tiles_per_seq=tiles_per_seq, n_tiles=n_tiles, final=final)
    any_space = pl.BlockSpec(memory_space=pl.ANY)
    return pl.pallas_call(
        kern,
        grid=(n_tiles + 1,),
        in_specs=[pl.BlockSpec((tm, D_SB), cur),
                  pl.BlockSpec((tm, D_SB), cur), pl.BlockSpec((tm, D_SB), before_cur),
                  pl.BlockSpec((tm, D_SB), cur), pl.BlockSpec((tm, D_SB), before_cur),
                  any_space, any_space,
                  _resident((2 * tq, 2 * tq)),
                  pl.BlockSpec((tm, d), prev), pl.BlockSpec((tm, yc.shape[1]), prev),
                  _resident((1, D_SB)), _resident(wo.shape), _resident((1, d)),
                  _resident((d, d_ff)), _resident((d, d_ff)), _resident((d_ff, d)), _resident((1, d))],
        out_specs=pl.BlockSpec((tm, d), prev),
        out_shape=jax.ShapeDtypeStruct((n, d), F32),
        scratch_shapes=[pltpu.VMEM((tm // tq, N_PAIRS, 2 * tq, LANES), BF16),
                        pltpu.VMEM((tm // tq, N_PAIRS * 2 * tq, tq), F32),
                        pltpu.VMEM((tm // tq, N_PAIRS, tq, LANES), F32),
                        pltpu.SMEM((tm // tq,), F32),
                        pltpu.VMEM((2, tm, D_SB), F32),
                        pltpu.VMEM((tq, D_SB), BF16), pltpu.VMEM((tq, D_SB), BF16),
                        pltpu.SemaphoreType.DMA((2,))],
        compiler_params=pltpu.CompilerParams(dimension_semantics=("arbitrary",),
                                             vmem_limit_bytes=VMEM_LIMIT),
        name="attention_out_proj_ffn2",
    )(q, k, k, v, v, k, v, _suffix_sum_matrix(tq), x1, yc, sbn, wo, n2, wg, wu, wd, fn)


def kernel(x, ffn1_norm, ffn1_w_gate, ffn1_w_up, ffn1_w_down, mix_norm, w_in, conv_w, conv_b,
           sb_out_norm, conv_out_norm, w_out, ffn2_norm, ffn2_w_gate, ffn2_w_up, ffn2_w_down,
           final_norm):
    batch, seq, d = x.shape
    depth = ffn1_norm.shape[0]
    assert seq % POST_ROW_TILE == 0 and seq % PRE_ROW_TILE == 0 and w_in.shape[2] == 6 * D_SB
    row = lambda t: t.reshape(1, -1)
    xs = x.reshape(batch * seq, d)
    for l in range(depth):
        x1, q, k, v, yc, wo, wg2, wu2, wd2 = _pre_call(
            xs, row(ffn1_norm[l]), ffn1_w_gate[l], ffn1_w_up[l], ffn1_w_down[l],
            row(mix_norm[l]), w_in[l], conv_w[l],
            row(conv_b[l]), row(conv_out_norm[l]),
            (w_out[l], ffn2_w_gate[l], ffn2_w_up[l], ffn2_w_down[l]), seq=seq)
        xs = _mix_post_call(
            q, k, v, x1, yc, row(sb_out_norm[l]), wo, row(ffn2_norm[l]), wg2, wu2, wd2,
            row(final_norm), seq=seq, final=(l == depth - 1))
    return xs.reshape(batch, seq, d)
```

```python
import functools
import math

import jax
import jax.numpy as jnp
from jax import lax
from jax.experimental import pallas as pl
from jax.experimental.pallas import tpu as pltpu

F32 = jnp.float32
BF16 = jnp.bfloat16

EPS = 1e-6
SB_HEADS = 8
SB_HEAD_DIM = 64
D_SB = SB_HEADS * SB_HEAD_DIM
CONV_WIDTH = 3
LANES = 128
BF16_SUBLANES = 16
N_PAIRS = D_SB // LANES
PRE_ROW_TILE = 512
POST_ROW_TILE = 512
SUB_TILE = 256
FF_CHUNK = 256
ATT_BLOCK = 128
FIRST_PASS_BLOCKS = 3
STAGE_ROWS = 128
STAGE_SLOTS = 4
CONV_HALO = 8
LOG_WEIGHT_UNDERFLOW = -104.0
V7X_VMEM_BYTES = 64 * 1024 * 1024
VMEM_LIMIT = V7X_VMEM_BYTES - 8 * 1024 * 1024


def _rms(x, g):
    return x * lax.rsqrt(jnp.mean(x * x, axis=-1, keepdims=True) + EPS) * g


def _load_as_bf16(pairs, stage, sem):
    slots, slab = stage.shape[0], stage.shape[1]
    jobs = [(src, dst, r) for src, dst in pairs for r in range(0, src.shape[0], slab)]

    def copy(n):
        src, _, r = jobs[n]
        k = n % slots
        return pltpu.make_async_copy(src.at[r:r + slab, :], stage.at[k, :, 0:src.shape[1]], sem.at[k])

    for n in range(min(slots - 1, len(jobs))):
        copy(n).start()
    for n, (src, dst, r) in enumerate(jobs):
        if n + slots - 1 < len(jobs):
            copy(n + slots - 1).start()
        copy(n).wait()
        dst[r:r + slab, :] = stage[n % slots, :, 0:src.shape[1]].astype(BF16)


def _pre_kernel(x_ref, n1_ref, wg_hbm, wu_hbm, wd_hbm, nm_ref, win_hbm, cw_ref, cb_ref, cn_ref,
                *rest, tiles_per_seq, q_scale, n_cast):
    cast_src = rest[:n_cast]
    x1_ref, q_ref, k_ref, v_ref, yc_ref = rest[n_cast:n_cast + 5]
    cast_dst = rest[n_cast + 5:2 * n_cast + 5]
    xc_buf, wg_ref, wu_ref, wd_ref, win_ref, stage, sem = rest[2 * n_cast + 5:]
    tm = x_ref.shape[0]
    i = pl.program_id(0)

    @pl.when(i == 0)
    def _():
        _load_as_bf16([(wg_hbm, wg_ref), (wu_hbm, wu_ref), (wd_hbm, wd_ref), (win_hbm, win_ref)], stage, sem)

    for src, dst in zip(cast_src, cast_dst):
        dst[...] = src[...].astype(BF16)
    first = i % tiles_per_seq == 0

    @pl.when(first)
    def _():
        xc_buf[0:CONV_HALO, :] = jnp.zeros((CONV_HALO, xc_buf.shape[1]), F32)

    @pl.when(jnp.logical_not(first))
    def _():
        xc_buf[0:CONV_HALO, :] = xc_buf[tm:tm + CONV_HALO, :]

    d = D_SB
    subs = [slice(r, r + SUB_TILE) for r in range(0, tm, SUB_TILE)]
    xs = [x_ref[r, :] for r in subs]
    hs = [_rms(x, n1_ref[...]).astype(BF16) for x in xs]
    acts = []
    for h in hs:
        gate = jnp.dot(h, wg_ref[...], preferred_element_type=F32)
        up = jnp.dot(h, wu_ref[...], preferred_element_type=F32)
        acts.append((gate * jax.nn.sigmoid(gate) * up).astype(BF16))
    hs = []
    for r, x, act in zip(subs, xs, acts):
        x1 = x + 0.5 * jnp.dot(act, wd_ref[...], preferred_element_type=F32)
        x1_ref[r, :] = x1
        hs.append(_rms(x1, nm_ref[...]).astype(BF16))
    gates = []
    for r, h in zip(subs, hs):
        proj = jnp.dot(h, win_ref[...], preferred_element_type=F32)
        q_ref[r, :] = (proj[:, 0:d] * q_scale).astype(BF16)
        k_ref[r, :] = proj[:, d:2 * d].astype(BF16)
        v_ref[r, :] = proj[:, 2 * d:3 * d].astype(BF16)
        gates.append(proj[:, 3 * d:4 * d])
        xc_buf[CONV_HALO + r.start:CONV_HALO + r.stop, :] = proj[:, 4 * d:5 * d] * proj[:, 5 * d:6 * d]
    for r, gate_b in zip(subs, gates):
        y = cb_ref[...]
        for j in range(CONV_WIDTH):
            off = CONV_HALO - (CONV_WIDTH - 1) + j + r.start
            y = y + xc_buf[off:off + SUB_TILE, :] * cw_ref[j:j + 1, :]
        yc_ref[r, :] = _rms(gate_b * y, cn_ref[...]).astype(BF16)


def _mix_post_kernel(q_ref, kc_ref, kp_ref, vc_ref, vp_ref, k_hbm, v_hbm, tri_ref, x1_ref, yc_ref,
                     sbn_ref, wo_ref, n2_ref, wg_ref, wu_ref, wd_ref, fn_ref, out_ref,
                     qs_ref, carry_ref, acc_ref, max_ref, ysn_ref, kbuf, vbuf, sem,
                     *, tiles_per_seq, n_tiles, final):
    tm = q_ref.shape[0]
    tq = ATT_BLOCK
    n_q = tm // tq
    rows = N_PAIRS * 2 * tq
    s = pl.program_id(0)
    slot = s % 2
    tile = jnp.minimum(s, n_tiles - 1)
    tile_in_seq = tile % tiles_per_seq
    first_block = tile_in_seq * n_q
    seq_row0 = (tile - tile_in_seq) * tm

    @pl.when(s == 0)
    def _():
        ysn_ref[1] = jnp.zeros(ysn_ref.shape[1:], BF16)

    lane = lax.broadcasted_iota(jnp.int32, (tq, LANES), 1)
    low = lane < SB_HEAD_DIM
    row = lax.broadcasted_iota(jnp.int32, (rows, tq), 0)
    col = lax.broadcasted_iota(jnp.int32, (rows, tq), 1)
    strict = col < (row & (tq - 1))
    tri_col = lax.broadcasted_iota(jnp.int32, tri_ref.shape, 1)
    zero = jnp.zeros((), BF16)
    prev_shift = jnp.where(tile_in_seq == 0, 2 * tq, 0)

    early = tq // 2
    head_rows = [p * 2 * tq + h * tq for p in range(N_PAIRS) for h in range(2)]

    def early_of(x):
        return jnp.concatenate([x[r:r + early] for r in head_rows], axis=0)

    def with_early(full, part, pad=False):
        pieces = []
        for n, r in enumerate(head_rows):
            rest = jnp.zeros((tq - early, part.shape[1]), part.dtype) if pad else full[r + early:r + tq]
            pieces += [part[n * early:(n + 1) * early], rest]
        return jnp.concatenate(pieces, axis=0)

    def block_weights(j, k_blk, v_blk, shift, carry, diagonal, only_early=False, skip_early=None):
        if shift is None:
            keep_k = lo_half = hi_half = tri = None
        else:
            k_lane = lane + shift
            keep_k, lo_half = k_lane < LANES, k_lane < SB_HEAD_DIM
            hi_half = (k_lane >= SB_HEAD_DIM) & keep_k
            tri = jnp.where(tri_col + shift < 2 * tq, tri_ref[...], zero)
        if only_early:
            q_stack = lambda p: jnp.concatenate([qs_ref[j, p, 0:early], qs_ref[j, p, tq:tq + early]], axis=0)
        else:
            q_stack = lambda p: qs_ref[j, p]
        z = jnp.concatenate(
            [lax.dot_general(q_stack(p), k_blk(p) if shift is None else jnp.where(keep_k, k_blk(p), zero),
                             (((1,), (1,)), ((), ())), preferred_element_type=F32)
             for p in range(N_PAIRS)], axis=0)
        sp = jnp.maximum(z, 0.0) + jnp.log(1.0 + jnp.exp(-jnp.abs(z)))
        if diagonal:
            sp = jnp.where(strict, sp, 0.0)
        if skip_early is not None:
            taking_part = (row & (tq - 1)) + jnp.where(skip_early, 0, tq) >= early
            sp = jnp.where(taking_part, sp, 0.0)
        hi = sp.astype(BF16)
        lo = (sp - hi.astype(F32)).astype(BF16)
        sums = jnp.dot(jnp.concatenate([hi, lo], axis=1), tri_ref[...] if shift is None else tri,
                       preferred_element_type=F32)
        logit = z + sums[:, 0:tq]
        if carry is not None:
            logit = logit + carry
        a = jnp.exp(logit)
        if diagonal:
            a = jnp.where(strict, a, 0.0)
        if skip_early is not None:
            a = jnp.where(taking_part, a, 0.0)
        carry = sums[:, tq:2 * tq] if carry is None else carry + sums[:, tq:2 * tq]
        halves = (low, ~low) if shift is None else (lo_half, hi_half)
        v2 = [jnp.concatenate([jnp.where(m, v_blk(p), zero) for m in halves], axis=0)
              for p in range(N_PAIRS)]
        return a.astype(BF16), carry, v2

    def apply_values(j, weights, values, carry, first):
        carry_ref[j] = carry
        heads = []
        for p in range(N_PAIRS):
            r0 = p * 2 * tq
            a2 = jnp.concatenate([w[r:r + tq] for w in weights for r in (r0, r0 + tq)], axis=1)
            v2 = jnp.concatenate([v[p] for v in values], axis=0)
            out = jnp.dot(a2, v2, preferred_element_type=F32)
            heads.append(out if first else acc_ref[j, p] + out)
            acc_ref[j, p] = heads[-1]
        max_ref[j] = jnp.max(carry)
        ysn_ref[slot, j * tq:(j + 1) * tq, :] = _rms(jnp.concatenate(heads, axis=1), sbn_ref[...]).astype(BF16)

    for j in range(n_q):
        for p in range(N_PAIRS):
            q2 = q_ref[j * tq:(j + 1) * tq, p * LANES:(p + 1) * LANES]
            qs_ref[j, p] = jnp.concatenate([jnp.where(low, q2, zero), jnp.where(low, zero, q2)], axis=0)

    state = {}

    def attention_item(j, b):
        weights, values, carry = state.get(j, ([], [], None))
        if j >= b:
            k_src, v_src, r0, shift = kc_ref, vc_ref, (j - b) * tq, None
        else:
            k_src, v_src, r0, shift = kp_ref, vp_ref, (n_q + j - b) * tq, prev_shift
        k_blk = lambda p: k_src[r0:r0 + tq, p * LANES:(p + 1) * LANES]
        v_blk = lambda p: v_src[r0:r0 + tq, p * LANES:(p + 1) * LANES]
        if b < FIRST_PASS_BLOCKS - 1:
            a, carry, v2 = block_weights(j, k_blk, v_blk, shift, carry, diagonal=(b == 0))
        else:
            a, part, v2 = block_weights(j, k_blk, v_blk, shift, early_of(carry), False, only_early=True)
            a, carry = with_early(None, a, pad=True), with_early(carry, part)
        state[j] = (weights + [a], values + [v2], carry)
        if b == FIRST_PASS_BLOCKS - 1:
            apply_values(j, *state[j], first=True)

    items = [(j, b) for j in range(n_q) for b in range(FIRST_PASS_BLOCKS)]
    d_ff = wg_ref.shape[1]
    chunks = [slice(c, c + FF_CHUNK) for c in range(0, d_ff, FF_CHUNK)]
    subs = [slice(r, r + SUB_TILE) for r in range(0, tm, SUB_TILE)]
    slots = [(t, c) for t in range(len(subs)) for c in range(len(chunks))]
    after_slot = {}
    for n, item in enumerate(items):
        after_slot.setdefault(slots[n * len(slots) // len(items)], []).append(item)

    x2s, hs = [], []
    for r in subs:
        y = jnp.concatenate([ysn_ref[1 - slot, r, :], yc_ref[r, :]], axis=1)
        x2s.append(x1_ref[r, :] + jnp.dot(y, wo_ref[...], preferred_element_type=F32))
        hs.append(_rms(x2s[-1], n2_ref[...]).astype(BF16))
    acts = [[] for _ in subs]
    for t, h in enumerate(hs):
        for ci, c in enumerate(chunks):
            gate = jnp.dot(h, wg_ref[:, c], preferred_element_type=F32)
            up = jnp.dot(h, wu_ref[:, c], preferred_element_type=F32)
            acts[t].append((gate * jax.nn.sigmoid(gate) * up).astype(BF16))
            for item in after_slot.get((t, ci), []):
                attention_item(*item)
    for r, x2, act in zip(subs, x2s, acts):
        x3 = x2 + 0.5 * jnp.dot(jnp.concatenate(act, axis=1), wd_ref[...], preferred_element_type=F32)
        out_ref[r, :] = _rms(x3, fn_ref[...]) if final else x3

    for j in range(n_q):
        def cond(state):
            kb, m = state
            return jnp.logical_and(kb >= 0, m > LOG_WEIGHT_UNDERFLOW)

        def body(state, j=j):
            kb, _ = state
            r0 = pl.multiple_of(seq_row0 + kb * tq, tq)
            copies = [pltpu.make_async_copy(src.at[pl.ds(r0, tq), :], dst, sem.at[n])
                      for n, (src, dst) in enumerate(((k_hbm, kbuf), (v_hbm, vbuf)))]
            for cp in copies:
                cp.start()
            for cp in copies:
                cp.wait()
            a, carry, v2 = block_weights(
                j, lambda p: kbuf[:, p * LANES:(p + 1) * LANES], lambda p: vbuf[:, p * LANES:(p + 1) * LANES],
                None, carry_ref[j], diagonal=False, skip_early=(kb == resume))
            apply_values(j, [a], [v2], carry, first=False)
            return kb - 1, max_ref[j]

        resume = first_block + j - (FIRST_PASS_BLOCKS - 1)
        lax.while_loop(cond, body, (resume, max_ref[j]))


def _resident(shape):
    return pl.BlockSpec(shape, lambda *_: (0,) * len(shape), pipeline_mode=pl.Buffered(1))


def _rows(tm, width):
    return pl.BlockSpec((tm, width), lambda i: (i, 0))


def _slab_spec(shape, steps):
    rows, cols = shape
    slab = next(r for r in range(BF16_SUBLANES, rows + 1, BF16_SUBLANES) if rows % r == 0 and r * steps >= rows)
    last = rows // slab - 1
    return pl.BlockSpec((slab, cols), lambda i: (jnp.minimum(i, last), 0))


def _pre_call(x, n1, wg, wu, wd, nm, win, cw, cb, cn, cast, *, seq):
    n, d = x.shape
    tm = PRE_ROW_TILE
    steps = n // tm
    d_ff = wg.shape[1]
    d_conv = cw.shape[1]
    kern = functools.partial(_pre_kernel, tiles_per_seq=seq // tm, q_scale=1.0 / math.sqrt(SB_HEAD_DIM),
                             n_cast=len(cast))
    cast_specs = [_slab_spec(w.shape, steps) for w in cast]
    hbm = pl.BlockSpec(memory_space=pl.ANY)
    own = (wg, wu, wd, win)
    assert all(w.shape[0] % STAGE_ROWS == 0 for w in own)
    stage_cols = max(w.shape[1] for w in own)
    return pl.pallas_call(
        kern,
        grid=(steps,),
        in_specs=[_rows(tm, d), _resident((1, d)), hbm, hbm, hbm, _resident((1, d)), hbm,
                  _resident(cw.shape), _resident((1, d_conv)), _resident((1, d_conv))] + cast_specs,
        out_specs=[_rows(tm, d), _rows(tm, D_SB), _rows(tm, D_SB), _rows(tm, D_SB), _rows(tm, d_conv)]
                  + cast_specs,
        out_shape=[jax.ShapeDtypeStruct((n, d), F32)] + [jax.ShapeDtypeStruct((n, D_SB), BF16)] * 3
                  + [jax.ShapeDtypeStruct((n, d_conv), BF16)]
                  + [jax.ShapeDtypeStruct(w.shape, BF16) for w in cast],
        scratch_shapes=[pltpu.VMEM((tm + 2 * CONV_HALO, d_conv), F32)]
                       + [pltpu.VMEM(w.shape, BF16) for w in own]
                       + [pltpu.VMEM((STAGE_SLOTS, STAGE_ROWS, stage_cols), F32),
                          pltpu.SemaphoreType.DMA((STAGE_SLOTS,))],
        compiler_params=pltpu.CompilerParams(dimension_semantics=("arbitrary",),
                                             vmem_limit_bytes=VMEM_LIMIT),
        name="ffn1_proj_conv",
    )(x, n1, wg, wu, wd, nm, win, cw, cb, cn, *cast)


def _suffix_sum_matrix(tk):
    j = jnp.arange(2 * tk)[:, None] % tk
    s = jnp.arange(2 * tk)[None, :]
    return -jnp.where(s < tk, j >= s, True).astype(BF16)


def _mix_post_call(q, k, v, x1, yc, sbn, wo, n2, wg, wu, wd, fn, *, seq, final):
    n, d = x1.shape
    tm = POST_ROW_TILE
    tq = ATT_BLOCK
    n_tiles = n // tm
    tiles_per_seq = seq // tm
    d_ff = wg.shape[1]
    cur = lambda s: (jnp.minimum(s, n_tiles - 1), 0)
    before_cur = lambda s: (jnp.maximum(jnp.minimum(s, n_tiles - 1) - 1, 0), 0)
    prev = lambda s: (jnp.maximum(s - 1, 0), 0)
    kern = functools.partial(_mix_post_kernel, tiles_per_seq=tiles_per_seq, n_tiles=n_tiles, final=final)
    any_space = pl.BlockSpec(memory_space=pl.ANY)
    return pl.pallas_call(
        kern,
        grid=(n_tiles + 1,),
        in_specs=[pl.BlockSpec((tm, D_SB), cur),
                  pl.BlockSpec((tm, D_SB), cur), pl.BlockSpec((tm, D_SB), before_cur),
                  pl.BlockSpec((tm, D_SB), cur), pl.BlockSpec((tm, D_SB), before_cur),
                  any_space, any_space,
                  _resident((2 * tq, 2 * tq)),
                  pl.BlockSpec((tm, d), prev), pl.BlockSpec((tm, yc.shape[1]), prev),
                  _resident((1, D_SB)), _resident(wo.shape), _resident((1, d)),
                  _resident((d, d_ff)), _resident((d, d_ff)), _resident((d_ff, d)), _resident((1, d))],
        out_specs=pl.BlockSpec((tm, d), prev),
        out_shape=jax.ShapeDtypeStruct((n, d), F32),
        scratch_shapes=[pltpu.VMEM((tm // tq, N_PAIRS, 2 * tq, LANES), BF16),
                        pltpu.VMEM((tm // tq, N_PAIRS * 2 * tq, tq), F32),
                        pltpu.VMEM((tm // tq, N_PAIRS, tq, LANES), F32),
                        pltpu.SMEM((tm // tq,), F32),
                        pltpu.VMEM((2, tm, D_SB), BF16),
                        pltpu.VMEM((tq, D_SB), BF16), pltpu.VMEM((tq, D_SB), BF16),
                        pltpu.SemaphoreType.DMA((2,))],
        compiler_params=pltpu.CompilerParams(dimension_semantics=("arbitrary",),
                                             vmem_limit_bytes=VMEM_LIMIT),
        name="attention_out_proj_ffn2",
    )(q, k, k, v, v, k, v, _suffix_sum_matrix(tq), x1, yc, sbn, wo, n2, wg, wu, wd, fn)


def kernel(x, ffn1_norm, ffn1_w_gate, ffn1_w_up, ffn1_w_down, mix_norm, w_in, conv_w, conv_b,
           sb_out_norm, conv_out_norm, w_out, ffn2_norm, ffn2_w_gate, ffn2_w_up, ffn2_w_down,
           final_norm):
    batch, seq, d = x.shape
    depth = ffn1_norm.shape[0]
    assert seq % POST_ROW_TILE == 0 and seq % PRE_ROW_TILE == 0 and w_in.shape[2] == 6 * D_SB
    row = lambda t: t.reshape(1, -1)
    xs = x.reshape(batch * seq, d)
    for l in range(depth):
        x1, q, k, v, yc, wo, wg2, wu2, wd2 = _pre_call(
            xs, row(ffn1_norm[l]), ffn1_w_gate[l], ffn1_w_up[l], ffn1_w_down[l],
            row(mix_norm[l]), w_in[l], conv_w[l],
            row(conv_b[l]), row(conv_out_norm[l]),
            (w_out[l], ffn2_w_gate[l], ffn2_w_up[l], ffn2_w_down[l]), seq=seq)
        xs = _mix_post_call(
            q, k, v, x1, yc, row(sb_out_norm[l]), wo, row(ffn2_norm[l]), wg2, wu2, wd2,
            row(final_norm), seq=seq, final=(l == depth - 1))
    return xs.reshape(batch, seq, d)
```

```python
import functools
import math

import jax
import jax.numpy as jnp
from jax import lax
from jax.experimental import pallas as pl
from jax.experimental.pallas import tpu as pltpu

F32 = jnp.float32
BF16 = jnp.bfloat16

EPS = 1e-6
SB_HEADS = 8
SB_HEAD_DIM = 64
D_SB = SB_HEADS * SB_HEAD_DIM
CONV_WIDTH = 3
LANES = 128
BF16_SUBLANES = 16
N_PAIRS = D_SB // LANES
PRE_ROW_TILE = 512
POST_ROW_TILE = 512
SUB_TILE = 256
FF_CHUNK = 256
ATT_BLOCK = 128
FIRST_PASS_BLOCKS = 3
STAGE_ROWS = 128
STAGE_SLOTS = 4
CONV_HALO = 8
LOG_WEIGHT_UNDERFLOW = -104.0
V7X_VMEM_BYTES = 64 * 1024 * 1024
VMEM_LIMIT = V7X_VMEM_BYTES - 8 * 1024 * 1024


def _rms(x, g):
    return x * lax.rsqrt(jnp.mean(x * x, axis=-1, keepdims=True) + EPS) * g


def _load_as_bf16(pairs, stage, sem):
    slots, slab = stage.shape[0], stage.shape[1]
    jobs = [(src, dst, r) for src, dst in pairs for r in range(0, src.shape[0], slab)]

    def copy(n):
        src, _, r = jobs[n]
        k = n % slots
        return pltpu.make_async_copy(src.at[r:r + slab, :], stage.at[k, :, 0:src.shape[1]], sem.at[k])

    for n in range(min(slots - 1, len(jobs))):
        copy(n).start()
    for n, (src, dst, r) in enumerate(jobs):
        if n + slots - 1 < len(jobs):
            copy(n + slots - 1).start()
        copy(n).wait()
        dst[r:r + slab, :] = stage[n % slots, :, 0:src.shape[1]].astype(BF16)


def _pre_kernel(x_ref, n1_ref, wg_hbm, wu_hbm, wd_hbm, nm_ref, win_hbm, cw_ref, cb_ref, cn_ref,
                *rest, tiles_per_seq, q_scale, n_cast):
    cast_src = rest[:n_cast]
    x1_ref, q_ref, k_ref, v_ref, yc_ref = rest[n_cast:n_cast + 5]
    cast_dst = rest[n_cast + 5:2 * n_cast + 5]
    xc_buf, wg_ref, wu_ref, wd_ref, win_ref, stage, sem = rest[2 * n_cast + 5:]
    tm = x_ref.shape[0]
    i = pl.program_id(0)

    @pl.when(i == 0)
    def _():
        _load_as_bf16([(wg_hbm, wg_ref), (wu_hbm, wu_ref), (wd_hbm, wd_ref), (win_hbm, win_ref)], stage, sem)

    for src, dst in zip(cast_src, cast_dst):
        dst[...] = src[...].astype(BF16)
    first = i % tiles_per_seq == 0

    @pl.when(first)
    def _():
        xc_buf[0:CONV_HALO, :] = jnp.zeros((CONV_HALO, xc_buf.shape[1]), F32)

    @pl.when(jnp.logical_not(first))
    def _():
        xc_buf[0:CONV_HALO, :] = xc_buf[tm:tm + CONV_HALO, :]

    d = D_SB
    subs = [slice(r, r + SUB_TILE) for r in range(0, tm, SUB_TILE)]
    xs = [x_ref[r, :] for r in subs]
    hs = [_rms(x, n1_ref[...]).astype(BF16) for x in xs]
    acts = []
    for h in hs:
        parts = []
        for c in range(0, wg_ref.shape[1], FF_CHUNK):
            gate = jnp.dot(h, wg_ref[:, c:c + FF_CHUNK], preferred_element_type=F32)
            up = jnp.dot(h, wu_ref[:, c:c + FF_CHUNK], preferred_element_type=F32)
            parts.append((gate * jax.nn.sigmoid(gate) * up).astype(BF16))
        acts.append(jnp.concatenate(parts, axis=1))
    hs = []
    for r, x, act in zip(subs, xs, acts):
        x1 = x + 0.5 * jnp.dot(act, wd_ref[...], preferred_element_type=F32)
        x1_ref[r, :] = x1
        hs.append(_rms(x1, nm_ref[...]).astype(BF16))
    gates = []
    for r, h in zip(subs, hs):
        proj = jnp.dot(h, win_ref[...], preferred_element_type=F32)
        q_ref[r, :] = (proj[:, 0:d] * q_scale).astype(BF16)
        k_ref[r, :] = proj[:, d:2 * d].astype(BF16)
        v_ref[r, :] = proj[:, 2 * d:3 * d].astype(BF16)
        gates.append(proj[:, 3 * d:4 * d])
        xc_buf[CONV_HALO + r.start:CONV_HALO + r.stop, :] = proj[:, 4 * d:5 * d] * proj[:, 5 * d:6 * d]
    for r, gate_b in zip(subs, gates):
        y = cb_ref[...]
        for j in range(CONV_WIDTH):
            off = CONV_HALO - (CONV_WIDTH - 1) + j + r.start
            y = y + xc_buf[off:off + SUB_TILE, :] * cw_ref[j:j + 1, :]
        yc_ref[r, :] = _rms(gate_b * y, cn_ref[...]).astype(BF16)


def _mix_post_kernel(q_ref, kc_ref, kp_ref, vc_ref, vp_ref, k_hbm, v_hbm, tri_ref, x1_ref, yc_ref,
                     sbn_ref, wo_ref, n2_ref, wg_ref, wu_ref, wd_ref, fn_ref, out_ref,
                     qs_ref, carry_ref, acc_ref, max_ref, ysn_ref, kbuf, vbuf, sem,
                     *, tiles_per_seq, n_tiles, final):
    tm = q_ref.shape[0]
    tq = ATT_BLOCK
    n_q = tm // tq
    rows = N_PAIRS * 2 * tq
    s = pl.program_id(0)
    slot = s % 2
    tile = jnp.minimum(s, n_tiles - 1)
    tile_in_seq = tile % tiles_per_seq
    first_block = tile_in_seq * n_q
    seq_row0 = (tile - tile_in_seq) * tm

    @pl.when(s == 0)
    def _():
        ysn_ref[1] = jnp.zeros(ysn_ref.shape[1:], BF16)

    lane = lax.broadcasted_iota(jnp.int32, (tq, LANES), 1)
    low = lane < SB_HEAD_DIM
    row = lax.broadcasted_iota(jnp.int32, (rows, tq), 0)
    col = lax.broadcasted_iota(jnp.int32, (rows, tq), 1)
    strict = col < (row & (tq - 1))
    tri_col = lax.broadcasted_iota(jnp.int32, tri_ref.shape, 1)
    zero = jnp.zeros((), BF16)
    prev_shift = jnp.where(tile_in_seq == 0, 2 * tq, 0)

    early = tq // 2
    head_rows = [p * 2 * tq + h * tq for p in range(N_PAIRS) for h in range(2)]

    def early_of(x):
        return jnp.concatenate([x[r:r + early] for r in head_rows], axis=0)

    def with_early(full, part, pad=False):
        pieces = []
        for n, r in enumerate(head_rows):
            rest = jnp.zeros((tq - early, part.shape[1]), part.dtype) if pad else full[r + early:r + tq]
            pieces += [part[n * early:(n + 1) * early], rest]
        return jnp.concatenate(pieces, axis=0)

    def block_weights(j, k_blk, v_blk, shift, carry, diagonal, only_early=False, skip_early=None):
        if shift is None:
            keep_k = lo_half = hi_half = tri = None
        else:
            k_lane = lane + shift
            keep_k, lo_half = k_lane < LANES, k_lane < SB_HEAD_DIM
            hi_half = (k_lane >= SB_HEAD_DIM) & keep_k
            tri = jnp.where(tri_col + shift < 2 * tq, tri_ref[...], zero)
        if only_early:
            q_stack = lambda p: jnp.concatenate([qs_ref[j, p, 0:early], qs_ref[j, p, tq:tq + early]], axis=0)
        else:
            q_stack = lambda p: qs_ref[j, p]
        z = jnp.concatenate(
            [lax.dot_general(q_stack(p), k_blk(p) if shift is None else jnp.where(keep_k, k_blk(p), zero),
                             (((1,), (1,)), ((), ())), preferred_element_type=F32)
             for p in range(N_PAIRS)], axis=0)
        sp = jnp.maximum(z, 0.0) + jnp.log(1.0 + jnp.exp(-jnp.abs(z)))
        if diagonal:
            sp = jnp.where(strict, sp, 0.0)
        if skip_early is not None:
            taking_part = (row & (tq - 1)) + jnp.where(skip_early, 0, tq) >= early
            sp = jnp.where(taking_part, sp, 0.0)
        hi = sp.astype(BF16)
        lo = (sp - hi.astype(F32)).astype(BF16)
        sums = jnp.dot(jnp.concatenate([hi, lo], axis=1), tri_ref[...] if shift is None else tri,
                       preferred_element_type=F32)
        logit = z + sums[:, 0:tq]
        if carry is not None:
            logit = logit + carry
        a = jnp.exp(logit)
        if diagonal:
            a = jnp.where(strict, a, 0.0)
        if skip_early is not None:
            a = jnp.where(taking_part, a, 0.0)
        carry = sums[:, tq:2 * tq] if carry is None else carry + sums[:, tq:2 * tq]
        halves = (low, ~low) if shift is None else (lo_half, hi_half)
        v2 = [jnp.concatenate([jnp.where(m, v_blk(p), zero) for m in halves], axis=0)
              for p in range(N_PAIRS)]
        return a.astype(BF16), carry, v2

    def apply_values(j, weights, values, carry, first):
        carry_ref[j] = carry
        heads = []
        for p in range(N_PAIRS):
            r0 = p * 2 * tq
            a2 = jnp.concatenate([w[r:r + tq] for w in weights for r in (r0, r0 + tq)], axis=1)
            v2 = jnp.concatenate([v[p] for v in values], axis=0)
            out = jnp.dot(a2, v2, preferred_element_type=F32)
            heads.append(out if first else acc_ref[j, p] + out)
            acc_ref[j, p] = heads[-1]
        max_ref[j] = jnp.max(carry)
        ysn_ref[slot, j * tq:(j + 1) * tq, :] = _rms(jnp.concatenate(heads, axis=1), sbn_ref[...]).astype(BF16)

    for j in range(n_q):
        for p in range(N_PAIRS):
            q2 = q_ref[j * tq:(j + 1) * tq, p * LANES:(p + 1) * LANES]
            qs_ref[j, p] = jnp.concatenate([jnp.where(low, q2, zero), jnp.where(low, zero, q2)], axis=0)

    state = {}

    def attention_item(j, b):
        weights, values, carry = state.get(j, ([], [], None))
        if j >= b:
            k_src, v_src, r0, shift = kc_ref, vc_ref, (j - b) * tq, None
        else:
            k_src, v_src, r0, shift = kp_ref, vp_ref, (n_q + j - b) * tq, prev_shift
        k_blk = lambda p: k_src[r0:r0 + tq, p * LANES:(p + 1) * LANES]
        v_blk = lambda p: v_src[r0:r0 + tq, p * LANES:(p + 1) * LANES]
        if b < FIRST_PASS_BLOCKS - 1:
            a, carry, v2 = block_weights(j, k_blk, v_blk, shift, carry, diagonal=(b == 0))
        else:
            a, part, v2 = block_weights(j, k_blk, v_blk, shift, early_of(carry), False, only_early=True)
            a, carry = with_early(None, a, pad=True), with_early(carry, part)
        state[j] = (weights + [a], values + [v2], carry)
        if b == FIRST_PASS_BLOCKS - 1:
            apply_values(j, *state[j], first=True)

    items = [(j, b) for j in range(n_q) for b in range(FIRST_PASS_BLOCKS)]
    d_ff = wg_ref.shape[1]
    chunks = [slice(c, c + FF_CHUNK) for c in range(0, d_ff, FF_CHUNK)]
    subs = [slice(r, r + SUB_TILE) for r in range(0, tm, SUB_TILE)]
    slots = [(t, c) for t in range(len(subs)) for c in range(len(chunks))]
    after_slot = {}
    for n, item in enumerate(items):
        after_slot.setdefault(slots[n * len(slots) // len(items)], []).append(item)

    x2s, hs = [], []
    for r in subs:
        y = jnp.concatenate([ysn_ref[1 - slot, r, :], yc_ref[r, :]], axis=1)
        x2s.append(x1_ref[r, :] + jnp.dot(y, wo_ref[...], preferred_element_type=F32))
        hs.append(_rms(x2s[-1], n2_ref[...]).astype(BF16))
    acts = [[] for _ in subs]
    for t, h in enumerate(hs):
        for ci, c in enumerate(chunks):
            gate = jnp.dot(h, wg_ref[:, c], preferred_element_type=F32)
            up = jnp.dot(h, wu_ref[:, c], preferred_element_type=F32)
            acts[t].append((gate * jax.nn.sigmoid(gate) * up).astype(BF16))
            for item in after_slot.get((t, ci), []):
                attention_item(*item)
    for r, x2, act in zip(subs, x2s, acts):
        x3 = x2 + 0.5 * jnp.dot(jnp.concatenate(act, axis=1), wd_ref[...], preferred_element_type=F32)
        out_ref[r, :] = _rms(x3, fn_ref[...]) if final else x3

    for j in range(n_q):
        def cond(state):
            kb, m = state
            return jnp.logical_and(kb >= 0, m > LOG_WEIGHT_UNDERFLOW)

        def body(state, j=j):
            kb, _ = state
            r0 = pl.multiple_of(seq_row0 + kb * tq, tq)
            copies = [pltpu.make_async_copy(src.at[pl.ds(r0, tq), :], dst, sem.at[n])
                      for n, (src, dst) in enumerate(((k_hbm, kbuf), (v_hbm, vbuf)))]
            for cp in copies:
                cp.start()
            for cp in copies:
                cp.wait()
            a, carry, v2 = block_weights(
                j, lambda p: kbuf[:, p * LANES:(p + 1) * LANES], lambda p: vbuf[:, p * LANES:(p + 1) * LANES],
                None, carry_ref[j], diagonal=False, skip_early=(kb == resume))
            apply_values(j, [a], [v2], carry, first=False)
            return kb - 1, max_ref[j]

        resume = first_block + j - (FIRST_PASS_BLOCKS - 1)
        lax.while_loop(cond, body, (resume, max_ref[j]))


def _resident(shape):
    return pl.BlockSpec(shape, lambda *_: (0,) * len(shape), pipeline_mode=pl.Buffered(1))


def _rows(tm, width):
    return pl.BlockSpec((tm, width), lambda i: (i, 0))


def _slab_spec(shape, steps):
    rows, cols = shape
    slab = next(r for r in range(BF16_SUBLANES, rows + 1, BF16_SUBLANES) if rows % r == 0 and r * steps >= rows)
    last = rows // slab - 1
    return pl.BlockSpec((slab, cols), lambda i: (jnp.minimum(i, last), 0))


def _pre_call(x, n1, wg, wu, wd, nm, win, cw, cb, cn, cast, *, seq):
    n, d = x.shape
    tm = PRE_ROW_TILE
    steps = n // tm
    d_ff = wg.shape[1]
    d_conv = cw.shape[1]
    kern = functools.partial(_pre_kernel, tiles_per_seq=seq // tm, q_scale=1.0 / math.sqrt(SB_HEAD_DIM),
                             n_cast=len(cast))
    cast_specs = [_slab_spec(w.shape, steps) for w in cast]
    hbm = pl.BlockSpec(memory_space=pl.ANY)
    own = (wg, wu, wd, win)
    assert all(w.shape[0] % STAGE_ROWS == 0 for w in own)
    stage_cols = max(w.shape[1] for w in own)
    return pl.pallas_call(
        kern,
        grid=(steps,),
        in_specs=[_rows(tm, d), _resident((1, d)), hbm, hbm, hbm, _resident((1, d)), hbm,
                  _resident(cw.shape), _resident((1, d_conv)), _resident((1, d_conv))] + cast_specs,
        out_specs=[_rows(tm, d), _rows(tm, D_SB), _rows(tm, D_SB), _rows(tm, D_SB), _rows(tm, d_conv)]
                  + cast_specs,
        out_shape=[jax.ShapeDtypeStruct((n, d), F32)] + [jax.ShapeDtypeStruct((n, D_SB), BF16)] * 3
                  + [jax.ShapeDtypeStruct((n, d_conv), BF16)]
                  + [jax.ShapeDtypeStruct(w.shape, BF16) for w in cast],
        scratch_shapes=[pltpu.VMEM((tm + 2 * CONV_HALO, d_conv), F32)]
                       + [pltpu.VMEM(w.shape, BF16) for w in own]
                       + [pltpu.VMEM((STAGE_SLOTS, STAGE_ROWS, stage_cols), F32),
                          pltpu.SemaphoreType.DMA((STAGE_SLOTS,))],
        compiler_params=pltpu.CompilerParams(dimension_semantics=("arbitrary",),
                                             vmem_limit_bytes=VMEM_LIMIT),
        name="ffn1_proj_conv",
    )(x, n1, wg, wu, wd, nm, win, cw, cb, cn, *cast)


def _suffix_sum_matrix(tk):
    j = jnp.arange(2 * tk)[:, None] % tk
    s = jnp.arange(2 * tk)[None, :]
    return -jnp.where(s < tk, j >= s, True).astype(BF16)


def _mix_post_call(q, k, v, x1, yc, sbn, wo, n2, wg, wu, wd, fn, *, seq, final):
    n, d = x1.shape
    tm = POST_ROW_TILE
    tq = ATT_BLOCK
    n_tiles = n // tm
    tiles_per_seq = seq // tm
    d_ff = wg.shape[1]
    cur = lambda s: (jnp.minimum(s, n_tiles - 1), 0)
    before_cur = lambda s: (jnp.maximum(jnp.minimum(s, n_tiles - 1) - 1, 0), 0)
    prev = lambda s: (jnp.maximum(s - 1, 0), 0)
    kern = functools.partial(_mix_post_kernel, tiles_per_seq=tiles_per_seq, n_tiles=n_tiles, final=final)
    any_space = pl.BlockSpec(memory_space=pl.ANY)
    return pl.pallas_call(
        kern,
        grid=(n_tiles + 1,),
        in_specs=[pl.BlockSpec((tm, D_SB), cur),
                  pl.BlockSpec((tm, D_SB), cur), pl.BlockSpec((tm, D_SB), before_cur),
                  pl.BlockSpec((tm, D_SB), cur), pl.BlockSpec((tm, D_SB), before_cur),
                  any_space, any_space,
                  _resident((2 * tq, 2 * tq)),
                  pl.BlockSpec((tm, d), prev), pl.BlockSpec((tm, yc.shape[1]), prev),
                  _resident((1, D_SB)), _resident(wo.shape), _resident((1, d)),
                  _resident((d, d_ff)), _resident((d, d_ff)), _resident((d_ff, d)), _resident((1, d))],
        out_specs=pl.BlockSpec((tm, d), prev),
        out_shape=jax.ShapeDtypeStruct((n, d), F32),
        scratch_shapes=[pltpu.VMEM((tm // tq, N_PAIRS, 2 * tq, LANES), BF16),
                        pltpu.VMEM((tm // tq, N_PAIRS * 2 * tq, tq), F32),
                        pltpu.VMEM((tm // tq, N_PAIRS, tq, LANES), F32),
                        pltpu.SMEM((tm // tq,), F32),
                        pltpu.VMEM((2, tm, D_SB), BF16),
                        pltpu.VMEM((tq, D_SB), BF16), pltpu.VMEM((tq, D_SB), BF16),
                        pltpu.SemaphoreType.DMA((2,))],
        compiler_params=pltpu.CompilerParams(dimension_semantics=("arbitrary",),
                                             vmem_limit_bytes=VMEM_LIMIT),
        name="attention_out_proj_ffn2",
    )(q, k, k, v, v, k, v, _suffix_sum_matrix(tq), x1, yc, sbn, wo, n2, wg, wu, wd, fn)


def kernel(x, ffn1_norm, ffn1_w_gate, ffn1_w_up, ffn1_w_down, mix_norm, w_in, conv_w, conv_b,
           sb_out_norm, conv_out_norm, w_out, ffn2_norm, ffn2_w_gate, ffn2_w_up, ffn2_w_down,
           final_norm):
    batch, seq, d = x.shape
    depth = ffn1_norm.shape[0]
    assert seq % POST_ROW_TILE == 0 and seq % PRE_ROW_TILE == 0 and w_in.shape[2] == 6 * D_SB
    row = lambda t: t.reshape(1, -1)
    xs = x.reshape(batch * seq, d)
    for l in range(depth):
        x1, q, k, v, yc, wo, wg2, wu2, wd2 = _pre_call(
            xs, row(ffn1_norm[l]), ffn1_w_gate[l], ffn1_w_up[l], ffn1_w_down[l],
            row(mix_norm[l]), w_in[l], conv_w[l],
            row(conv_b[l]), row(conv_out_norm[l]),
            (w_out[l], ffn2_w_gate[l], ffn2_w_up[l], ffn2_w_down[l]), seq=seq)
        xs = _mix_post_call(
            q, k, v, x1, yc, row(sb_out_norm[l]), wo, row(ffn2_norm[l]), wg2, wu2, wd2,
            row(final_norm), seq=seq, final=(l == depth - 1))
    return xs.reshape(batch, seq, d)
```

```python
import functools
import math

import jax
import jax.numpy as jnp
from jax import lax
from jax.experimental import pallas as pl
from jax.experimental.pallas import tpu as pltpu

F32 = jnp.float32
BF16 = jnp.bfloat16

EPS = 1e-6
SB_HEADS = 8
SB_HEAD_DIM = 64
D_SB = SB_HEADS * SB_HEAD_DIM
CONV_WIDTH = 3
LANES = 128
BF16_SUBLANES = 16
N_PAIRS = D_SB // LANES
PRE_ROW_TILE = 512
POST_ROW_TILE = 512
SUB_TILE = 256
FF_CHUNK = 256
ATT_BLOCK = 128
FIRST_PASS_BLOCKS = 3
STAGE_ROWS = 128
STAGE_SLOTS = 4
CONV_HALO = 8
LOG_WEIGHT_UNDERFLOW = -104.0
V7X_VMEM_BYTES = 64 * 1024 * 1024
VMEM_LIMIT = V7X_VMEM_BYTES - 8 * 1024 * 1024


def _rms(x, g):
    return x * lax.rsqrt(jnp.mean(x * x, axis=-1, keepdims=True) + EPS) * g


def _load_as_bf16(pairs, stage, sem):
    slots, slab = stage.shape[0], stage.shape[1]
    jobs = [(src, dst, r) for src, dst in pairs for r in range(0, src.shape[0], slab)]

    def copy(n):
        src, _, r = jobs[n]
        k = n % slots
        return pltpu.make_async_copy(src.at[r:r + slab, :], stage.at[k, :, 0:src.shape[1]], sem.at[k])

    for n in range(min(slots - 1, len(jobs))):
        copy(n).start()
    for n, (src, dst, r) in enumerate(jobs):
        if n + slots - 1 < len(jobs):
            copy(n + slots - 1).start()
        copy(n).wait()
        dst[r:r + slab, :] = stage[n % slots, :, 0:src.shape[1]].astype(BF16)


def _pre_kernel(x_ref, n1_ref, wg_hbm, wu_hbm, wd_hbm, nm_ref, win_hbm, cw_ref, cb_ref, cn_ref,
                *rest, tiles_per_seq, q_scale, n_cast):
    cast_src = rest[:n_cast]
    x1_ref, q_ref, k_ref, v_ref, yc_ref = rest[n_cast:n_cast + 5]
    cast_dst = rest[n_cast + 5:2 * n_cast + 5]
    xc_buf, wg_ref, wu_ref, wd_ref, win_ref, stage, sem = rest[2 * n_cast + 5:]
    tm = x_ref.shape[0]
    i = pl.program_id(0)

    @pl.when(i == 0)
    def _():
        _load_as_bf16([(wg_hbm, wg_ref), (wu_hbm, wu_ref), (wd_hbm, wd_ref), (win_hbm, win_ref)], stage, sem)

    for src, dst in zip(cast_src, cast_dst):
        dst[...] = src[...].astype(BF16)
    first = i % tiles_per_seq == 0

    @pl.when(first)
    def _():
        xc_buf[0:CONV_HALO, :] = jnp.zeros((CONV_HALO, xc_buf.shape[1]), F32)

    @pl.when(jnp.logical_not(first))
    def _():
        xc_buf[0:CONV_HALO, :] = xc_buf[tm:tm + CONV_HALO, :]

    d = D_SB
    subs = [slice(r, r + SUB_TILE) for r in range(0, tm, SUB_TILE)]
    xs = [x_ref[r, :] for r in subs]
    hs = [_rms(x, n1_ref[...]).astype(BF16) for x in xs]
    acts = []
    for h in hs:
        parts = []
        for c in range(0, wg_ref.shape[1], FF_CHUNK):
            gate = jnp.dot(h, wg_ref[:, c:c + FF_CHUNK], preferred_element_type=F32)
            up = jnp.dot(h, wu_ref[:, c:c + FF_CHUNK], preferred_element_type=F32)
            parts.append((gate * jax.nn.sigmoid(gate) * up).astype(BF16))
        acts.append(jnp.concatenate(parts, axis=1))
    hs = []
    for r, x, act in zip(subs, xs, acts):
        x1 = x + 0.5 * jnp.dot(act, wd_ref[...], preferred_element_type=F32)
        x1_ref[r, :] = x1
        hs.append(_rms(x1, nm_ref[...]).astype(BF16))
    gates = []
    for r, h in zip(subs, hs):
        proj = jnp.dot(h, win_ref[...], preferred_element_type=F32)
        q_ref[r, :] = (proj[:, 0:d] * q_scale).astype(BF16)
        k_ref[r, :] = proj[:, d:2 * d].astype(BF16)
        v_ref[r, :] = proj[:, 2 * d:3 * d].astype(BF16)
        gates.append(proj[:, 3 * d:4 * d])
        xc_buf[CONV_HALO + r.start:CONV_HALO + r.stop, :] = proj[:, 4 * d:5 * d] * proj[:, 5 * d:6 * d]
    for r, gate_b in zip(subs, gates):
        y = cb_ref[...]
        for j in range(CONV_WIDTH):
            off = CONV_HALO - (CONV_WIDTH - 1) + j + r.start
            y = y + xc_buf[off:off + SUB_TILE, :] * cw_ref[j:j + 1, :]
        yc_ref[r, :] = _rms(gate_b * y, cn_ref[...]).astype(BF16)


def _mix_post_kernel(q_ref, kc_ref, kp_ref, vc_ref, vp_ref, k_hbm, v_hbm, tri_ref, x1_ref, yc_ref,
                     sbn_ref, wo_ref, n2_ref, wg_ref, wu_ref, wd_ref, fn_ref, out_ref,
                     qs_ref, carry_ref, acc_ref, max_ref, ysn_ref, kbuf, vbuf, sem,
                     *, tiles_per_seq, n_tiles, final):
    tm = q_ref.shape[0]
    tq = ATT_BLOCK
    n_q = tm // tq
    rows = N_PAIRS * 2 * tq
    s = pl.program_id(0)
    slot = s % 2
    tile = jnp.minimum(s, n_tiles - 1)
    tile_in_seq = tile % tiles_per_seq
    first_block = tile_in_seq * n_q
    seq_row0 = (tile - tile_in_seq) * tm

    lane = lax.broadcasted_iota(jnp.int32, (tq, LANES), 1)
    low = lane < SB_HEAD_DIM
    row = lax.broadcasted_iota(jnp.int32, (rows, tq), 0)
    col = lax.broadcasted_iota(jnp.int32, (rows, tq), 1)
    strict = col < (row & (tq - 1))
    tri_col = lax.broadcasted_iota(jnp.int32, tri_ref.shape, 1)
    zero = jnp.zeros((), BF16)
    prev_shift = jnp.where(tile_in_seq == 0, 2 * tq, 0)

    early = tq // 2
    head_rows = [p * 2 * tq + h * tq for p in range(N_PAIRS) for h in range(2)]

    def early_of(x):
        return jnp.concatenate([x[r:r + early] for r in head_rows], axis=0)

    def with_early(full, part, pad=False):
        pieces = []
        for n, r in enumerate(head_rows):
            rest = jnp.zeros((tq - early, part.shape[1]), part.dtype) if pad else full[r + early:r + tq]
            pieces += [part[n * early:(n + 1) * early], rest]
        return jnp.concatenate(pieces, axis=0)

    def block_weights(j, k_blk, v_blk, shift, carry, diagonal, only_early=False, skip_early=None):
        if shift is None:
            keep_k = lo_half = hi_half = tri = None
        else:
            k_lane = lane + shift
            keep_k, lo_half = k_lane < LANES, k_lane < SB_HEAD_DIM
            hi_half = (k_lane >= SB_HEAD_DIM) & keep_k
            tri = jnp.where(tri_col + shift < 2 * tq, tri_ref[...], zero)
        if only_early:
            q_stack = lambda p: jnp.concatenate([qs_ref[j, p, 0:early], qs_ref[j, p, tq:tq + early]], axis=0)
        else:
            q_stack = lambda p: qs_ref[j, p]
        z = jnp.concatenate(
            [lax.dot_general(q_stack(p), k_blk(p) if shift is None else jnp.where(keep_k, k_blk(p), zero),
                             (((1,), (1,)), ((), ())), preferred_element_type=F32)
             for p in range(N_PAIRS)], axis=0)
        sp = jnp.maximum(z, 0.0) + jnp.log(1.0 + jnp.exp(-jnp.abs(z)))
        if diagonal:
            sp = jnp.where(strict, sp, 0.0)
        if skip_early is not None:
            taking_part = (row & (tq - 1)) + jnp.where(skip_early, 0, tq) >= early
            sp = jnp.where(taking_part, sp, 0.0)
        hi = sp.astype(BF16)
        lo = (sp - hi.astype(F32)).astype(BF16)
        sums = jnp.dot(jnp.concatenate([hi, lo], axis=1), tri_ref[...] if shift is None else tri,
                       preferred_element_type=F32)
        logit = z + sums[:, 0:tq]
        if carry is not None:
            logit = logit + carry
        a = jnp.exp(logit)
        if diagonal:
            a = jnp.where(strict, a, 0.0)
        if skip_early is not None:
            a = jnp.where(taking_part, a, 0.0)
        carry = sums[:, tq:2 * tq] if carry is None else carry + sums[:, tq:2 * tq]
        halves = (low, ~low) if shift is None else (lo_half, hi_half)
        v2 = [jnp.concatenate([jnp.where(m, v_blk(p), zero) for m in halves], axis=0)
              for p in range(N_PAIRS)]
        return a.astype(BF16), carry, v2

    def apply_values(j, weights, values, carry, first):
        carry_ref[j] = carry
        heads = []
        for p in range(N_PAIRS):
            r0 = p * 2 * tq
            a2 = jnp.concatenate([w[r:r + tq] for w in weights for r in (r0, r0 + tq)], axis=1)
            v2 = jnp.concatenate([v[p] for v in values], axis=0)
            out = jnp.dot(a2, v2, preferred_element_type=F32)
            heads.append(out if first else acc_ref[j, p] + out)
            acc_ref[j, p] = heads[-1]
        max_ref[j] = jnp.max(carry)
        ysn_ref[slot, j * tq:(j + 1) * tq, :] = _rms(jnp.concatenate(heads, axis=1), sbn_ref[...]).astype(BF16)

    def main_block(dense):
        for j in range(n_q):
            for p in range(N_PAIRS):
                q2 = q_ref[j * tq:(j + 1) * tq, p * LANES:(p + 1) * LANES]
                qs_ref[j, p] = jnp.concatenate([jnp.where(low, q2, zero), jnp.where(low, zero, q2)], axis=0)

        state = {}

        def attention_item(j, b):
            weights, values, carry = state.get(j, ([], [], None))
            if j >= b:
                k_src, v_src, r0, shift = kc_ref, vc_ref, (j - b) * tq, None
            else:
                k_src, v_src, r0, shift = kp_ref, vp_ref, (n_q + j - b) * tq, prev_shift
            k_blk = lambda p: k_src[r0:r0 + tq, p * LANES:(p + 1) * LANES]
            v_blk = lambda p: v_src[r0:r0 + tq, p * LANES:(p + 1) * LANES]
            if b < FIRST_PASS_BLOCKS - 1:
                a, carry, v2 = block_weights(j, k_blk, v_blk, shift, carry, diagonal=(b == 0))
            else:
                a, part, v2 = block_weights(j, k_blk, v_blk, shift, early_of(carry), False, only_early=True)
                a, carry = with_early(None, a, pad=True), with_early(carry, part)
            state[j] = (weights + [a], values + [v2], carry)
            if b == FIRST_PASS_BLOCKS - 1:
                apply_values(j, *state[j], first=True)

        items = [(j, b) for j in range(n_q) for b in range(FIRST_PASS_BLOCKS)]
        if not dense:
            for item in items:
                attention_item(*item)
            return
        d_ff = wg_ref.shape[1]
        chunks = [slice(c, c + FF_CHUNK) for c in range(0, d_ff, FF_CHUNK)]
        subs = [slice(r, r + SUB_TILE) for r in range(0, tm, SUB_TILE)]
        slots = [(t, c) for t in range(len(subs)) for c in range(len(chunks))]
        after_slot = {}
        for n, item in enumerate(items):
            after_slot.setdefault(slots[n * len(slots) // len(items)], []).append(item)

        x2s, hs = [], []
        for r in subs:
            y = jnp.concatenate([ysn_ref[1 - slot, r, :], yc_ref[r, :]], axis=1)
            x2s.append(x1_ref[r, :] + jnp.dot(y, wo_ref[...], preferred_element_type=F32))
            hs.append(_rms(x2s[-1], n2_ref[...]).astype(BF16))
        acts = [[] for _ in subs]
        for t, h in enumerate(hs):
            for ci, c in enumerate(chunks):
                gate = jnp.dot(h, wg_ref[:, c], preferred_element_type=F32)
                up = jnp.dot(h, wu_ref[:, c], preferred_element_type=F32)
                acts[t].append((gate * jax.nn.sigmoid(gate) * up).astype(BF16))
                for item in after_slot.get((t, ci), []):
                    attention_item(*item)
        for r, x2, act in zip(subs, x2s, acts):
            x3 = x2 + 0.5 * jnp.dot(jnp.concatenate(act, axis=1), wd_ref[...], preferred_element_type=F32)
            out_ref[r, :] = _rms(x3, fn_ref[...]) if final else x3

    pl.when(s == 0)(lambda: main_block(False))
    pl.when(s > 0)(lambda: main_block(True))

    for j in range(n_q):
        def cond(state):
            kb, m = state
            return jnp.logical_and(kb >= 0, m > LOG_WEIGHT_UNDERFLOW)

        def body(state, j=j):
            kb, _ = state
            r0 = pl.multiple_of(seq_row0 + kb * tq, tq)
            copies = [pltpu.make_async_copy(src.at[pl.ds(r0, tq), :], dst, sem.at[n])
                      for n, (src, dst) in enumerate(((k_hbm, kbuf), (v_hbm, vbuf)))]
            for cp in copies:
                cp.start()
            for cp in copies:
                cp.wait()
            a, carry, v2 = block_weights(
                j, lambda p: kbuf[:, p * LANES:(p + 1) * LANES], lambda p: vbuf[:, p * LANES:(p + 1) * LANES],
                None, carry_ref[j], diagonal=False, skip_early=(kb == resume))
            apply_values(j, [a], [v2], carry, first=False)
            return kb - 1, max_ref[j]

        resume = first_block + j - (FIRST_PASS_BLOCKS - 1)
        lax.while_loop(cond, body, (resume, max_ref[j]))


def _resident(shape):
    return pl.BlockSpec(shape, lambda *_: (0,) * len(shape), pipeline_mode=pl.Buffered(1))


def _rows(tm, width):
    return pl.BlockSpec((tm, width), lambda i: (i, 0))


def _slab_spec(shape, steps):
    rows, cols = shape
    slab = next(r for r in range(BF16_SUBLANES, rows + 1, BF16_SUBLANES) if rows % r == 0 and r * steps >= rows)
    last = rows // slab - 1
    return pl.BlockSpec((slab, cols), lambda i: (jnp.minimum(i, last), 0))


def _pre_call(x, n1, wg, wu, wd, nm, win, cw, cb, cn, cast, *, seq):
    n, d = x.shape
    tm = PRE_ROW_TILE
    steps = n // tm
    d_ff = wg.shape[1]
    d_conv = cw.shape[1]
    kern = functools.partial(_pre_kernel, tiles_per_seq=seq // tm, q_scale=1.0 / math.sqrt(SB_HEAD_DIM),
                             n_cast=len(cast))
    cast_specs = [_slab_spec(w.shape, steps) for w in cast]
    hbm = pl.BlockSpec(memory_space=pl.ANY)
    own = (wg, wu, wd, win)
    assert all(w.shape[0] % STAGE_ROWS == 0 for w in own)
    stage_cols = max(w.shape[1] for w in own)
    return pl.pallas_call(
        kern,
        grid=(steps,),
        in_specs=[_rows(tm, d), _resident((1, d)), hbm, hbm, hbm, _resident((1, d)), hbm,
                  _resident(cw.shape), _resident((1, d_conv)), _resident((1, d_conv))] + cast_specs,
        out_specs=[_rows(tm, d), _rows(tm, D_SB), _rows(tm, D_SB), _rows(tm, D_SB), _rows(tm, d_conv)]
                  + cast_specs,
        out_shape=[jax.ShapeDtypeStruct((n, d), F32)] + [jax.ShapeDtypeStruct((n, D_SB), BF16)] * 3
                  + [jax.ShapeDtypeStruct((n, d_conv), BF16)]
                  + [jax.ShapeDtypeStruct(w.shape, BF16) for w in cast],
        scratch_shapes=[pltpu.VMEM((tm + 2 * CONV_HALO, d_conv), F32)]
                       + [pltpu.VMEM(w.shape, BF16) for w in own]
                       + [pltpu.VMEM((STAGE_SLOTS, STAGE_ROWS, stage_cols), F32),
                          pltpu.SemaphoreType.DMA((STAGE_SLOTS,))],
        compiler_params=pltpu.CompilerParams(dimension_semantics=("arbitrary",),
                                             vmem_limit_bytes=VMEM_LIMIT),
        name="ffn1_proj_conv",
    )(x, n1, wg, wu, wd, nm, win, cw, cb, cn, *cast)


def _suffix_sum_matrix(tk):
    j = jnp.arange(2 * tk)[:, None] % tk
    s = jnp.arange(2 * tk)[None, :]
    return -jnp.where(s < tk, j >= s, True).astype(BF16)


def _mix_post_call(q, k, v, x1, yc, sbn, wo, n2, wg, wu, wd, fn, *, seq, final):
    n, d = x1.shape
    tm = POST_ROW_TILE
    tq = ATT_BLOCK
    n_tiles = n // tm
    tiles_per_seq = seq // tm
    d_ff = wg.shape[1]
    cur = lambda s: (jnp.minimum(s, n_tiles - 1), 0)
    before_cur = lambda s: (jnp.maximum(jnp.minimum(s, n_tiles - 1) - 1, 0), 0)
    prev = lambda s: (jnp.maximum(s - 1, 0), 0)
    kern = functools.partial(_mix_post_kernel, tiles_per_seq=tiles_per_seq, n_tiles=n_tiles, final=final)
    any_space = pl.BlockSpec(memory_space=pl.ANY)
    return pl.pallas_call(
        kern,
        grid=(n_tiles + 1,),
        in_specs=[pl.BlockSpec((tm, D_SB), cur),
                  pl.BlockSpec((tm, D_SB), cur), pl.BlockSpec((tm, D_SB), before_cur),
                  pl.BlockSpec((tm, D_SB), cur), pl.BlockSpec((tm, D_SB), before_cur),
                  any_space, any_space,
                  _resident((2 * tq, 2 * tq)),
                  pl.BlockSpec((tm, d), prev), pl.BlockSpec((tm, yc.shape[1]), prev),
                  _resident((1, D_SB)), _resident(wo.shape), _resident((1, d)),
                  _resident((d, d_ff)), _resident((d, d_ff)), _resident((d_ff, d)), _resident((1, d))],
        out_specs=pl.BlockSpec((tm, d), prev),
        out_shape=jax.ShapeDtypeStruct((n, d), F32),
        scratch_shapes=[pltpu.VMEM((tm // tq, N_PAIRS, 2 * tq, LANES), BF16),
                        pltpu.VMEM((tm // tq, N_PAIRS * 2 * tq, tq), F32),
                        pltpu.VMEM((tm // tq, N_PAIRS, tq, LANES), F32),
                        pltpu.SMEM((tm // tq,), F32),
                        pltpu.VMEM((2, tm, D_SB), BF16),
                        pltpu.VMEM((tq, D_SB), BF16), pltpu.VMEM((tq, D_SB), BF16),
                        pltpu.SemaphoreType.DMA((2,))],
        compiler_params=pltpu.CompilerParams(dimension_semantics=("arbitrary",),
                                             vmem_limit_bytes=VMEM_LIMIT),
        name="attention_out_proj_ffn2",
    )(q, k, k, v, v, k, v, _suffix_sum_matrix(tq), x1, yc, sbn, wo, n2, wg, wu, wd, fn)


def kernel(x, ffn1_norm, ffn1_w_gate, ffn1_w_up, ffn1_w_down, mix_norm, w_in, conv_w, conv_b,
           sb_out_norm, conv_out_norm, w_out, ffn2_norm, ffn2_w_gate, ffn2_w_up, ffn2_w_down,
           final_norm):
    batch, seq, d = x.shape
    depth = ffn1_norm.shape[0]
    assert seq % POST_ROW_TILE == 0 and seq % PRE_ROW_TILE == 0 and w_in.shape[2] == 6 * D_SB
    row = lambda t: t.reshape(1, -1)
    xs = x.reshape(batch * seq, d)
    for l in range(depth):
        x1, q, k, v, yc, wo, wg2, wu2, wd2 = _pre_call(
            xs, row(ffn1_norm[l]), ffn1_w_gate[l], ffn1_w_up[l], ffn1_w_down[l],
            row(mix_norm[l]), w_in[l], conv_w[l],
            row(conv_b[l]), row(conv_out_norm[l]),
            (w_out[l], ffn2_w_gate[l], ffn2_w_up[l], ffn2_w_down[l]), seq=seq)
        xs = _mix_post_call(
            q, k, v, x1, yc, row(sb_out_norm[l]), wo, row(ffn2_norm[l]), wg2, wu2, wd2,
            row(final_norm), seq=seq, final=(l == depth - 1))
    return xs.reshape(batch, seq, d)
```

```python
import functools
import math

import jax
import jax.numpy as jnp
from jax import lax
from jax.experimental import pallas as pl
from jax.experimental.pallas import tpu as pltpu

F32 = jnp.float32
BF16 = jnp.bfloat16

EPS = 1e-6
SB_HEADS = 8
SB_HEAD_DIM = 64
D_SB = SB_HEADS * SB_HEAD_DIM
CONV_WIDTH = 3
LANES = 128
BF16_SUBLANES = 16
N_PAIRS = D_SB // LANES
PRE_ROW_TILE = 512
POST_ROW_TILE = 512
SUB_TILE = 256
FF_CHUNK = 256
ATT_BLOCK = 128
FIRST_PASS_BLOCKS = 3
STAGE_ROWS = 128
STAGE_SLOTS = 4
CONV_HALO = 8
LOG_WEIGHT_UNDERFLOW = -104.0
V7X_VMEM_BYTES = 64 * 1024 * 1024
VMEM_LIMIT = V7X_VMEM_BYTES - 8 * 1024 * 1024


def _rms(x, g):
    return x * lax.rsqrt(jnp.mean(x * x, axis=-1, keepdims=True) + EPS) * g


def _load_as_bf16(pairs, stage, sem):
    slots, slab = stage.shape[0], stage.shape[1]
    jobs = [(src, dst, r) for src, dst in pairs for r in range(0, src.shape[0], slab)]

    def copy(n):
        src, _, r = jobs[n]
        k = n % slots
        return pltpu.make_async_copy(src.at[r:r + slab, :], stage.at[k, :, 0:src.shape[1]], sem.at[k])

    for n in range(min(slots - 1, len(jobs))):
        copy(n).start()
    for n, (src, dst, r) in enumerate(jobs):
        if n + slots - 1 < len(jobs):
            copy(n + slots - 1).start()
        copy(n).wait()
        dst[r:r + slab, :] = stage[n % slots, :, 0:src.shape[1]].astype(BF16)


def _pre_kernel(x_ref, n1_ref, wg_hbm, wu_hbm, wd_hbm, nm_ref, win_hbm, cw_ref, cb_ref, cn_ref,
                *rest, tiles_per_seq, q_scale, n_cast):
    cast_src = rest[:n_cast]
    x1_ref, q_ref, k_ref, v_ref, yc_ref = rest[n_cast:n_cast + 5]
    cast_dst = rest[n_cast + 5:2 * n_cast + 5]
    xc_buf, wg_ref, wu_ref, wd_ref, win_ref, stage, sem = rest[2 * n_cast + 5:]
    tm = x_ref.shape[0]
    i = pl.program_id(0)

    @pl.when(i == 0)
    def _():
        _load_as_bf16([(wg_hbm, wg_ref), (wu_hbm, wu_ref), (wd_hbm, wd_ref), (win_hbm, win_ref)], stage, sem)

    for src, dst in zip(cast_src, cast_dst):
        dst[...] = src[...].astype(BF16)
    first = i % tiles_per_seq == 0

    @pl.when(first)
    def _():
        xc_buf[0:CONV_HALO, :] = jnp.zeros((CONV_HALO, xc_buf.shape[1]), F32)

    @pl.when(jnp.logical_not(first))
    def _():
        xc_buf[0:CONV_HALO, :] = xc_buf[tm:tm + CONV_HALO, :]

    d = D_SB
    subs = [slice(r, r + SUB_TILE) for r in range(0, tm, SUB_TILE)]
    xs = [x_ref[r, :] for r in subs]
    hs = [_rms(x, n1_ref[...]).astype(BF16) for x in xs]
    acts = []
    for h in hs:
        parts = []
        for c in range(0, wg_ref.shape[1], FF_CHUNK):
            gate = jnp.dot(h, wg_ref[:, c:c + FF_CHUNK], preferred_element_type=F32)
            up = jnp.dot(h, wu_ref[:, c:c + FF_CHUNK], preferred_element_type=F32)
            parts.append((gate * jax.nn.sigmoid(gate) * up).astype(BF16))
        acts.append(jnp.concatenate(parts, axis=1))
    hs = []
    for r, x, act in zip(subs, xs, acts):
        x1 = x + 0.5 * jnp.dot(act, wd_ref[...], preferred_element_type=F32)
        x1_ref[r, :] = x1
        hs.append(_rms(x1, nm_ref[...]).astype(BF16))
    gates = []
    for r, h in zip(subs, hs):
        proj = jnp.dot(h, win_ref[...], preferred_element_type=F32)
        q_ref[r, :] = (proj[:, 0:d] * q_scale).astype(BF16)
        k_ref[r, :] = proj[:, d:2 * d].astype(BF16)
        v_ref[r, :] = proj[:, 2 * d:3 * d].astype(BF16)
        gates.append(proj[:, 3 * d:4 * d])
        xc_buf[CONV_HALO + r.start:CONV_HALO + r.stop, :] = proj[:, 4 * d:5 * d] * proj[:, 5 * d:6 * d]
    for r, gate_b in zip(subs, gates):
        y = cb_ref[...]
        for j in range(CONV_WIDTH):
            off = CONV_HALO - (CONV_WIDTH - 1) + j + r.start
            y = y + xc_buf[off:off + SUB_TILE, :] * cw_ref[j:j + 1, :]
        yc_ref[r, :] = _rms(gate_b * y, cn_ref[...]).astype(BF16)


def _mix_post_kernel(q_ref, kc_ref, kp_ref, vc_ref, vp_ref, k_hbm, v_hbm, tri_ref, x1_ref, yc_ref,
                     sbn_ref, wo_hbm, n2_ref, wg_hbm, wu_hbm, wd_hbm, fn_ref, out_ref,
                     qs_ref, carry_ref, acc_ref, max_ref, ysn_ref, kbuf, vbuf, sem,
                     wo_ref, wg_ref, wu_ref, wd_ref, wsem,
                     *, tiles_per_seq, n_tiles, final):
    tm = q_ref.shape[0]
    tq = ATT_BLOCK
    n_q = tm // tq
    rows = N_PAIRS * 2 * tq
    s = pl.program_id(0)
    slot = s % 2
    tile = jnp.minimum(s, n_tiles - 1)
    tile_in_seq = tile % tiles_per_seq
    first_block = tile_in_seq * n_q
    seq_row0 = (tile - tile_in_seq) * tm

    lane = lax.broadcasted_iota(jnp.int32, (tq, LANES), 1)
    low = lane < SB_HEAD_DIM
    row = lax.broadcasted_iota(jnp.int32, (rows, tq), 0)
    col = lax.broadcasted_iota(jnp.int32, (rows, tq), 1)
    strict = col < (row & (tq - 1))
    tri_col = lax.broadcasted_iota(jnp.int32, tri_ref.shape, 1)
    zero = jnp.zeros((), BF16)
    prev_shift = jnp.where(tile_in_seq == 0, 2 * tq, 0)

    early = tq // 2
    head_rows = [p * 2 * tq + h * tq for p in range(N_PAIRS) for h in range(2)]

    def early_of(x):
        return jnp.concatenate([x[r:r + early] for r in head_rows], axis=0)

    def with_early(full, part, pad=False):
        pieces = []
        for n, r in enumerate(head_rows):
            rest = jnp.zeros((tq - early, part.shape[1]), part.dtype) if pad else full[r + early:r + tq]
            pieces += [part[n * early:(n + 1) * early], rest]
        return jnp.concatenate(pieces, axis=0)

    def block_weights(j, k_blk, v_blk, shift, carry, diagonal, only_early=False, skip_early=None):
        if shift is None:
            keep_k = lo_half = hi_half = tri = None
        else:
            k_lane = lane + shift
            keep_k, lo_half = k_lane < LANES, k_lane < SB_HEAD_DIM
            hi_half = (k_lane >= SB_HEAD_DIM) & keep_k
            tri = jnp.where(tri_col + shift < 2 * tq, tri_ref[...], zero)
        if only_early:
            q_stack = lambda p: jnp.concatenate([qs_ref[j, p, 0:early], qs_ref[j, p, tq:tq + early]], axis=0)
        else:
            q_stack = lambda p: qs_ref[j, p]
        z = jnp.concatenate(
            [lax.dot_general(q_stack(p), k_blk(p) if shift is None else jnp.where(keep_k, k_blk(p), zero),
                             (((1,), (1,)), ((), ())), preferred_element_type=F32)
             for p in range(N_PAIRS)], axis=0)
        sp = jnp.maximum(z, 0.0) + jnp.log(1.0 + jnp.exp(-jnp.abs(z)))
        if diagonal:
            sp = jnp.where(strict, sp, 0.0)
        if skip_early is not None:
            taking_part = (row & (tq - 1)) + jnp.where(skip_early, 0, tq) >= early
            sp = jnp.where(taking_part, sp, 0.0)
        hi = sp.astype(BF16)
        lo = (sp - hi.astype(F32)).astype(BF16)
        sums = jnp.dot(jnp.concatenate([hi, lo], axis=1), tri_ref[...] if shift is None else tri,
                       preferred_element_type=F32)
        logit = z + sums[:, 0:tq]
        if carry is not None:
            logit = logit + carry
        a = jnp.exp(logit)
        if diagonal:
            a = jnp.where(strict, a, 0.0)
        if skip_early is not None:
            a = jnp.where(taking_part, a, 0.0)
        carry = sums[:, tq:2 * tq] if carry is None else carry + sums[:, tq:2 * tq]
        halves = (low, ~low) if shift is None else (lo_half, hi_half)
        v2 = [jnp.concatenate([jnp.where(m, v_blk(p), zero) for m in halves], axis=0)
              for p in range(N_PAIRS)]
        return a.astype(BF16), carry, v2

    def apply_values(j, weights, values, carry, first):
        carry_ref[j] = carry
        heads = []
        for p in range(N_PAIRS):
            r0 = p * 2 * tq
            a2 = jnp.concatenate([w[r:r + tq] for w in weights for r in (r0, r0 + tq)], axis=1)
            v2 = jnp.concatenate([v[p] for v in values], axis=0)
            out = jnp.dot(a2, v2, preferred_element_type=F32)
            heads.append(out if first else acc_ref[j, p] + out)
            acc_ref[j, p] = heads[-1]
        max_ref[j] = jnp.max(carry)
        ysn_ref[slot, j * tq:(j + 1) * tq, :] = _rms(jnp.concatenate(heads, axis=1), sbn_ref[...]).astype(BF16)

    def main_block(dense):
        for j in range(n_q):
            for p in range(N_PAIRS):
                q2 = q_ref[j * tq:(j + 1) * tq, p * LANES:(p + 1) * LANES]
                qs_ref[j, p] = jnp.concatenate([jnp.where(low, q2, zero), jnp.where(low, zero, q2)], axis=0)

        state = {}

        def attention_item(j, b):
            weights, values, carry = state.get(j, ([], [], None))
            if j >= b:
                k_src, v_src, r0, shift = kc_ref, vc_ref, (j - b) * tq, None
            else:
                k_src, v_src, r0, shift = kp_ref, vp_ref, (n_q + j - b) * tq, prev_shift
            k_blk = lambda p: k_src[r0:r0 + tq, p * LANES:(p + 1) * LANES]
            v_blk = lambda p: v_src[r0:r0 + tq, p * LANES:(p + 1) * LANES]
            if b < FIRST_PASS_BLOCKS - 1:
                a, carry, v2 = block_weights(j, k_blk, v_blk, shift, carry, diagonal=(b == 0))
            else:
                a, part, v2 = block_weights(j, k_blk, v_blk, shift, early_of(carry), False, only_early=True)
                a, carry = with_early(None, a, pad=True), with_early(carry, part)
            state[j] = (weights + [a], values + [v2], carry)
            if b == FIRST_PASS_BLOCKS - 1:
                apply_values(j, *state[j], first=True)

        items = [(j, b) for j in range(n_q) for b in range(FIRST_PASS_BLOCKS)]
        if not dense:
            for item in items:
                attention_item(*item)
            return
        d_ff = wg_ref.shape[1]
        chunks = [slice(c, c + FF_CHUNK) for c in range(0, d_ff, FF_CHUNK)]
        subs = [slice(r, r + SUB_TILE) for r in range(0, tm, SUB_TILE)]
        slots = [(t, c) for t in range(len(subs)) for c in range(len(chunks))]
        after_slot = {}
        for n, item in enumerate(items):
            after_slot.setdefault(slots[n * len(slots) // len(items)], []).append(item)

        x2s, hs = [], []
        for r in subs:
            y = jnp.concatenate([ysn_ref[1 - slot, r, :], yc_ref[r, :]], axis=1)
            x2s.append(x1_ref[r, :] + jnp.dot(y, wo_ref[...], preferred_element_type=F32))
            hs.append(_rms(x2s[-1], n2_ref[...]).astype(BF16))
        acts = [[] for _ in subs]
        for t, h in enumerate(hs):
            for ci, c in enumerate(chunks):
                gate = jnp.dot(h, wg_ref[:, c], preferred_element_type=F32)
                up = jnp.dot(h, wu_ref[:, c], preferred_element_type=F32)
                acts[t].append((gate * jax.nn.sigmoid(gate) * up).astype(BF16))
                for item in after_slot.get((t, ci), []):
                    attention_item(*item)
        for r, x2, act in zip(subs, x2s, acts):
            x3 = x2 + 0.5 * jnp.dot(jnp.concatenate(act, axis=1), wd_ref[...], preferred_element_type=F32)
            out_ref[r, :] = _rms(x3, fn_ref[...]) if final else x3

    weight_copies = [pltpu.make_async_copy(src, dst, wsem.at[n]) for n, (src, dst) in enumerate(
        ((wo_hbm, wo_ref), (wg_hbm, wg_ref), (wu_hbm, wu_ref), (wd_hbm, wd_ref)))]

    @pl.when(s == 0)
    def _():
        for cp in weight_copies:
            cp.start()
        main_block(False)
        for cp in weight_copies:
            cp.wait()

    pl.when(s > 0)(lambda: main_block(True))

    for j in range(n_q):
        def cond(state):
            kb, m = state
            return jnp.logical_and(kb >= 0, m > LOG_WEIGHT_UNDERFLOW)

        def body(state, j=j):
            kb, _ = state
            r0 = pl.multiple_of(seq_row0 + kb * tq, tq)
            copies = [pltpu.make_async_copy(src.at[pl.ds(r0, tq), :], dst, sem.at[n])
                      for n, (src, dst) in enumerate(((k_hbm, kbuf), (v_hbm, vbuf)))]
            for cp in copies:
                cp.start()
            for cp in copies:
                cp.wait()
            a, carry, v2 = block_weights(
                j, lambda p: kbuf[:, p * LANES:(p + 1) * LANES], lambda p: vbuf[:, p * LANES:(p + 1) * LANES],
                None, carry_ref[j], diagonal=False, skip_early=(kb == resume))
            apply_values(j, [a], [v2], carry, first=False)
            return kb - 1, max_ref[j]

        resume = first_block + j - (FIRST_PASS_BLOCKS - 1)
        lax.while_loop(cond, body, (resume, max_ref[j]))


def _resident(shape):
    return pl.BlockSpec(shape, lambda *_: (0,) * len(shape), pipeline_mode=pl.Buffered(1))


def _rows(tm, width):
    return pl.BlockSpec((tm, width), lambda i: (i, 0))


def _slab_spec(shape, steps):
    rows, cols = shape
    slab = next(r for r in range(BF16_SUBLANES, rows + 1, BF16_SUBLANES) if rows % r == 0 and r * steps >= rows)
    last = rows // slab - 1
    return pl.BlockSpec((slab, cols), lambda i: (jnp.minimum(i, last), 0))


def _pre_call(x, n1, wg, wu, wd, nm, win, cw, cb, cn, cast, *, seq):
    n, d = x.shape
    tm = PRE_ROW_TILE
    steps = n // tm
    d_ff = wg.shape[1]
    d_conv = cw.shape[1]
    kern = functools.partial(_pre_kernel, tiles_per_seq=seq // tm, q_scale=1.0 / math.sqrt(SB_HEAD_DIM),
                             n_cast=len(cast))
    cast_specs = [_slab_spec(w.shape, steps) for w in cast]
    hbm = pl.BlockSpec(memory_space=pl.ANY)
    own = (wg, wu, wd, win)
    assert all(w.shape[0] % STAGE_ROWS == 0 for w in own)
    stage_cols = max(w.shape[1] for w in own)
    return pl.pallas_call(
        kern,
        grid=(steps,),
        in_specs=[_rows(tm, d), _resident((1, d)), hbm, hbm, hbm, _resident((1, d)), hbm,
                  _resident(cw.shape), _resident((1, d_conv)), _resident((1, d_conv))] + cast_specs,
        out_specs=[_rows(tm, d), _rows(tm, D_SB), _rows(tm, D_SB), _rows(tm, D_SB), _rows(tm, d_conv)]
                  + cast_specs,
        out_shape=[jax.ShapeDtypeStruct((n, d), F32)] + [jax.ShapeDtypeStruct((n, D_SB), BF16)] * 3
                  + [jax.ShapeDtypeStruct((n, d_conv), BF16)]
                  + [jax.ShapeDtypeStruct(w.shape, BF16) for w in cast],
        scratch_shapes=[pltpu.VMEM((tm + 2 * CONV_HALO, d_conv), F32)]
                       + [pltpu.VMEM(w.shape, BF16) for w in own]
                       + [pltpu.VMEM((STAGE_SLOTS, STAGE_ROWS, stage_cols), F32),
                          pltpu.SemaphoreType.DMA((STAGE_SLOTS,))],
        compiler_params=pltpu.CompilerParams(dimension_semantics=("arbitrary",),
                                             vmem_limit_bytes=VMEM_LIMIT),
        name="ffn1_proj_conv",
    )(x, n1, wg, wu, wd, nm, win, cw, cb, cn, *cast)


def _suffix_sum_matrix(tk):
    j = jnp.arange(2 * tk)[:, None] % tk
    s = jnp.arange(2 * tk)[None, :]
    return -jnp.where(s < tk, j >= s, True).astype(BF16)


def _mix_post_call(q, k, v, x1, yc, sbn, wo, n2, wg, wu, wd, fn, *, seq, final):
    n, d = x1.shape
    tm = POST_ROW_TILE
    tq = ATT_BLOCK
    n_tiles = n // tm
    tiles_per_seq = seq // tm
    d_ff = wg.shape[1]
    cur = lambda s: (jnp.minimum(s, n_tiles - 1), 0)
    before_cur = lambda s: (jnp.maximum(jnp.minimum(s, n_tiles - 1) - 1, 0), 0)
    prev = lambda s: (jnp.maximum(s - 1, 0), 0)
    kern = functools.partial(_mix_post_kernel, tiles_per_seq=tiles_per_seq, n_tiles=n_tiles, final=final)
    any_space = pl.BlockSpec(memory_space=pl.ANY)
    return pl.pallas_call(
        kern,
        grid=(n_tiles + 1,),
        in_specs=[pl.BlockSpec((tm, D_SB), cur),
                  pl.BlockSpec((tm, D_SB), cur), pl.BlockSpec((tm, D_SB), before_cur),
                  pl.BlockSpec((tm, D_SB), cur), pl.BlockSpec((tm, D_SB), before_cur),
                  any_space, any_space,
                  _resident((2 * tq, 2 * tq)),
                  pl.BlockSpec((tm, d), prev), pl.BlockSpec((tm, yc.shape[1]), prev),
                  _resident((1, D_SB)), any_space, _resident((1, d)),
                  any_space, any_space, any_space, _resident((1, d))],
        out_specs=pl.BlockSpec((tm, d), prev),
        out_shape=jax.ShapeDtypeStruct((n, d), F32),
        scratch_shapes=[pltpu.VMEM((tm // tq, N_PAIRS, 2 * tq, LANES), BF16),
                        pltpu.VMEM((tm // tq, N_PAIRS * 2 * tq, tq), F32),
                        pltpu.VMEM((tm // tq, N_PAIRS, tq, LANES), F32),
                        pltpu.SMEM((tm // tq,), F32),
                        pltpu.VMEM((2, tm, D_SB), BF16),
                        pltpu.VMEM((tq, D_SB), BF16), pltpu.VMEM((tq, D_SB), BF16),
                        pltpu.SemaphoreType.DMA((2,))]
                       + [pltpu.VMEM(w.shape, BF16) for w in (wo, wg, wu, wd)]
                       + [pltpu.SemaphoreType.DMA((4,))],
        compiler_params=pltpu.CompilerParams(dimension_semantics=("arbitrary",),
                                             vmem_limit_bytes=VMEM_LIMIT),
        name="attention_out_proj_ffn2",
    )(q, k, k, v, v, k, v, _suffix_sum_matrix(tq), x1, yc, sbn, wo, n2, wg, wu, wd, fn)


def kernel(x, ffn1_norm, ffn1_w_gate, ffn1_w_up, ffn1_w_down, mix_norm, w_in, conv_w, conv_b,
           sb_out_norm, conv_out_norm, w_out, ffn2_norm, ffn2_w_gate, ffn2_w_up, ffn2_w_down,
           final_norm):
    batch, seq, d = x.shape
    depth = ffn1_norm.shape[0]
    assert seq % POST_ROW_TILE == 0 and seq % PRE_ROW_TILE == 0 and w_in.shape[2] == 6 * D_SB
    row = lambda t: t.reshape(1, -1)
    xs = x.reshape(batch * seq, d)
    for l in range(depth):
        x1, q, k, v, yc, wo, wg2, wu2, wd2 = _pre_call(
            xs, row(ffn1_norm[l]), ffn1_w_gate[l], ffn1_w_up[l], ffn1_w_down[l],
            row(mix_norm[l]), w_in[l], conv_w[l],
            row(conv_b[l]), row(conv_out_norm[l]),
            (w_out[l], ffn2_w_gate[l], ffn2_w_up[l], ffn2_w_down[l]), seq=seq)
        xs = _mix_post_call(
            q, k, v, x1, yc, row(sb_out_norm[l]), wo, row(ffn2_norm[l]), wg2, wu2, wd2,
            row(final_norm), seq=seq, final=(l == depth - 1))
    return xs.reshape(batch, seq, d)
```

```python
import functools
import math

import jax
import jax.numpy as jnp
from jax import lax
from jax.experimental import pallas as pl
from jax.experimental.pallas import tpu as pltpu

F32 = jnp.float32
BF16 = jnp.bfloat16

EPS = 1e-6
SB_HEADS = 8
SB_HEAD_DIM = 64
D_SB = SB_HEADS * SB_HEAD_DIM
CONV_WIDTH = 3
LANES = 128
BF16_SUBLANES = 16
N_PAIRS = D_SB // LANES
PRE_ROW_TILE = 512
POST_ROW_TILE = 512
SUB_TILE = 256
FF_CHUNK = 256
ATT_BLOCK = 128
FIRST_PASS_BLOCKS = 3
STAGE_ROWS = 128
STAGE_SLOTS = 4
CONV_HALO = 8
LOG_WEIGHT_UNDERFLOW = -104.0
V7X_VMEM_BYTES = 64 * 1024 * 1024
VMEM_LIMIT = V7X_VMEM_BYTES - 8 * 1024 * 1024


def _rms(x, g):
    return x * lax.rsqrt(jnp.mean(x * x, axis=-1, keepdims=True) + EPS) * g


def _load_as_bf16(pairs, stage, sem):
    slots, slab = stage.shape[0], stage.shape[1]
    jobs = [(src, dst, r) for src, dst in pairs for r in range(0, src.shape[0], slab)]

    def copy(n):
        src, _, r = jobs[n]
        k = n % slots
        return pltpu.make_async_copy(src.at[r:r + slab, :], stage.at[k, :, 0:src.shape[1]], sem.at[k])

    for n in range(min(slots - 1, len(jobs))):
        copy(n).start()
    for n, (src, dst, r) in enumerate(jobs):
        if n + slots - 1 < len(jobs):
            copy(n + slots - 1).start()
        copy(n).wait()
        dst[r:r + slab, :] = stage[n % slots, :, 0:src.shape[1]].astype(BF16)


def _pre_kernel(x_ref, n1_ref, wg_hbm, wu_hbm, wd_hbm, nm_ref, win_hbm, cw_ref, cb_ref, cn_ref,
                *rest, tiles_per_seq, q_scale, n_cast):
    cast_src = rest[:n_cast]
    x1_ref, mix_ref = rest[n_cast:n_cast + 2]
    cast_dst = rest[n_cast + 2:2 * n_cast + 2]
    xc_buf, wg_ref, wu_ref, wd_ref, win_ref, stage, sem = rest[2 * n_cast + 2:]
    tm = x_ref.shape[0]
    i = pl.program_id(0)

    @pl.when(i == 0)
    def _():
        _load_as_bf16([(wg_hbm, wg_ref), (wu_hbm, wu_ref), (wd_hbm, wd_ref), (win_hbm, win_ref)], stage, sem)

    for src, dst in zip(cast_src, cast_dst):
        dst[...] = src[...].astype(BF16)
    first = i % tiles_per_seq == 0

    @pl.when(first)
    def _():
        xc_buf[0:CONV_HALO, :] = jnp.zeros((CONV_HALO, xc_buf.shape[1]), F32)

    @pl.when(jnp.logical_not(first))
    def _():
        xc_buf[0:CONV_HALO, :] = xc_buf[tm:tm + CONV_HALO, :]

    d = D_SB
    subs = [slice(r, r + SUB_TILE) for r in range(0, tm, SUB_TILE)]
    xs = [x_ref[r, :] for r in subs]
    hs = [_rms(x, n1_ref[...]).astype(BF16) for x in xs]
    acts = []
    for h in hs:
        parts = []
        for c in range(0, wg_ref.shape[1], FF_CHUNK):
            gate = jnp.dot(h, wg_ref[:, c:c + FF_CHUNK], preferred_element_type=F32)
            up = jnp.dot(h, wu_ref[:, c:c + FF_CHUNK], preferred_element_type=F32)
            parts.append((gate * jax.nn.sigmoid(gate) * up).astype(BF16))
        acts.append(jnp.concatenate(parts, axis=1))
    hs = []
    for r, x, act in zip(subs, xs, acts):
        x1 = x + 0.5 * jnp.dot(act, wd_ref[...], preferred_element_type=F32)
        x1_ref[r, :] = x1
        hs.append(_rms(x1, nm_ref[...]).astype(BF16))
    gates = []
    for r, h in zip(subs, hs):
        proj = jnp.dot(h, win_ref[...], preferred_element_type=F32)
        mix_ref[r, 0:d] = (proj[:, 0:d] * q_scale).astype(BF16)
        mix_ref[r, d:3 * d] = proj[:, d:3 * d].astype(BF16)
        gates.append(proj[:, 3 * d:4 * d])
        xc_buf[CONV_HALO + r.start:CONV_HALO + r.stop, :] = proj[:, 4 * d:5 * d] * proj[:, 5 * d:6 * d]
    for r, gate_b in zip(subs, gates):
        y = cb_ref[...]
        for j in range(CONV_WIDTH):
            off = CONV_HALO - (CONV_WIDTH - 1) + j + r.start
            y = y + xc_buf[off:off + SUB_TILE, :] * cw_ref[j:j + 1, :]
        mix_ref[r, 3 * d:4 * d] = _rms(gate_b * y, cn_ref[...]).astype(BF16)


def _mix_post_kernel(q_ref, kc_ref, kp_ref, vc_ref, vp_ref, mix_hbm, tri_ref, x1_ref, yc_ref,
                     sbn_ref, wo_hbm, n2_ref, wg_hbm, wu_hbm, wd_hbm, fn_ref, out_ref,
                     qs_ref, carry_ref, acc_ref, max_ref, ysn_ref, kbuf, vbuf, sem,
                     wo_ref, wg_ref, wu_ref, wd_ref, wsem,
                     *, tiles_per_seq, n_tiles, final):
    tm = q_ref.shape[0]
    tq = ATT_BLOCK
    n_q = tm // tq
    rows = N_PAIRS * 2 * tq
    s = pl.program_id(0)
    slot = s % 2
    tile = jnp.minimum(s, n_tiles - 1)
    tile_in_seq = tile % tiles_per_seq
    first_block = tile_in_seq * n_q
    seq_row0 = (tile - tile_in_seq) * tm

    lane = lax.broadcasted_iota(jnp.int32, (tq, LANES), 1)
    low = lane < SB_HEAD_DIM
    row = lax.broadcasted_iota(jnp.int32, (rows, tq), 0)
    col = lax.broadcasted_iota(jnp.int32, (rows, tq), 1)
    strict = col < (row & (tq - 1))
    tri_col = lax.broadcasted_iota(jnp.int32, tri_ref.shape, 1)
    zero = jnp.zeros((), BF16)
    prev_shift = jnp.where(tile_in_seq == 0, 2 * tq, 0)

    early = tq // 2
    head_rows = [p * 2 * tq + h * tq for p in range(N_PAIRS) for h in range(2)]

    def early_of(x):
        return jnp.concatenate([x[r:r + early] for r in head_rows], axis=0)

    def with_early(full, part, pad=False):
        pieces = []
        for n, r in enumerate(head_rows):
            rest = jnp.zeros((tq - early, part.shape[1]), part.dtype) if pad else full[r + early:r + tq]
            pieces += [part[n * early:(n + 1) * early], rest]
        return jnp.concatenate(pieces, axis=0)

    def block_weights(j, k_blk, v_blk, shift, carry, diagonal, only_early=False, skip_early=None):
        if shift is None:
            keep_k = lo_half = hi_half = tri = None
        else:
            k_lane = lane + shift
            keep_k, lo_half = k_lane < LANES, k_lane < SB_HEAD_DIM
            hi_half = (k_lane >= SB_HEAD_DIM) & keep_k
            tri = jnp.where(tri_col + shift < 2 * tq, tri_ref[...], zero)
        if only_early:
            q_stack = lambda p: jnp.concatenate([qs_ref[j, p, 0:early], qs_ref[j, p, tq:tq + early]], axis=0)
        else:
            q_stack = lambda p: qs_ref[j, p]
        z = jnp.concatenate(
            [lax.dot_general(q_stack(p), k_blk(p) if shift is None else jnp.where(keep_k, k_blk(p), zero),
                             (((1,), (1,)), ((), ())), preferred_element_type=F32)
             for p in range(N_PAIRS)], axis=0)
        sp = jnp.maximum(z, 0.0) + jnp.log(1.0 + jnp.exp(-jnp.abs(z)))
        if diagonal:
            sp = jnp.where(strict, sp, 0.0)
        if skip_early is not None:
            taking_part = (row & (tq - 1)) + jnp.where(skip_early, 0, tq) >= early
            sp = jnp.where(taking_part, sp, 0.0)
        hi = sp.astype(BF16)
        lo = (sp - hi.astype(F32)).astype(BF16)
        sums = jnp.dot(jnp.concatenate([hi, lo], axis=1), tri_ref[...] if shift is None else tri,
                       preferred_element_type=F32)
        logit = z + sums[:, 0:tq]
        if carry is not None:
            logit = logit + carry
        a = jnp.exp(logit)
        if diagonal:
            a = jnp.where(strict, a, 0.0)
        if skip_early is not None:
            a = jnp.where(taking_part, a, 0.0)
        carry = sums[:, tq:2 * tq] if carry is None else carry + sums[:, tq:2 * tq]
        halves = (low, ~low) if shift is None else (lo_half, hi_half)
        v2 = [jnp.concatenate([jnp.where(m, v_blk(p), zero) for m in halves], axis=0)
              for p in range(N_PAIRS)]
        return a.astype(BF16), carry, v2

    def apply_values(j, weights, values, carry, first):
        carry_ref[j] = carry
        heads = []
        for p in range(N_PAIRS):
            r0 = p * 2 * tq
            a2 = jnp.concatenate([w[r:r + tq] for w in weights for r in (r0, r0 + tq)], axis=1)
            v2 = jnp.concatenate([v[p] for v in values], axis=0)
            out = jnp.dot(a2, v2, preferred_element_type=F32)
            heads.append(out if first else acc_ref[j, p] + out)
            acc_ref[j, p] = heads[-1]
        max_ref[j] = jnp.max(carry)
        ysn_ref[slot, j * tq:(j + 1) * tq, :] = _rms(jnp.concatenate(heads, axis=1), sbn_ref[...]).astype(BF16)

    def main_block(dense):
        for j in range(n_q):
            for p in range(N_PAIRS):
                q2 = q_ref[j * tq:(j + 1) * tq, p * LANES:(p + 1) * LANES]
                qs_ref[j, p] = jnp.concatenate([jnp.where(low, q2, zero), jnp.where(low, zero, q2)], axis=0)

        state = {}

        def attention_item(j, b):
            weights, values, carry = state.get(j, ([], [], None))
            if j >= b:
                k_src, v_src, r0, shift = kc_ref, vc_ref, (j - b) * tq, None
            else:
                k_src, v_src, r0, shift = kp_ref, vp_ref, (n_q + j - b) * tq, prev_shift
            k_blk = lambda p: k_src[r0:r0 + tq, p * LANES:(p + 1) * LANES]
            v_blk = lambda p: v_src[r0:r0 + tq, p * LANES:(p + 1) * LANES]
            if b < FIRST_PASS_BLOCKS - 1:
                a, carry, v2 = block_weights(j, k_blk, v_blk, shift, carry, diagonal=(b == 0))
            else:
                a, part, v2 = block_weights(j, k_blk, v_blk, shift, early_of(carry), False, only_early=True)
                a, carry = with_early(None, a, pad=True), with_early(carry, part)
            state[j] = (weights + [a], values + [v2], carry)
            if b == FIRST_PASS_BLOCKS - 1:
                apply_values(j, *state[j], first=True)

        items = [(j, b) for j in range(n_q) for b in range(FIRST_PASS_BLOCKS)]
        if not dense:
            for item in items:
                attention_item(*item)
            return
        d_ff = wg_ref.shape[1]
        chunks = [slice(c, c + FF_CHUNK) for c in range(0, d_ff, FF_CHUNK)]
        subs = [slice(r, r + SUB_TILE) for r in range(0, tm, SUB_TILE)]
        slots = [(t, c) for t in range(len(subs)) for c in range(len(chunks))]
        after_slot = {}
        for n, item in enumerate(items):
            after_slot.setdefault(slots[n * len(slots) // len(items)], []).append(item)

        x2s, hs = [], []
        for r in subs:
            y = jnp.concatenate([ysn_ref[1 - slot, r, :], yc_ref[r, :]], axis=1)
            x2s.append(x1_ref[r, :] + jnp.dot(y, wo_ref[...], preferred_element_type=F32))
            hs.append(_rms(x2s[-1], n2_ref[...]).astype(BF16))
        acts = [[] for _ in subs]
        for t, h in enumerate(hs):
            for ci, c in enumerate(chunks):
                gate = jnp.dot(h, wg_ref[:, c], preferred_element_type=F32)
                up = jnp.dot(h, wu_ref[:, c], preferred_element_type=F32)
                acts[t].append((gate * jax.nn.sigmoid(gate) * up).astype(BF16))
                for item in after_slot.get((t, ci), []):
                    attention_item(*item)
        for r, x2, act in zip(subs, x2s, acts):
            x3 = x2 + 0.5 * jnp.dot(jnp.concatenate(act, axis=1), wd_ref[...], preferred_element_type=F32)
            out_ref[r, :] = _rms(x3, fn_ref[...]) if final else x3

    weight_copies = [pltpu.make_async_copy(src, dst, wsem.at[n]) for n, (src, dst) in enumerate(
        ((wo_hbm, wo_ref), (wg_hbm, wg_ref), (wu_hbm, wu_ref), (wd_hbm, wd_ref)))]

    @pl.when(s == 0)
    def _():
        for cp in weight_copies:
            cp.start()
        main_block(False)
        for cp in weight_copies:
            cp.wait()

    pl.when(s > 0)(lambda: main_block(True))

    for j in range(n_q):
        def cond(state):
            kb, m = state
            return jnp.logical_and(kb >= 0, m > LOG_WEIGHT_UNDERFLOW)

        def body(state, j=j):
            kb, _ = state
            r0 = pl.multiple_of(seq_row0 + kb * tq, tq)
            copies = [pltpu.make_async_copy(mix_hbm.at[pl.ds(r0, tq), c * D_SB:(c + 1) * D_SB], dst, sem.at[n])
                      for n, (c, dst) in enumerate(((1, kbuf), (2, vbuf)))]
            for cp in copies:
                cp.start()
            for cp in copies:
                cp.wait()
            a, carry, v2 = block_weights(
                j, lambda p: kbuf[:, p * LANES:(p + 1) * LANES], lambda p: vbuf[:, p * LANES:(p + 1) * LANES],
                None, carry_ref[j], diagonal=False, skip_early=(kb == resume))
            apply_values(j, [a], [v2], carry, first=False)
            return kb - 1, max_ref[j]

        resume = first_block + j - (FIRST_PASS_BLOCKS - 1)
        lax.while_loop(cond, body, (resume, max_ref[j]))


def _resident(shape):
    return pl.BlockSpec(shape, lambda *_: (0,) * len(shape), pipeline_mode=pl.Buffered(1))


def _rows(tm, width):
    return pl.BlockSpec((tm, width), lambda i: (i, 0))


def _slab_spec(shape, steps):
    rows, cols = shape
    slab = next(r for r in range(BF16_SUBLANES, rows + 1, BF16_SUBLANES) if rows % r == 0 and r * steps >= rows)
    last = rows // slab - 1
    return pl.BlockSpec((slab, cols), lambda i: (jnp.minimum(i, last), 0))


def _pre_call(x, n1, wg, wu, wd, nm, win, cw, cb, cn, cast, *, seq):
    n, d = x.shape
    tm = PRE_ROW_TILE
    steps = n // tm
    d_ff = wg.shape[1]
    d_conv = cw.shape[1]
    kern = functools.partial(_pre_kernel, tiles_per_seq=seq // tm, q_scale=1.0 / math.sqrt(SB_HEAD_DIM),
                             n_cast=len(cast))
    cast_specs = [_slab_spec(w.shape, steps) for w in cast]
    hbm = pl.BlockSpec(memory_space=pl.ANY)
    own = (wg, wu, wd, win)
    assert all(w.shape[0] % STAGE_ROWS == 0 for w in own)
    stage_cols = max(w.shape[1] for w in own)
    return pl.pallas_call(
        kern,
        grid=(steps,),
        in_specs=[_rows(tm, d), _resident((1, d)), hbm, hbm, hbm, _resident((1, d)), hbm,
                  _resident(cw.shape), _resident((1, d_conv)), _resident((1, d_conv))] + cast_specs,
        out_specs=[_rows(tm, d), _rows(tm, 4 * D_SB)] + cast_specs,
        out_shape=[jax.ShapeDtypeStruct((n, d), F32), jax.ShapeDtypeStruct((n, 4 * D_SB), BF16)]
                  + [jax.ShapeDtypeStruct(w.shape, BF16) for w in cast],
        scratch_shapes=[pltpu.VMEM((tm + 2 * CONV_HALO, d_conv), F32)]
                       + [pltpu.VMEM(w.shape, BF16) for w in own]
                       + [pltpu.VMEM((STAGE_SLOTS, STAGE_ROWS, stage_cols), F32),
                          pltpu.SemaphoreType.DMA((STAGE_SLOTS,))],
        compiler_params=pltpu.CompilerParams(dimension_semantics=("arbitrary",),
                                             vmem_limit_bytes=VMEM_LIMIT),
        name="ffn1_proj_conv",
    )(x, n1, wg, wu, wd, nm, win, cw, cb, cn, *cast)


def _suffix_sum_matrix(tk):
    j = jnp.arange(2 * tk)[:, None] % tk
    s = jnp.arange(2 * tk)[None, :]
    return -jnp.where(s < tk, j >= s, True).astype(BF16)


def _mix_post_call(mix, x1, sbn, wo, n2, wg, wu, wd, fn, *, seq, final):
    n, d = x1.shape
    tm = POST_ROW_TILE
    tq = ATT_BLOCK
    n_tiles = n // tm
    tiles_per_seq = seq // tm
    d_ff = wg.shape[1]
    cur = lambda c: lambda s: (jnp.minimum(s, n_tiles - 1), c)
    before_cur = lambda c: lambda s: (jnp.maximum(jnp.minimum(s, n_tiles - 1) - 1, 0), c)
    prev = lambda c: lambda s: (jnp.maximum(s - 1, 0), c)
    kern = functools.partial(_mix_post_kernel, tiles_per_seq=tiles_per_seq, n_tiles=n_tiles, final=final)
    any_space = pl.BlockSpec(memory_space=pl.ANY)
    return pl.pallas_call(
        kern,
        grid=(n_tiles + 1,),
        in_specs=[pl.BlockSpec((tm, D_SB), cur(0)),
                  pl.BlockSpec((tm, D_SB), cur(1)), pl.BlockSpec((tm, D_SB), before_cur(1)),
                  pl.BlockSpec((tm, D_SB), cur(2)), pl.BlockSpec((tm, D_SB), before_cur(2)),
                  any_space,
                  _resident((2 * tq, 2 * tq)),
                  pl.BlockSpec((tm, d), prev(0)), pl.BlockSpec((tm, D_SB), prev(3)),
                  _resident((1, D_SB)), any_space, _resident((1, d)),
                  any_space, any_space, any_space, _resident((1, d))],
        out_specs=pl.BlockSpec((tm, d), prev(0)),
        out_shape=jax.ShapeDtypeStruct((n, d), F32),
        scratch_shapes=[pltpu.VMEM((tm // tq, N_PAIRS, 2 * tq, LANES), BF16),
                        pltpu.VMEM((tm // tq, N_PAIRS * 2 * tq, tq), F32),
                        pltpu.VMEM((tm // tq, N_PAIRS, tq, LANES), F32),
                        pltpu.SMEM((tm // tq,), F32),
                        pltpu.VMEM((2, tm, D_SB), BF16),
                        pltpu.VMEM((tq, D_SB), BF16), pltpu.VMEM((tq, D_SB), BF16),
                        pltpu.SemaphoreType.DMA((2,))]
                       + [pltpu.VMEM(w.shape, BF16) for w in (wo, wg, wu, wd)]
                       + [pltpu.SemaphoreType.DMA((4,))],
        compiler_params=pltpu.CompilerParams(dimension_semantics=("arbitrary",),
                                             vmem_limit_bytes=VMEM_LIMIT),
        name="attention_out_proj_ffn2",
    )(mix, mix, mix, mix, mix, mix, _suffix_sum_matrix(tq), x1, mix, sbn, wo, n2, wg, wu, wd, fn)


def kernel(x, ffn1_norm, ffn1_w_gate, ffn1_w_up, ffn1_w_down, mix_norm, w_in, conv_w, conv_b,
           sb_out_norm, conv_out_norm, w_out, ffn2_norm, ffn2_w_gate, ffn2_w_up, ffn2_w_down,
           final_norm):
    batch, seq, d = x.shape
    depth = ffn1_norm.shape[0]
    assert seq % POST_ROW_TILE == 0 and seq % PRE_ROW_TILE == 0
    assert w_in.shape[2] == 6 * D_SB and conv_w.shape[2] == D_SB
    row = lambda t: t.reshape(1, -1)
    xs = x.reshape(batch * seq, d)
    for l in range(depth):
        x1, mix, wo, wg2, wu2, wd2 = _pre_call(
            xs, row(ffn1_norm[l]), ffn1_w_gate[l], ffn1_w_up[l], ffn1_w_down[l],
            row(mix_norm[l]), w_in[l], conv_w[l],
            row(conv_b[l]), row(conv_out_norm[l]),
            (w_out[l], ffn2_w_gate[l], ffn2_w_up[l], ffn2_w_down[l]), seq=seq)
        xs = _mix_post_call(
            mix, x1, row(sb_out_norm[l]), wo, row(ffn2_norm[l]), wg2, wu2, wd2,
            row(final_norm), seq=seq, final=(l == depth - 1))
    return xs.reshape(batch, seq, d)
```

```python
import functools
import math

import jax
import jax.numpy as jnp
from jax import lax
from jax.experimental import pallas as pl
from jax.experimental.pallas import tpu as pltpu

F32 = jnp.float32
BF16 = jnp.bfloat16

EPS = 1e-6
SB_HEADS = 8
SB_HEAD_DIM = 64
D_SB = SB_HEADS * SB_HEAD_DIM
CONV_WIDTH = 3
LANES = 128
BF16_SUBLANES = 16
N_PAIRS = D_SB // LANES
PRE_ROW_TILE = 512
POST_ROW_TILE = 512
SUB_TILE = 256
FF_CHUNK = 256
ATT_BLOCK = 128
FIRST_PASS_BLOCKS = 3
STAGE_ROWS = 128
STAGE_SLOTS = 6
CONV_HALO = 8
LOG_WEIGHT_UNDERFLOW = -104.0
V7X_VMEM_BYTES = 64 * 1024 * 1024
VMEM_LIMIT = V7X_VMEM_BYTES - 8 * 1024 * 1024


def _rms(x, g):
    return x * lax.rsqrt(jnp.mean(x * x, axis=-1, keepdims=True) + EPS) * g


def _load_as_bf16(pairs, stage, sem):
    slots, slab = stage.shape[0], stage.shape[1]
    jobs = [(src, dst, r) for src, dst in pairs for r in range(0, src.shape[0], slab)]

    def copy(n):
        src, _, r = jobs[n]
        k = n % slots
        return pltpu.make_async_copy(src.at[r:r + slab, :], stage.at[k, :, 0:src.shape[1]], sem.at[k])

    for n in range(min(slots - 1, len(jobs))):
        copy(n).start()
    for n, (src, dst, r) in enumerate(jobs):
        if n + slots - 1 < len(jobs):
            copy(n + slots - 1).start()
        copy(n).wait()
        dst[r:r + slab, :] = stage[n % slots, :, 0:src.shape[1]].astype(BF16)


def _pre_kernel(x_ref, n1_ref, wg_hbm, wu_hbm, wd_hbm, nm_ref, win_hbm, cw_ref, cb_ref, cn_ref,
                *rest, tiles_per_seq, q_scale, n_cast):
    cast_src = rest[:n_cast]
    x1_ref, q_ref, k_ref, v_ref, yc_ref = rest[n_cast:n_cast + 5]
    cast_dst = rest[n_cast + 5:2 * n_cast + 5]
    xc_buf, wg_ref, wu_ref, wd_ref, win_ref, stage, sem = rest[2 * n_cast + 5:]
    tm = x_ref.shape[0]
    i = pl.program_id(0)

    @pl.when(i == 0)
    def _():
        _load_as_bf16([(wg_hbm, wg_ref), (wu_hbm, wu_ref), (wd_hbm, wd_ref), (win_hbm, win_ref)], stage, sem)

    for src, dst in zip(cast_src, cast_dst):
        dst[...] = src[...].astype(BF16)
    first = i % tiles_per_seq == 0

    @pl.when(first)
    def _():
        xc_buf[0:CONV_HALO, :] = jnp.zeros((CONV_HALO, xc_buf.shape[1]), F32)

    @pl.when(jnp.logical_not(first))
    def _():
        xc_buf[0:CONV_HALO, :] = xc_buf[tm:tm + CONV_HALO, :]

    d = D_SB
    subs = [slice(r, r + SUB_TILE) for r in range(0, tm, SUB_TILE)]
    xs = [x_ref[r, :] for r in subs]
    hs = [_rms(x, n1_ref[...]).astype(BF16) for x in xs]
    acts = []
    for h in hs:
        parts = []
        for c in range(0, wg_ref.shape[1], FF_CHUNK):
            gate = jnp.dot(h, wg_ref[:, c:c + FF_CHUNK], preferred_element_type=F32)
            up = jnp.dot(h, wu_ref[:, c:c + FF_CHUNK], preferred_element_type=F32)
            parts.append((gate * jax.nn.sigmoid(gate) * up).astype(BF16))
        acts.append(jnp.concatenate(parts, axis=1))
    hs = []
    for r, x, act in zip(subs, xs, acts):
        x1 = x + 0.5 * jnp.dot(act, wd_ref[...], preferred_element_type=F32)
        x1_ref[r, :] = x1
        hs.append(_rms(x1, nm_ref[...]).astype(BF16))
    gates = []
    for r, h in zip(subs, hs):
        proj = jnp.dot(h, win_ref[...], preferred_element_type=F32)
        q_ref[r, :] = (proj[:, 0:d] * q_scale).astype(BF16)
        k_ref[r, :] = proj[:, d:2 * d].astype(BF16)
        v_ref[r, :] = proj[:, 2 * d:3 * d].astype(BF16)
        gates.append(proj[:, 3 * d:4 * d])
        xc_buf[CONV_HALO + r.start:CONV_HALO + r.stop, :] = proj[:, 4 * d:5 * d] * proj[:, 5 * d:6 * d]
    for r, gate_b in zip(subs, gates):
        y = cb_ref[...]
        for j in range(CONV_WIDTH):
            off = CONV_HALO - (CONV_WIDTH - 1) + j + r.start
            y = y + xc_buf[off:off + SUB_TILE, :] * cw_ref[j:j + 1, :]
        yc_ref[r, :] = _rms(gate_b * y, cn_ref[...]).astype(BF16)


def _mix_post_kernel(q_ref, kc_ref, kp_ref, vc_ref, vp_ref, k_hbm, v_hbm, tri_ref, x1_ref, yc_ref,
                     sbn_ref, wo_hbm, n2_ref, wg_hbm, wu_hbm, wd_hbm, fn_ref, out_ref,
                     qs_ref, carry_ref, acc_ref, max_ref, ysn_ref, kbuf, vbuf, sem,
                     wo_ref, wg_ref, wu_ref, wd_ref, wsem,
                     *, tiles_per_seq, n_tiles, final):
    tm = q_ref.shape[0]
    tq = ATT_BLOCK
    n_q = tm // tq
    rows = N_PAIRS * 2 * tq
    s = pl.program_id(0)
    slot = s % 2
    tile = jnp.minimum(s, n_tiles - 1)
    tile_in_seq = tile % tiles_per_seq
    first_block = tile_in_seq * n_q
    seq_row0 = (tile - tile_in_seq) * tm

    lane = lax.broadcasted_iota(jnp.int32, (tq, LANES), 1)
    low = lane < SB_HEAD_DIM
    row = lax.broadcasted_iota(jnp.int32, (rows, tq), 0)
    col = lax.broadcasted_iota(jnp.int32, (rows, tq), 1)
    strict = col < (row & (tq - 1))
    tri_col = lax.broadcasted_iota(jnp.int32, tri_ref.shape, 1)
    zero = jnp.zeros((), BF16)
    prev_shift = jnp.where(tile_in_seq == 0, 2 * tq, 0)

    early = tq // 2
    head_rows = [p * 2 * tq + h * tq for p in range(N_PAIRS) for h in range(2)]

    def early_of(x):
        return jnp.concatenate([x[r:r + early] for r in head_rows], axis=0)

    def with_early(full, part, pad=False):
        pieces = []
        for n, r in enumerate(head_rows):
            rest = jnp.zeros((tq - early, part.shape[1]), part.dtype) if pad else full[r + early:r + tq]
            pieces += [part[n * early:(n + 1) * early], rest]
        return jnp.concatenate(pieces, axis=0)

    def block_weights(j, k_blk, v_blk, shift, carry, diagonal, only_early=False, skip_early=None):
        if shift is None:
            keep_k = lo_half = hi_half = tri = None
        else:
            k_lane = lane + shift
            keep_k, lo_half = k_lane < LANES, k_lane < SB_HEAD_DIM
            hi_half = (k_lane >= SB_HEAD_DIM) & keep_k
            tri = jnp.where(tri_col + shift < 2 * tq, tri_ref[...], zero)
        if only_early:
            q_stack = lambda p: jnp.concatenate([qs_ref[j, p, 0:early], qs_ref[j, p, tq:tq + early]], axis=0)
        else:
            q_stack = lambda p: qs_ref[j, p]
        z = jnp.concatenate(
            [lax.dot_general(q_stack(p), k_blk(p) if shift is None else jnp.where(keep_k, k_blk(p), zero),
                             (((1,), (1,)), ((), ())), preferred_element_type=F32)
             for p in range(N_PAIRS)], axis=0)
        sp = jnp.maximum(z, 0.0) + jnp.log(1.0 + jnp.exp(-jnp.abs(z)))
        if diagonal:
            sp = jnp.where(strict, sp, 0.0)
        if skip_early is not None:
            taking_part = (row & (tq - 1)) + jnp.where(skip_early, 0, tq) >= early
            sp = jnp.where(taking_part, sp, 0.0)
        hi = sp.astype(BF16)
        lo = (sp - hi.astype(F32)).astype(BF16)
        sums = jnp.dot(jnp.concatenate([hi, lo], axis=1), tri_ref[...] if shift is None else tri,
                       preferred_element_type=F32)
        logit = z + sums[:, 0:tq]
        if carry is not None:
            logit = logit + carry
        a = jnp.exp(logit)
        if diagonal:
            a = jnp.where(strict, a, 0.0)
        if skip_early is not None:
            a = jnp.where(taking_part, a, 0.0)
        carry = sums[:, tq:2 * tq] if carry is None else carry + sums[:, tq:2 * tq]
        halves = (low, ~low) if shift is None else (lo_half, hi_half)
        v2 = [jnp.concatenate([jnp.where(m, v_blk(p), zero) for m in halves], axis=0)
              for p in range(N_PAIRS)]
        return a.astype(BF16), carry, v2

    def apply_values(j, weights, values, carry, first):
        carry_ref[j] = carry
        heads = []
        for p in range(N_PAIRS):
            r0 = p * 2 * tq
            a2 = jnp.concatenate([w[r:r + tq] for w in weights for r in (r0, r0 + tq)], axis=1)
            v2 = jnp.concatenate([v[p] for v in values], axis=0)
            out = jnp.dot(a2, v2, preferred_element_type=F32)
            heads.append(out if first else acc_ref[j, p] + out)
            acc_ref[j, p] = heads[-1]
        max_ref[j] = jnp.max(carry)
        ysn_ref[slot, j * tq:(j + 1) * tq, :] = _rms(jnp.concatenate(heads, axis=1), sbn_ref[...]).astype(BF16)

    def main_block(dense):
        for j in range(n_q):
            for p in range(N_PAIRS):
                q2 = q_ref[j * tq:(j + 1) * tq, p * LANES:(p + 1) * LANES]
                qs_ref[j, p] = jnp.concatenate([jnp.where(low, q2, zero), jnp.where(low, zero, q2)], axis=0)

        state = {}

        def attention_item(j, b):
            weights, values, carry = state.get(j, ([], [], None))
            if j >= b:
                k_src, v_src, r0, shift = kc_ref, vc_ref, (j - b) * tq, None
            else:
                k_src, v_src, r0, shift = kp_ref, vp_ref, (n_q + j - b) * tq, prev_shift
            k_blk = lambda p: k_src[r0:r0 + tq, p * LANES:(p + 1) * LANES]
            v_blk = lambda p: v_src[r0:r0 + tq, p * LANES:(p + 1) * LANES]
            if b < FIRST_PASS_BLOCKS - 1:
                a, carry, v2 = block_weights(j, k_blk, v_blk, shift, carry, diagonal=(b == 0))
            else:
                a, part, v2 = block_weights(j, k_blk, v_blk, shift, early_of(carry), False, only_early=True)
                a, carry = with_early(None, a, pad=True), with_early(carry, part)
            state[j] = (weights + [a], values + [v2], carry)
            if b == FIRST_PASS_BLOCKS - 1:
                apply_values(j, *state[j], first=True)

        items = [(j, b) for j in range(n_q) for b in range(FIRST_PASS_BLOCKS)]
        if not dense:
            for item in items:
                attention_item(*item)
            return
        d_ff = wg_ref.shape[1]
        chunks = [slice(c, c + FF_CHUNK) for c in range(0, d_ff, FF_CHUNK)]
        subs = [slice(r, r + SUB_TILE) for r in range(0, tm, SUB_TILE)]
        slots = [(t, c) for t in range(len(subs)) for c in range(len(chunks))]
        after_slot = {}
        for n, item in enumerate(items):
            after_slot.setdefault(slots[n * len(slots) // len(items)], []).append(item)

        x2s, hs = [], []
        for r in subs:
            y = jnp.concatenate([ysn_ref[1 - slot, r, :], yc_ref[r, :]], axis=1)
            x2s.append(x1_ref[r, :] + jnp.dot(y, wo_ref[...], preferred_element_type=F32))
            hs.append(_rms(x2s[-1], n2_ref[...]).astype(BF16))
        acts = [[] for _ in subs]
        for t, h in enumerate(hs):
            for ci, c in enumerate(chunks):
                gate = jnp.dot(h, wg_ref[:, c], preferred_element_type=F32)
                up = jnp.dot(h, wu_ref[:, c], preferred_element_type=F32)
                acts[t].append((gate * jax.nn.sigmoid(gate) * up).astype(BF16))
                for item in after_slot.get((t, ci), []):
                    attention_item(*item)
        for r, x2, act in zip(subs, x2s, acts):
            x3 = x2 + 0.5 * jnp.dot(jnp.concatenate(act, axis=1), wd_ref[...], preferred_element_type=F32)
            out_ref[r, :] = _rms(x3, fn_ref[...]) if final else x3

    weight_copies = [pltpu.make_async_copy(src, dst, wsem.at[n]) for n, (src, dst) in enumerate(
        ((wo_hbm, wo_ref), (wg_hbm, wg_ref), (wu_hbm, wu_ref), (wd_hbm, wd_ref)))]

    @pl.when(s == 0)
    def _():
        for cp in weight_copies:
            cp.start()
        main_block(False)
        for cp in weight_copies:
            cp.wait()

    pl.when(s > 0)(lambda: main_block(True))

    for j in range(n_q):
        def cond(state):
            kb, m = state
            return jnp.logical_and(kb >= 0, m > LOG_WEIGHT_UNDERFLOW)

        def body(state, j=j):
            kb, _ = state
            r0 = pl.multiple_of(seq_row0 + kb * tq, tq)
            copies = [pltpu.make_async_copy(src.at[pl.ds(r0, tq), :], dst, sem.at[n])
                      for n, (src, dst) in enumerate(((k_hbm, kbuf), (v_hbm, vbuf)))]
            for cp in copies:
                cp.start()
            for cp in copies:
                cp.wait()
            a, carry, v2 = block_weights(
                j, lambda p: kbuf[:, p * LANES:(p + 1) * LANES], lambda p: vbuf[:, p * LANES:(p + 1) * LANES],
                None, carry_ref[j], diagonal=False, skip_early=(kb == resume))
            apply_values(j, [a], [v2], carry, first=False)
            return kb - 1, max_ref[j]

        resume = first_block + j - (FIRST_PASS_BLOCKS - 1)
        lax.while_loop(cond, body, (resume, max_ref[j]))


def _resident(shape):
    return pl.BlockSpec(shape, lambda *_: (0,) * len(shape), pipeline_mode=pl.Buffered(1))


def _rows(tm, width):
    return pl.BlockSpec((tm, width), lambda i: (i, 0))


def _slab_spec(shape, steps):
    rows, cols = shape
    slab = next(r for r in range(BF16_SUBLANES, rows + 1, BF16_SUBLANES) if rows % r == 0 and r * steps >= rows)
    last = rows // slab - 1
    return pl.BlockSpec((slab, cols), lambda i: (jnp.minimum(i, last), 0))


def _pre_call(x, n1, wg, wu, wd, nm, win, cw, cb, cn, cast, *, seq):
    n, d = x.shape
    tm = PRE_ROW_TILE
    steps = n // tm
    d_ff = wg.shape[1]
    d_conv = cw.shape[1]
    kern = functools.partial(_pre_kernel, tiles_per_seq=seq // tm, q_scale=1.0 / math.sqrt(SB_HEAD_DIM),
                             n_cast=len(cast))
    cast_specs = [_slab_spec(w.shape, steps) for w in cast]
    hbm = pl.BlockSpec(memory_space=pl.ANY)
    own = (wg, wu, wd, win)
    assert all(w.shape[0] % STAGE_ROWS == 0 for w in own)
    stage_cols = max(w.shape[1] for w in own)
    return pl.pallas_call(
        kern,
        grid=(steps,),
        in_specs=[_rows(tm, d), _resident((1, d)), hbm, hbm, hbm, _resident((1, d)), hbm,
                  _resident(cw.shape), _resident((1, d_conv)), _resident((1, d_conv))] + cast_specs,
        out_specs=[_rows(tm, d), _rows(tm, D_SB), _rows(tm, D_SB), _rows(tm, D_SB), _rows(tm, d_conv)]
                  + cast_specs,
        out_shape=[jax.ShapeDtypeStruct((n, d), F32)] + [jax.ShapeDtypeStruct((n, D_SB), BF16)] * 3
                  + [jax.ShapeDtypeStruct((n, d_conv), BF16)]
                  + [jax.ShapeDtypeStruct(w.shape, BF16) for w in cast],
        scratch_shapes=[pltpu.VMEM((tm + 2 * CONV_HALO, d_conv), F32)]
                       + [pltpu.VMEM(w.shape, BF16) for w in own]
                       + [pltpu.VMEM((STAGE_SLOTS, STAGE_ROWS, stage_cols), F32),
                          pltpu.SemaphoreType.DMA((STAGE_SLOTS,))],
        compiler_params=pltpu.CompilerParams(dimension_semantics=("arbitrary",),
                                             vmem_limit_bytes=VMEM_LIMIT),
        name="ffn1_proj_conv",
    )(x, n1, wg, wu, wd, nm, win, cw, cb, cn, *cast)


def _suffix_sum_matrix(tk):
    j = jnp.arange(2 * tk)[:, None] % tk
    s = jnp.arange(2 * tk)[None, :]
    return -jnp.where(s < tk, j >= s, True).astype(BF16)


def _mix_post_call(q, k, v, x1, yc, sbn, wo, n2, wg, wu, wd, fn, *, seq, final):
    n, d = x1.shape
    tm = POST_ROW_TILE
    tq = ATT_BLOCK
    n_tiles = n // tm
    tiles_per_seq = seq // tm
    d_ff = wg.shape[1]
    cur = lambda s: (jnp.minimum(s, n_tiles - 1), 0)
    before_cur = lambda s: (jnp.maximum(jnp.minimum(s, n_tiles - 1) - 1, 0), 0)
    prev = lambda s: (jnp.maximum(s - 1, 0), 0)
    kern = functools.partial(_mix_post_kernel, tiles_per_seq=tiles_per_seq, n_tiles=n_tiles, final=final)
    any_space = pl.BlockSpec(memory_space=pl.ANY)
    return pl.pallas_call(
        kern,
        grid=(n_tiles + 1,),
        in_specs=[pl.BlockSpec((tm, D_SB), cur),
                  pl.BlockSpec((tm, D_SB), cur), pl.BlockSpec((tm, D_SB), before_cur),
                  pl.BlockSpec((tm, D_SB), cur), pl.BlockSpec((tm, D_SB), before_cur),
                  any_space, any_space,
                  _resident((2 * tq, 2 * tq)),
                  pl.BlockSpec((tm, d), prev), pl.BlockSpec((tm, yc.shape[1]), prev),
                  _resident((1, D_SB)), any_space, _resident((1, d)),
                  any_space, any_space, any_space, _resident((1, d))],
        out_specs=pl.BlockSpec((tm, d), prev),
        out_shape=jax.ShapeDtypeStruct((n, d), F32),
        scratch_shapes=[pltpu.VMEM((tm // tq, N_PAIRS, 2 * tq, LANES), BF16),
                        pltpu.VMEM((tm // tq, N_PAIRS * 2 * tq, tq), F32),
                        pltpu.VMEM((tm // tq, N_PAIRS, tq, LANES), F32),
                        pltpu.SMEM((tm // tq,), F32),
                        pltpu.VMEM((2, tm, D_SB), BF16),
                        pltpu.VMEM((tq, D_SB), BF16), pltpu.VMEM((tq, D_SB), BF16),
                        pltpu.SemaphoreType.DMA((2,))]
                       + [pltpu.VMEM(w.shape, BF16) for w in (wo, wg, wu, wd)]
                       + [pltpu.SemaphoreType.DMA((4,))],
        compiler_params=pltpu.CompilerParams(dimension_semantics=("arbitrary",),
                                             vmem_limit_bytes=VMEM_LIMIT),
        name="attention_out_proj_ffn2",
    )(q, k, k, v, v, k, v, _suffix_sum_matrix(tq), x1, yc, sbn, wo, n2, wg, wu, wd, fn)


def kernel(x, ffn1_norm, ffn1_w_gate, ffn1_w_up, ffn1_w_down, mix_norm, w_in, conv_w, conv_b,
           sb_out_norm, conv_out_norm, w_out, ffn2_norm, ffn2_w_gate, ffn2_w_up, ffn2_w_down,
           final_norm):
    batch, seq, d = x.shape
    depth = ffn1_norm.shape[0]
    assert seq % POST_ROW_TILE == 0 and seq % PRE_ROW_TILE == 0 and w_in.shape[2] == 6 * D_SB
    row = lambda t: t.reshape(1, -1)
    xs = x.reshape(batch * seq, d)
    for l in range(depth):
        x1, q, k, v, yc, wo, wg2, wu2, wd2 = _pre_call(
            xs, row(ffn1_norm[l]), ffn1_w_gate[l], ffn1_w_up[l], ffn1_w_down[l],
            row(mix_norm[l]), w_in[l], conv_w[l],
            row(conv_b[l]), row(conv_out_norm[l]),
            (w_out[l], ffn2_w_gate[l], ffn2_w_up[l], ffn2_w_down[l]), seq=seq)
        xs = _mix_post_call(
            q, k, v, x1, yc, row(sb_out_norm[l]), wo, row(ffn2_norm[l]), wg2, wu2, wd2,
            row(final_norm), seq=seq, final=(l == depth - 1))
    return xs.reshape(batch, seq, d)
```

```python
import functools
import math

import jax
import jax.numpy as jnp
from jax import lax
from jax.experimental import pallas as pl
from jax.experimental.pallas import tpu as pltpu

F32 = jnp.float32
BF16 = jnp.bfloat16

EPS = 1e-6
SB_HEADS = 8
SB_HEAD_DIM = 64
D_SB = SB_HEADS * SB_HEAD_DIM
CONV_WIDTH = 3
LANES = 128
BF16_SUBLANES = 16
N_PAIRS = D_SB // LANES
PRE_ROW_TILE = 512
POST_ROW_TILE = 512
SUB_TILE = 256
FF_CHUNK = 256
ATT_BLOCK = 128
FIRST_PASS_BLOCKS = 3
STAGE_ROWS = 128
STAGE_SLOTS = 6
CONV_HALO = 8
LOG_WEIGHT_UNDERFLOW = -104.0
V7X_VMEM_BYTES = 64 * 1024 * 1024
VMEM_LIMIT = V7X_VMEM_BYTES - 8 * 1024 * 1024


def _rms(x, g):
    return x * lax.rsqrt(jnp.mean(x * x, axis=-1, keepdims=True) + EPS) * g


def _load_as_bf16(pairs, stage, sem):
    slots, slab = stage.shape[0], stage.shape[1]
    jobs = [(src, dst, r) for src, dst in pairs for r in range(0, src.shape[0], slab)]

    def copy(n):
        src, _, r = jobs[n]
        k = n % slots
        return pltpu.make_async_copy(src.at[r:r + slab, :], stage.at[k, :, 0:src.shape[1]], sem.at[k])

    for n in range(min(slots - 1, len(jobs))):
        copy(n).start()
    for n, (src, dst, r) in enumerate(jobs):
        if n + slots - 1 < len(jobs):
            copy(n + slots - 1).start()
        copy(n).wait()
        dst[r:r + slab, :] = stage[n % slots, :, 0:src.shape[1]].astype(BF16)


def _pre_kernel(x_ref, n1_ref, wg_hbm, wu_hbm, wd_hbm, nm_ref, win_hbm, cw_ref, cb_ref, cn_ref,
                *rest, tiles_per_seq, q_scale, n_cast):
    cast_src = rest[:n_cast]
    x1_ref, q_ref, k_ref, v_ref, yc_ref = rest[n_cast:n_cast + 5]
    cast_dst = rest[n_cast + 5:2 * n_cast + 5]
    xc_buf, wg_ref, wu_ref, wd_ref, win_ref, stage, sem = rest[2 * n_cast + 5:]
    tm = x_ref.shape[0]
    i = pl.program_id(0)

    @pl.when(i == 0)
    def _():
        _load_as_bf16([(wg_hbm, wg_ref), (wu_hbm, wu_ref), (wd_hbm, wd_ref), (win_hbm, win_ref)], stage, sem)

    for src, dst in zip(cast_src, cast_dst):
        dst[...] = src[...].astype(BF16)
    first = i % tiles_per_seq == 0

    @pl.when(first)
    def _():
        xc_buf[0:CONV_HALO, :] = jnp.zeros((CONV_HALO, xc_buf.shape[1]), F32)

    @pl.when(jnp.logical_not(first))
    def _():
        xc_buf[0:CONV_HALO, :] = xc_buf[tm:tm + CONV_HALO, :]

    d = D_SB
    subs = [slice(r, r + SUB_TILE) for r in range(0, tm, SUB_TILE)]
    xs = [x_ref[r, :] for r in subs]
    hs = [_rms(x, n1_ref[...]).astype(BF16) for x in xs]
    acts = []
    for h in hs:
        parts = []
        for c in range(0, wg_ref.shape[1], FF_CHUNK):
            gate = jnp.dot(h, wg_ref[:, c:c + FF_CHUNK], preferred_element_type=F32)
            up = jnp.dot(h, wu_ref[:, c:c + FF_CHUNK], preferred_element_type=F32)
            parts.append((gate * jax.nn.sigmoid(gate) * up).astype(BF16))
        acts.append(jnp.concatenate(parts, axis=1))
    hs = []
    for r, x, act in zip(subs, xs, acts):
        x1 = x + 0.5 * jnp.dot(act, wd_ref[...], preferred_element_type=F32)
        x1_ref[r, :] = x1
        hs.append(_rms(x1, nm_ref[...]).astype(BF16))
    for r, h in zip(subs, hs):
        cu = jnp.dot(h, win_ref[:, 4 * d:6 * d], preferred_element_type=F32)
        xc_buf[CONV_HALO + r.start:CONV_HALO + r.stop, :] = cu[:, 0:d] * cu[:, d:2 * d]
    convs = []
    for r in subs:
        y = cb_ref[...]
        for j in range(CONV_WIDTH):
            off = CONV_HALO - (CONV_WIDTH - 1) + j + r.start
            y = y + xc_buf[off:off + SUB_TILE, :] * cw_ref[j:j + 1, :]
        convs.append(y)
    for r, h, y in zip(subs, hs, convs):
        proj = jnp.dot(h, win_ref[:, 0:4 * d], preferred_element_type=F32)
        q_ref[r, :] = (proj[:, 0:d] * q_scale).astype(BF16)
        k_ref[r, :] = proj[:, d:2 * d].astype(BF16)
        v_ref[r, :] = proj[:, 2 * d:3 * d].astype(BF16)
        yc_ref[r, :] = _rms(proj[:, 3 * d:4 * d] * y, cn_ref[...]).astype(BF16)


def _mix_post_kernel(q_ref, kc_ref, kp_ref, vc_ref, vp_ref, k_hbm, v_hbm, tri_ref, x1_ref, yc_ref,
                     sbn_ref, wo_hbm, n2_ref, wg_hbm, wu_hbm, wd_hbm, fn_ref, out_ref,
                     qs_ref, carry_ref, acc_ref, max_ref, ysn_ref, kbuf, vbuf, sem,
                     wo_ref, wg_ref, wu_ref, wd_ref, wsem,
                     *, tiles_per_seq, n_tiles, final):
    tm = q_ref.shape[0]
    tq = ATT_BLOCK
    n_q = tm // tq
    rows = N_PAIRS * 2 * tq
    s = pl.program_id(0)
    slot = s % 2
    tile = jnp.minimum(s, n_tiles - 1)
    tile_in_seq = tile % tiles_per_seq
    first_block = tile_in_seq * n_q
    seq_row0 = (tile - tile_in_seq) * tm

    lane = lax.broadcasted_iota(jnp.int32, (tq, LANES), 1)
    low = lane < SB_HEAD_DIM
    row = lax.broadcasted_iota(jnp.int32, (rows, tq), 0)
    col = lax.broadcasted_iota(jnp.int32, (rows, tq), 1)
    strict = col < (row & (tq - 1))
    tri_col = lax.broadcasted_iota(jnp.int32, tri_ref.shape, 1)
    zero = jnp.zeros((), BF16)
    prev_shift = jnp.where(tile_in_seq == 0, 2 * tq, 0)

    early = tq // 2
    head_rows = [p * 2 * tq + h * tq for p in range(N_PAIRS) for h in range(2)]

    def early_of(x):
        return jnp.concatenate([x[r:r + early] for r in head_rows], axis=0)

    def with_early(full, part, pad=False):
        pieces = []
        for n, r in enumerate(head_rows):
            rest = jnp.zeros((tq - early, part.shape[1]), part.dtype) if pad else full[r + early:r + tq]
            pieces += [part[n * early:(n + 1) * early], rest]
        return jnp.concatenate(pieces, axis=0)

    def block_weights(j, k_blk, v_blk, shift, carry, diagonal, only_early=False, skip_early=None):
        if shift is None:
            keep_k = lo_half = hi_half = tri = None
        else:
            k_lane = lane + shift
            keep_k, lo_half = k_lane < LANES, k_lane < SB_HEAD_DIM
            hi_half = (k_lane >= SB_HEAD_DIM) & keep_k
            tri = jnp.where(tri_col + shift < 2 * tq, tri_ref[...], zero)
        if only_early:
            q_stack = lambda p: jnp.concatenate([qs_ref[j, p, 0:early], qs_ref[j, p, tq:tq + early]], axis=0)
        else:
            q_stack = lambda p: qs_ref[j, p]
        z = jnp.concatenate(
            [lax.dot_general(q_stack(p), k_blk(p) if shift is None else jnp.where(keep_k, k_blk(p), zero),
                             (((1,), (1,)), ((), ())), preferred_element_type=F32)
             for p in range(N_PAIRS)], axis=0)
        sp = jnp.maximum(z, 0.0) + jnp.log(1.0 + jnp.exp(-jnp.abs(z)))
        if diagonal:
            sp = jnp.where(strict, sp, 0.0)
        if skip_early is not None:
            taking_part = (row & (tq - 1)) + jnp.where(skip_early, 0, tq) >= early
            sp = jnp.where(taking_part, sp, 0.0)
        hi = sp.astype(BF16)
        lo = (sp - hi.astype(F32)).astype(BF16)
        sums = jnp.dot(jnp.concatenate([hi, lo], axis=1), tri_ref[...] if shift is None else tri,
                       preferred_element_type=F32)
        logit = z + sums[:, 0:tq]
        if carry is not None:
            logit = logit + carry
        a = jnp.exp(logit)
        if diagonal:
            a = jnp.where(strict, a, 0.0)
        if skip_early is not None:
            a = jnp.where(taking_part, a, 0.0)
        carry = sums[:, tq:2 * tq] if carry is None else carry + sums[:, tq:2 * tq]
        halves = (low, ~low) if shift is None else (lo_half, hi_half)
        v2 = [jnp.concatenate([jnp.where(m, v_blk(p), zero) for m in halves], axis=0)
              for p in range(N_PAIRS)]
        return a.astype(BF16), carry, v2

    def apply_values(j, weights, values, carry, first):
        carry_ref[j] = carry
        heads = []
        for p in range(N_PAIRS):
            r0 = p * 2 * tq
            a2 = jnp.concatenate([w[r:r + tq] for w in weights for r in (r0, r0 + tq)], axis=1)
            v2 = jnp.concatenate([v[p] for v in values], axis=0)
            out = jnp.dot(a2, v2, preferred_element_type=F32)
            heads.append(out if first else acc_ref[j, p] + out)
            acc_ref[j, p] = heads[-1]
        max_ref[j] = jnp.max(carry)
        ysn_ref[slot, j * tq:(j + 1) * tq, :] = _rms(jnp.concatenate(heads, axis=1), sbn_ref[...]).astype(BF16)

    def main_block(dense):
        for j in range(n_q):
            for p in range(N_PAIRS):
                q2 = q_ref[j * tq:(j + 1) * tq, p * LANES:(p + 1) * LANES]
                qs_ref[j, p] = jnp.concatenate([jnp.where(low, q2, zero), jnp.where(low, zero, q2)], axis=0)

        state = {}

        def attention_item(j, b):
            weights, values, carry = state.get(j, ([], [], None))
            if j >= b:
                k_src, v_src, r0, shift = kc_ref, vc_ref, (j - b) * tq, None
            else:
                k_src, v_src, r0, shift = kp_ref, vp_ref, (n_q + j - b) * tq, prev_shift
            k_blk = lambda p: k_src[r0:r0 + tq, p * LANES:(p + 1) * LANES]
            v_blk = lambda p: v_src[r0:r0 + tq, p * LANES:(p + 1) * LANES]
            if b < FIRST_PASS_BLOCKS - 1:
                a, carry, v2 = block_weights(j, k_blk, v_blk, shift, carry, diagonal=(b == 0))
            else:
                a, part, v2 = block_weights(j, k_blk, v_blk, shift, early_of(carry), False, only_early=True)
                a, carry = with_early(None, a, pad=True), with_early(carry, part)
            state[j] = (weights + [a], values + [v2], carry)
            if b == FIRST_PASS_BLOCKS - 1:
                apply_values(j, *state[j], first=True)

        items = [(j, b) for j in range(n_q) for b in range(FIRST_PASS_BLOCKS)]
        if not dense:
            for item in items:
                attention_item(*item)
            return
        d_ff = wg_ref.shape[1]
        chunks = [slice(c, c + FF_CHUNK) for c in range(0, d_ff, FF_CHUNK)]
        subs = [slice(r, r + SUB_TILE) for r in range(0, tm, SUB_TILE)]
        slots = [(t, c) for t in range(len(subs)) for c in range(len(chunks))]
        after_slot = {}
        for n, item in enumerate(items):
            after_slot.setdefault(slots[n * len(slots) // len(items)], []).append(item)

        x2s, hs = [], []
        for r in subs:
            y = jnp.concatenate([ysn_ref[1 - slot, r, :], yc_ref[r, :]], axis=1)
            x2s.append(x1_ref[r, :] + jnp.dot(y, wo_ref[...], preferred_element_type=F32))
            hs.append(_rms(x2s[-1], n2_ref[...]).astype(BF16))
        acts = [[] for _ in subs]
        for t, h in enumerate(hs):
            for ci, c in enumerate(chunks):
                gate = jnp.dot(h, wg_ref[:, c], preferred_element_type=F32)
                up = jnp.dot(h, wu_ref[:, c], preferred_element_type=F32)
                acts[t].append((gate * jax.nn.sigmoid(gate) * up).astype(BF16))
                for item in after_slot.get((t, ci), []):
                    attention_item(*item)
        for r, x2, act in zip(subs, x2s, acts):
            x3 = x2 + 0.5 * jnp.dot(jnp.concatenate(act, axis=1), wd_ref[...], preferred_element_type=F32)
            out_ref[r, :] = _rms(x3, fn_ref[...]) if final else x3

    weight_copies = [pltpu.make_async_copy(src, dst, wsem.at[n]) for n, (src, dst) in enumerate(
        ((wo_hbm, wo_ref), (wg_hbm, wg_ref), (wu_hbm, wu_ref), (wd_hbm, wd_ref)))]

    @pl.when(s == 0)
    def _():
        for cp in weight_copies:
            cp.start()
        main_block(False)
        for cp in weight_copies:
            cp.wait()

    pl.when(s > 0)(lambda: main_block(True))

    for j in range(n_q):
        def cond(state):
            kb, m = state
            return jnp.logical_and(kb >= 0, m > LOG_WEIGHT_UNDERFLOW)

        def body(state, j=j):
            kb, _ = state
            r0 = pl.multiple_of(seq_row0 + kb * tq, tq)
            copies = [pltpu.make_async_copy(src.at[pl.ds(r0, tq), :], dst, sem.at[n])
                      for n, (src, dst) in enumerate(((k_hbm, kbuf), (v_hbm, vbuf)))]
            for cp in copies:
                cp.start()
            for cp in copies:
                cp.wait()
            a, carry, v2 = block_weights(
                j, lambda p: kbuf[:, p * LANES:(p + 1) * LANES], lambda p: vbuf[:, p * LANES:(p + 1) * LANES],
                None, carry_ref[j], diagonal=False, skip_early=(kb == resume))
            apply_values(j, [a], [v2], carry, first=False)
            return kb - 1, max_ref[j]

        resume = first_block + j - (FIRST_PASS_BLOCKS - 1)
        lax.while_loop(cond, body, (resume, max_ref[j]))


def _resident(shape):
    return pl.BlockSpec(shape, lambda *_: (0,) * len(shape), pipeline_mode=pl.Buffered(1))


def _rows(tm, width):
    return pl.BlockSpec((tm, width), lambda i: (i, 0))


def _slab_spec(shape, steps):
    rows, cols = shape
    slab = next(r for r in range(BF16_SUBLANES, rows + 1, BF16_SUBLANES) if rows % r == 0 and r * steps >= rows)
    last = rows // slab - 1
    return pl.BlockSpec((slab, cols), lambda i: (jnp.minimum(i, last), 0))


def _pre_call(x, n1, wg, wu, wd, nm, win, cw, cb, cn, cast, *, seq):
    n, d = x.shape
    tm = PRE_ROW_TILE
    steps = n // tm
    d_ff = wg.shape[1]
    d_conv = cw.shape[1]
    kern = functools.partial(_pre_kernel, tiles_per_seq=seq // tm, q_scale=1.0 / math.sqrt(SB_HEAD_DIM),
                             n_cast=len(cast))
    cast_specs = [_slab_spec(w.shape, steps) for w in cast]
    hbm = pl.BlockSpec(memory_space=pl.ANY)
    own = (wg, wu, wd, win)
    assert all(w.shape[0] % STAGE_ROWS == 0 for w in own)
    stage_cols = max(w.shape[1] for w in own)
    return pl.pallas_call(
        kern,
        grid=(steps,),
        in_specs=[_rows(tm, d), _resident((1, d)), hbm, hbm, hbm, _resident((1, d)), hbm,
                  _resident(cw.shape), _resident((1, d_conv)), _resident((1, d_conv))] + cast_specs,
        out_specs=[_rows(tm, d), _rows(tm, D_SB), _rows(tm, D_SB), _rows(tm, D_SB), _rows(tm, d_conv)]
                  + cast_specs,
        out_shape=[jax.ShapeDtypeStruct((n, d), F32)] + [jax.ShapeDtypeStruct((n, D_SB), BF16)] * 3
                  + [jax.ShapeDtypeStruct((n, d_conv), BF16)]
                  + [jax.ShapeDtypeStruct(w.shape, BF16) for w in cast],
        scratch_shapes=[pltpu.VMEM((tm + 2 * CONV_HALO, d_conv), F32)]
                       + [pltpu.VMEM(w.shape, BF16) for w in own]
                       + [pltpu.VMEM((STAGE_SLOTS, STAGE_ROWS, stage_cols), F32),
                          pltpu.SemaphoreType.DMA((STAGE_SLOTS,))],
        compiler_params=pltpu.CompilerParams(dimension_semantics=("arbitrary",),
                                             vmem_limit_bytes=VMEM_LIMIT),
        name="ffn1_proj_conv",
    )(x, n1, wg, wu, wd, nm, win, cw, cb, cn, *cast)


def _suffix_sum_matrix(tk):
    j = jnp.arange(2 * tk)[:, None] % tk
    s = jnp.arange(2 * tk)[None, :]
    return -jnp.where(s < tk, j >= s, True).astype(BF16)


def _mix_post_call(q, k, v, x1, yc, sbn, wo, n2, wg, wu, wd, fn, *, seq, final):
    n, d = x1.shape
    tm = POST_ROW_TILE
    tq = ATT_BLOCK
    n_tiles = n // tm
    tiles_per_seq = seq // tm
    d_ff = wg.shape[1]
    cur = lambda s: (jnp.minimum(s, n_tiles - 1), 0)
    before_cur = lambda s: (jnp.maximum(jnp.minimum(s, n_tiles - 1) - 1, 0), 0)
    prev = lambda s: (jnp.maximum(s - 1, 0), 0)
    kern = functools.partial(_mix_post_kernel, tiles_per_seq=tiles_per_seq, n_tiles=n_tiles, final=final)
    any_space = pl.BlockSpec(memory_space=pl.ANY)
    return pl.pallas_call(
        kern,
        grid=(n_tiles + 1,),
        in_specs=[pl.BlockSpec((tm, D_SB), cur),
                  pl.BlockSpec((tm, D_SB), cur), pl.BlockSpec((tm, D_SB), before_cur),
                  pl.BlockSpec((tm, D_SB), cur), pl.BlockSpec((tm, D_SB), before_cur),
                  any_space, any_space,
                  _resident((2 * tq, 2 * tq)),
                  pl.BlockSpec((tm, d), prev), pl.BlockSpec((tm, yc.shape[1]), prev),
                  _resident((1, D_SB)), any_space, _resident((1, d)),
                  any_space, any_space, any_space, _resident((1, d))],
        out_specs=pl.BlockSpec((tm, d), prev),
        out_shape=jax.ShapeDtypeStruct((n, d), F32),
        scratch_shapes=[pltpu.VMEM((tm // tq, N_PAIRS, 2 * tq, LANES), BF16),
                        pltpu.VMEM((tm // tq, N_PAIRS * 2 * tq, tq), F32),
                        pltpu.VMEM((tm // tq, N_PAIRS, tq, LANES), F32),
                        pltpu.SMEM((tm // tq,), F32),
                        pltpu.VMEM((2, tm, D_SB), BF16),
                        pltpu.VMEM((tq, D_SB), BF16), pltpu.VMEM((tq, D_SB), BF16),
                        pltpu.SemaphoreType.DMA((2,))]
                       + [pltpu.VMEM(w.shape, BF16) for w in (wo, wg, wu, wd)]
                       + [pltpu.SemaphoreType.DMA((4,))],
        compiler_params=pltpu.CompilerParams(dimension_semantics=("arbitrary",),
                                             vmem_limit_bytes=VMEM_LIMIT),
        name="attention_out_proj_ffn2",
    )(q, k, k, v, v, k, v, _suffix_sum_matrix(tq), x1, yc, sbn, wo, n2, wg, wu, wd, fn)


def kernel(x, ffn1_norm, ffn1_w_gate, ffn1_w_up, ffn1_w_down, mix_norm, w_in, conv_w, conv_b,
           sb_out_norm, conv_out_norm, w_out, ffn2_norm, ffn2_w_gate, ffn2_w_up, ffn2_w_down,
           final_norm):
    batch, seq, d = x.shape
    depth = ffn1_norm.shape[0]
    assert seq % POST_ROW_TILE == 0 and seq % PRE_ROW_TILE == 0 and w_in.shape[2] == 6 * D_SB
    row = lambda t: t.reshape(1, -1)
    xs = x.reshape(batch * seq, d)
    for l in range(depth):
        x1, q, k, v, yc, wo, wg2, wu2, wd2 = _pre_call(
            xs, row(ffn1_norm[l]), ffn1_w_gate[l], ffn1_w_up[l], ffn1_w_down[l],
            row(mix_norm[l]), w_in[l], conv_w[l],
            row(conv_b[l]), row(conv_out_norm[l]),
            (w_out[l], ffn2_w_gate[l], ffn2_w_up[l], ffn2_w_down[l]), seq=seq)
        xs = _mix_post_call(
            q, k, v, x1, yc, row(sb_out_norm[l]), wo, row(ffn2_norm[l]), wg2, wu2, wd2,
            row(final_norm), seq=seq, final=(l == depth - 1))
    return xs.reshape(batch, seq, d)
```

```python
import functools
import math

import jax
import jax.numpy as jnp
from jax import lax
from jax.experimental import pallas as pl
from jax.experimental.pallas import tpu as pltpu

F32 = jnp.float32
BF16 = jnp.bfloat16

EPS = 1e-6
SB_HEADS = 8
SB_HEAD_DIM = 64
D_SB = SB_HEADS * SB_HEAD_DIM
CONV_WIDTH = 3
LANES = 128
BF16_SUBLANES = 16
N_PAIRS = D_SB // LANES
PRE_ROW_TILE = 512
POST_ROW_TILE = 512
SUB_TILE = 256
FF_CHUNK = 256
ATT_BLOCK = 128
FIRST_PASS_BLOCKS = 3
STAGE_ROWS = 128
STAGE_SLOTS = 6
CONV_HALO = 8
LOG_WEIGHT_UNDERFLOW = -104.0
V7X_VMEM_BYTES = 64 * 1024 * 1024
VMEM_LIMIT = V7X_VMEM_BYTES - 8 * 1024 * 1024


def _rms(x, g):
    return x * lax.rsqrt(jnp.mean(x * x, axis=-1, keepdims=True) + EPS) * g


def _load_as_bf16(pairs, stage, sem):
    slots, slab = stage.shape[0], stage.shape[1]
    jobs = [(src, dst, r) for src, dst in pairs for r in range(0, src.shape[0], slab)]

    def copy(n):
        src, _, r = jobs[n]
        k = n % slots
        return pltpu.make_async_copy(src.at[r:r + slab, :], stage.at[k, :, 0:src.shape[1]], sem.at[k])

    for n in range(min(slots - 1, len(jobs))):
        copy(n).start()
    for n, (src, dst, r) in enumerate(jobs):
        if n + slots - 1 < len(jobs):
            copy(n + slots - 1).start()
        copy(n).wait()
        dst[r:r + slab, :] = stage[n % slots, :, 0:src.shape[1]].astype(BF16)


def _pre_kernel(x_ref, n1_ref, wg_hbm, wu_hbm, wd_hbm, nm_ref, win_hbm, cw_ref, cb_ref, cn_ref,
                *rest, tiles_per_seq, q_scale, n_cast):
    cast_src = rest[:n_cast]
    x1_ref, q_ref, k_ref, v_ref, yc_ref = rest[n_cast:n_cast + 5]
    cast_dst = rest[n_cast + 5:2 * n_cast + 5]
    xc_buf, wg_ref, wu_ref, wd_ref, win_ref, stage, sem = rest[2 * n_cast + 5:]
    tm = x_ref.shape[0]
    i = pl.program_id(0)

    @pl.when(i == 0)
    def _():
        _load_as_bf16([(wg_hbm, wg_ref), (wu_hbm, wu_ref), (wd_hbm, wd_ref), (win_hbm, win_ref)], stage, sem)

    first = i % tiles_per_seq == 0

    @pl.when(first)
    def _():
        xc_buf[0:CONV_HALO, :] = jnp.zeros((CONV_HALO, xc_buf.shape[1]), F32)

    @pl.when(jnp.logical_not(first))
    def _():
        xc_buf[0:CONV_HALO, :] = xc_buf[tm:tm + CONV_HALO, :]

    d = D_SB
    subs = [slice(r, r + SUB_TILE) for r in range(0, tm, SUB_TILE)]
    xs = [x_ref[r, :] for r in subs]
    hs = [_rms(x, n1_ref[...]).astype(BF16) for x in xs]
    acts = []
    for h in hs:
        parts = []
        for c in range(0, wg_ref.shape[1], FF_CHUNK):
            gate = jnp.dot(h, wg_ref[:, c:c + FF_CHUNK], preferred_element_type=F32)
            up = jnp.dot(h, wu_ref[:, c:c + FF_CHUNK], preferred_element_type=F32)
            parts.append((gate * jax.nn.sigmoid(gate) * up).astype(BF16))
        acts.append(jnp.concatenate(parts, axis=1))
    for src, dst in zip(cast_src, cast_dst):
        dst[...] = src[...].astype(BF16)
    hs = []
    for r, x, act in zip(subs, xs, acts):
        x1 = x + 0.5 * jnp.dot(act, wd_ref[...], preferred_element_type=F32)
        x1_ref[r, :] = x1
        hs.append(_rms(x1, nm_ref[...]).astype(BF16))
    for r, h in zip(subs, hs):
        cu = jnp.dot(h, win_ref[:, 4 * d:6 * d], preferred_element_type=F32)
        xc_buf[CONV_HALO + r.start:CONV_HALO + r.stop, :] = cu[:, 0:d] * cu[:, d:2 * d]
    convs = []
    for r in subs:
        y = cb_ref[...]
        for j in range(CONV_WIDTH):
            off = CONV_HALO - (CONV_WIDTH - 1) + j + r.start
            y = y + xc_buf[off:off + SUB_TILE, :] * cw_ref[j:j + 1, :]
        convs.append(y)
    for r, h, y in zip(subs, hs, convs):
        gate_b = jnp.dot(h, win_ref[:, 3 * d:4 * d], preferred_element_type=F32)
        yc_ref[r, :] = _rms(gate_b * y, cn_ref[...]).astype(BF16)
        proj = jnp.dot(h, win_ref[:, 0:3 * d], preferred_element_type=F32)
        q_ref[r, :] = (proj[:, 0:d] * q_scale).astype(BF16)
        k_ref[r, :] = proj[:, d:2 * d].astype(BF16)
        v_ref[r, :] = proj[:, 2 * d:3 * d].astype(BF16)


def _mix_post_kernel(q_ref, kc_ref, kp_ref, vc_ref, vp_ref, k_hbm, v_hbm, tri_ref, x1_ref, yc_ref,
                     sbn_ref, wo_hbm, n2_ref, wg_hbm, wu_hbm, wd_hbm, fn_ref, out_ref,
                     qs_ref, carry_ref, acc_ref, max_ref, ysn_ref, kbuf, vbuf, sem,
                     wo_ref, wg_ref, wu_ref, wd_ref, wsem,
                     *, tiles_per_seq, n_tiles, final):
    tm = q_ref.shape[0]
    tq = ATT_BLOCK
    n_q = tm // tq
    rows = N_PAIRS * 2 * tq
    s = pl.program_id(0)
    slot = s % 2
    tile = jnp.minimum(s, n_tiles - 1)
    tile_in_seq = tile % tiles_per_seq
    first_block = tile_in_seq * n_q
    seq_row0 = (tile - tile_in_seq) * tm

    lane = lax.broadcasted_iota(jnp.int32, (tq, LANES), 1)
    low = lane < SB_HEAD_DIM
    row = lax.broadcasted_iota(jnp.int32, (rows, tq), 0)
    col = lax.broadcasted_iota(jnp.int32, (rows, tq), 1)
    strict = col < (row & (tq - 1))
    tri_col = lax.broadcasted_iota(jnp.int32, tri_ref.shape, 1)
    zero = jnp.zeros((), BF16)
    prev_shift = jnp.where(tile_in_seq == 0, 2 * tq, 0)

    early = tq // 2
    head_rows = [p * 2 * tq + h * tq for p in range(N_PAIRS) for h in range(2)]

    def early_of(x):
        return jnp.concatenate([x[r:r + early] for r in head_rows], axis=0)

    def with_early(full, part, pad=False):
        pieces = []
        for n, r in enumerate(head_rows):
            rest = jnp.zeros((tq - early, part.shape[1]), part.dtype) if pad else full[r + early:r + tq]
            pieces += [part[n * early:(n + 1) * early], rest]
        return jnp.concatenate(pieces, axis=0)

    def block_weights(j, k_blk, v_blk, shift, carry, diagonal, only_early=False, skip_early=None):
        if shift is None:
            keep_k = lo_half = hi_half = tri = None
        else:
            k_lane = lane + shift
            keep_k, lo_half = k_lane < LANES, k_lane < SB_HEAD_DIM
            hi_half = (k_lane >= SB_HEAD_DIM) & keep_k
            tri = jnp.where(tri_col + shift < 2 * tq, tri_ref[...], zero)
        if only_early:
            q_stack = lambda p: jnp.concatenate([qs_ref[j, p, 0:early], qs_ref[j, p, tq:tq + early]], axis=0)
        else:
            q_stack = lambda p: qs_ref[j, p]
        z = jnp.concatenate(
            [lax.dot_general(q_stack(p), k_blk(p) if shift is None else jnp.where(keep_k, k_blk(p), zero),
                             (((1,), (1,)), ((), ())), preferred_element_type=F32)
             for p in range(N_PAIRS)], axis=0)
        sp = jnp.maximum(z, 0.0) + jnp.log(1.0 + jnp.exp(-jnp.abs(z)))
        if diagonal:
            sp = jnp.where(strict, sp, 0.0)
        if skip_early is not None:
            taking_part = (row & (tq - 1)) + jnp.where(skip_early, 0, tq) >= early
            sp = jnp.where(taking_part, sp, 0.0)
        hi = sp.astype(BF16)
        lo = (sp - hi.astype(F32)).astype(BF16)
        sums = jnp.dot(jnp.concatenate([hi, lo], axis=1), tri_ref[...] if shift is None else tri,
                       preferred_element_type=F32)
        logit = z + sums[:, 0:tq]
        if carry is not None:
            logit = logit + carry
        a = jnp.exp(logit)
        if diagonal:
            a = jnp.where(strict, a, 0.0)
        if skip_early is not None:
            a = jnp.where(taking_part, a, 0.0)
        carry = sums[:, tq:2 * tq] if carry is None else carry + sums[:, tq:2 * tq]
        halves = (low, ~low) if shift is None else (lo_half, hi_half)
        v2 = [jnp.concatenate([jnp.where(m, v_blk(p), zero) for m in halves], axis=0)
              for p in range(N_PAIRS)]
        return a.astype(BF16), carry, v2

    def apply_values(j, weights, values, carry, first):
        carry_ref[j] = carry
        heads = []
        for p in range(N_PAIRS):
            r0 = p * 2 * tq
            a2 = jnp.concatenate([w[r:r + tq] for w in weights for r in (r0, r0 + tq)], axis=1)
            v2 = jnp.concatenate([v[p] for v in values], axis=0)
            out = jnp.dot(a2, v2, preferred_element_type=F32)
            heads.append(out if first else acc_ref[j, p] + out)
            acc_ref[j, p] = heads[-1]
        max_ref[j] = jnp.max(carry)
        ysn_ref[slot, j * tq:(j + 1) * tq, :] = _rms(jnp.concatenate(heads, axis=1), sbn_ref[...]).astype(BF16)

    def main_block(dense):
        for j in range(n_q):
            for p in range(N_PAIRS):
                q2 = q_ref[j * tq:(j + 1) * tq, p * LANES:(p + 1) * LANES]
                qs_ref[j, p] = jnp.concatenate([jnp.where(low, q2, zero), jnp.where(low, zero, q2)], axis=0)

        state = {}

        def attention_item(j, b):
            weights, values, carry = state.get(j, ([], [], None))
            if j >= b:
                k_src, v_src, r0, shift = kc_ref, vc_ref, (j - b) * tq, None
            else:
                k_src, v_src, r0, shift = kp_ref, vp_ref, (n_q + j - b) * tq, prev_shift
            k_blk = lambda p: k_src[r0:r0 + tq, p * LANES:(p + 1) * LANES]
            v_blk = lambda p: v_src[r0:r0 + tq, p * LANES:(p + 1) * LANES]
            if b < FIRST_PASS_BLOCKS - 1:
                a, carry, v2 = block_weights(j, k_blk, v_blk, shift, carry, diagonal=(b == 0))
            else:
                a, part, v2 = block_weights(j, k_blk, v_blk, shift, early_of(carry), False, only_early=True)
                a, carry = with_early(None, a, pad=True), with_early(carry, part)
            state[j] = (weights + [a], values + [v2], carry)
            if b == FIRST_PASS_BLOCKS - 1:
                apply_values(j, *state[j], first=True)

        items = [(j, b) for j in range(n_q) for b in range(FIRST_PASS_BLOCKS)]
        if not dense:
            for item in items:
                attention_item(*item)
            return
        d_ff = wg_ref.shape[1]
        chunks = [slice(c, c + FF_CHUNK) for c in range(0, d_ff, FF_CHUNK)]
        subs = [slice(r, r + SUB_TILE) for r in range(0, tm, SUB_TILE)]
        slots = [(t, c) for t in range(len(subs)) for c in range(len(chunks))]
        after_slot = {}
        for n, item in enumerate(items):
            after_slot.setdefault(slots[n * len(slots) // len(items)], []).append(item)

        x2s, hs = [], []
        for r in subs:
            y = jnp.concatenate([ysn_ref[1 - slot, r, :], yc_ref[r, :]], axis=1)
            x2s.append(x1_ref[r, :] + jnp.dot(y, wo_ref[...], preferred_element_type=F32))
            hs.append(_rms(x2s[-1], n2_ref[...]).astype(BF16))
        acts = [[] for _ in subs]
        for t, h in enumerate(hs):
            for ci, c in enumerate(chunks):
                gate = jnp.dot(h, wg_ref[:, c], preferred_element_type=F32)
                up = jnp.dot(h, wu_ref[:, c], preferred_element_type=F32)
                acts[t].append((gate * jax.nn.sigmoid(gate) * up).astype(BF16))
                for item in after_slot.get((t, ci), []):
                    attention_item(*item)
        for r, x2, act in zip(subs, x2s, acts):
            x3 = x2 + 0.5 * jnp.dot(jnp.concatenate(act, axis=1), wd_ref[...], preferred_element_type=F32)
            out_ref[r, :] = _rms(x3, fn_ref[...]) if final else x3

    weight_copies = [pltpu.make_async_copy(src, dst, wsem.at[n]) for n, (src, dst) in enumerate(
        ((wo_hbm, wo_ref), (wg_hbm, wg_ref), (wu_hbm, wu_ref), (wd_hbm, wd_ref)))]

    @pl.when(s == 0)
    def _():
        for cp in weight_copies:
            cp.start()
        main_block(False)
        for cp in weight_copies:
            cp.wait()

    pl.when(s > 0)(lambda: main_block(True))

    for j in range(n_q):
        def cond(state):
            kb, m = state
            return jnp.logical_and(kb >= 0, m > LOG_WEIGHT_UNDERFLOW)

        def body(state, j=j):
            kb, _ = state
            r0 = pl.multiple_of(seq_row0 + kb * tq, tq)
            copies = [pltpu.make_async_copy(src.at[pl.ds(r0, tq), :], dst, sem.at[n])
                      for n, (src, dst) in enumerate(((k_hbm, kbuf), (v_hbm, vbuf)))]
            for cp in copies:
                cp.start()
            for cp in copies:
                cp.wait()
            a, carry, v2 = block_weights(
                j, lambda p: kbuf[:, p * LANES:(p + 1) * LANES], lambda p: vbuf[:, p * LANES:(p + 1) * LANES],
                None, carry_ref[j], diagonal=False, skip_early=(kb == resume))
            apply_values(j, [a], [v2], carry, first=False)
            return kb - 1, max_ref[j]

        resume = first_block + j - (FIRST_PASS_BLOCKS - 1)
        lax.while_loop(cond, body, (resume, max_ref[j]))


def _resident(shape):
    return pl.BlockSpec(shape, lambda *_: (0,) * len(shape), pipeline_mode=pl.Buffered(1))


def _rows(tm, width):
    return pl.BlockSpec((tm, width), lambda i: (i, 0))


def _slab_spec(shape, steps):
    rows, cols = shape
    slab = next(r for r in range(BF16_SUBLANES, rows + 1, BF16_SUBLANES) if rows % r == 0 and r * steps >= rows)
    last = rows // slab - 1
    return pl.BlockSpec((slab, cols), lambda i: (jnp.minimum(i, last), 0))


def _pre_call(x, n1, wg, wu, wd, nm, win, cw, cb, cn, cast, *, seq):
    n, d = x.shape
    tm = PRE_ROW_TILE
    steps = n // tm
    d_ff = wg.shape[1]
    d_conv = cw.shape[1]
    kern = functools.partial(_pre_kernel, tiles_per_seq=seq // tm, q_scale=1.0 / math.sqrt(SB_HEAD_DIM),
                             n_cast=len(cast))
    cast_specs = [_slab_spec(w.shape, steps) for w in cast]
    hbm = pl.BlockSpec(memory_space=pl.ANY)
    own = (wg, wu, wd, win)
    assert all(w.shape[0] % STAGE_ROWS == 0 for w in own)
    stage_cols = max(w.shape[1] for w in own)
    return pl.pallas_call(
        kern,
        grid=(steps,),
        in_specs=[_rows(tm, d), _resident((1, d)), hbm, hbm, hbm, _resident((1, d)), hbm,
                  _resident(cw.shape), _resident((1, d_conv)), _resident((1, d_conv))] + cast_specs,
        out_specs=[_rows(tm, d), _rows(tm, D_SB), _rows(tm, D_SB), _rows(tm, D_SB), _rows(tm, d_conv)]
                  + cast_specs,
        out_shape=[jax.ShapeDtypeStruct((n, d), F32)] + [jax.ShapeDtypeStruct((n, D_SB), BF16)] * 3
                  + [jax.ShapeDtypeStruct((n, d_conv), BF16)]
                  + [jax.ShapeDtypeStruct(w.shape, BF16) for w in cast],
        scratch_shapes=[pltpu.VMEM((tm + 2 * CONV_HALO, d_conv), F32)]
                       + [pltpu.VMEM(w.shape, BF16) for w in own]
                       + [pltpu.VMEM((STAGE_SLOTS, STAGE_ROWS, stage_cols), F32),
                          pltpu.SemaphoreType.DMA((STAGE_SLOTS,))],
        compiler_params=pltpu.CompilerParams(dimension_semantics=("arbitrary",),
                                             vmem_limit_bytes=VMEM_LIMIT),
        name="ffn1_proj_conv",
    )(x, n1, wg, wu, wd, nm, win, cw, cb, cn, *cast)


def _suffix_sum_matrix(tk):
    j = jnp.arange(2 * tk)[:, None] % tk
    s = jnp.arange(2 * tk)[None, :]
    return -jnp.where(s < tk, j >= s, True).astype(BF16)


def _mix_post_call(q, k, v, x1, yc, sbn, wo, n2, wg, wu, wd, fn, *, seq, final):
    n, d = x1.shape
    tm = POST_ROW_TILE
    tq = ATT_BLOCK
    n_tiles = n // tm
    tiles_per_seq = seq // tm
    d_ff = wg.shape[1]
    cur = lambda s: (jnp.minimum(s, n_tiles - 1), 0)
    before_cur = lambda s: (jnp.maximum(jnp.minimum(s, n_tiles - 1) - 1, 0), 0)
    prev = lambda s: (jnp.maximum(s - 1, 0), 0)
    kern = functools.partial(_mix_post_kernel, tiles_per_seq=tiles_per_seq, n_tiles=n_tiles, final=final)
    any_space = pl.BlockSpec(memory_space=pl.ANY)
    return pl.pallas_call(
        kern,
        grid=(n_tiles + 1,),
        in_specs=[pl.BlockSpec((tm, D_SB), cur),
                  pl.BlockSpec((tm, D_SB), cur), pl.BlockSpec((tm, D_SB), before_cur),
                  pl.BlockSpec((tm, D_SB), cur), pl.BlockSpec((tm, D_SB), before_cur),
                  any_space, any_space,
                  _resident((2 * tq, 2 * tq)),
                  pl.BlockSpec((tm, d), prev), pl.BlockSpec((tm, yc.shape[1]), prev),
                  _resident((1, D_SB)), any_space, _resident((1, d)),
                  any_space, any_space, any_space, _resident((1, d))],
        out_specs=pl.BlockSpec((tm, d), prev),
        out_shape=jax.ShapeDtypeStruct((n, d), F32),
        scratch_shapes=[pltpu.VMEM((tm // tq, N_PAIRS, 2 * tq, LANES), BF16),
                        pltpu.VMEM((tm // tq, N_PAIRS * 2 * tq, tq), F32),
                        pltpu.VMEM((tm // tq, N_PAIRS, tq, LANES), F32),
                        pltpu.SMEM((tm // tq,), F32),
                        pltpu.VMEM((2, tm, D_SB), BF16),
                        pltpu.VMEM((tq, D_SB), BF16), pltpu.VMEM((tq, D_SB), BF16),
                        pltpu.SemaphoreType.DMA((2,))]
                       + [pltpu.VMEM(w.shape, BF16) for w in (wo, wg, wu, wd)]
                       + [pltpu.SemaphoreType.DMA((4,))],
        compiler_params=pltpu.CompilerParams(dimension_semantics=("arbitrary",),
                                             vmem_limit_bytes=VMEM_LIMIT),
        name="attention_out_proj_ffn2",
    )(q, k, k, v, v, k, v, _suffix_sum_matrix(tq), x1, yc, sbn, wo, n2, wg, wu, wd, fn)


def kernel(x, ffn1_norm, ffn1_w_gate, ffn1_w_up, ffn1_w_down, mix_norm, w_in, conv_w, conv_b,
           sb_out_norm, conv_out_norm, w_out, ffn2_norm, ffn2_w_gate, ffn2_w_up, ffn2_w_down,
           final_norm):
    batch, seq, d = x.shape
    depth = ffn1_norm.shape[0]
    assert seq % POST_ROW_TILE == 0 and seq % PRE_ROW_TILE == 0 and w_in.shape[2] == 6 * D_SB
    row = lambda t: t.reshape(1, -1)
    xs = x.reshape(batch * seq, d)
    for l in range(depth):
        x1, q, k, v, yc, wo, wg2, wu2, wd2 = _pre_call(
            xs, row(ffn1_norm[l]), ffn1_w_gate[l], ffn1_w_up[l], ffn1_w_down[l],
            row(mix_norm[l]), w_in[l], conv_w[l],
            row(conv_b[l]), row(conv_out_norm[l]),
            (w_out[l], ffn2_w_gate[l], ffn2_w_up[l], ffn2_w_down[l]), seq=seq)
        xs = _mix_post_call(
            q, k, v, x1, yc, row(sb_out_norm[l]), wo, row(ffn2_norm[l]), wg2, wu2, wd2,
            row(final_norm), seq=seq, final=(l == depth - 1))
    return xs.reshape(batch, seq, d)
```

```python
import functools
import math

import jax
import jax.numpy as jnp
from jax import lax
from jax.experimental import pallas as pl
from jax.experimental.pallas import tpu as pltpu

F32 = jnp.float32
BF16 = jnp.bfloat16

EPS = 1e-6
SB_HEADS = 8
SB_HEAD_DIM = 64
D_SB = SB_HEADS * SB_HEAD_DIM
CONV_WIDTH = 3
LANES = 128
BF16_SUBLANES = 16
N_PAIRS = D_SB // LANES
PRE_ROW_TILE = 512
POST_ROW_TILE = 512
SUB_TILE = 256
FF_CHUNK = 256
ATT_BLOCK = 128
FIRST_PASS_BLOCKS = 3
STAGE_ROWS = 128
STAGE_SLOTS = 6
CONV_HALO = 8
LOG_WEIGHT_UNDERFLOW = -104.0
V7X_VMEM_BYTES = 64 * 1024 * 1024
VMEM_LIMIT = V7X_VMEM_BYTES - 8 * 1024 * 1024


def _rms(x, g):
    return x * lax.rsqrt(jnp.mean(x * x, axis=-1, keepdims=True) + EPS) * g


def _load_as_bf16(pairs, stage, sem):
    slots, slab = stage.shape[0], stage.shape[1]
    jobs = [(src, dst, r) for src, dst in pairs for r in range(0, src.shape[0], slab)]

    def copy(n):
        src, _, r = jobs[n]
        k = n % slots
        return pltpu.make_async_copy(src.at[r:r + slab, :], stage.at[k, :, 0:src.shape[1]], sem.at[k])

    for n in range(min(slots - 1, len(jobs))):
        copy(n).start()
    for n, (src, dst, r) in enumerate(jobs):
        if n + slots - 1 < len(jobs):
            copy(n + slots - 1).start()
        copy(n).wait()
        dst[r:r + slab, :] = stage[n % slots, :, 0:src.shape[1]].astype(BF16)


def _pre_kernel(x_ref, n1_ref, wg_hbm, wu_hbm, wd_hbm, nm_ref, win_hbm, cw_ref, cb_ref, cn_ref,
                *rest, tiles_per_seq, q_scale, n_cast):
    cast_src = rest[:n_cast]
    x1_ref, q_ref, k_ref, v_ref, yc_ref = rest[n_cast:n_cast + 5]
    cast_dst = rest[n_cast + 5:2 * n_cast + 5]
    xc_buf, wg_ref, wu_ref, wd_ref, win_ref, stage, sem = rest[2 * n_cast + 5:]
    tm = x_ref.shape[0]
    i = pl.program_id(0)

    @pl.when(i == 0)
    def _():
        _load_as_bf16([(wg_hbm, wg_ref), (wu_hbm, wu_ref), (wd_hbm, wd_ref), (win_hbm, win_ref)], stage, sem)

    first = i % tiles_per_seq == 0

    @pl.when(first)
    def _():
        xc_buf[0:CONV_HALO, :] = jnp.zeros((CONV_HALO, xc_buf.shape[1]), F32)

    @pl.when(jnp.logical_not(first))
    def _():
        xc_buf[0:CONV_HALO, :] = xc_buf[tm:tm + CONV_HALO, :]

    d = D_SB
    subs = [slice(r, r + SUB_TILE) for r in range(0, tm, SUB_TILE)]
    xs = [x_ref[r, :] for r in subs]
    hs = [_rms(x, n1_ref[...]).astype(BF16) for x in xs]
    acts = []
    for h in hs:
        parts = []
        for c in range(0, wg_ref.shape[1], FF_CHUNK):
            gate = jnp.dot(h, wg_ref[:, c:c + FF_CHUNK], preferred_element_type=F32)
            up = jnp.dot(h, wu_ref[:, c:c + FF_CHUNK], preferred_element_type=F32)
            parts.append((gate * jax.nn.sigmoid(gate) * up).astype(BF16))
        acts.append(jnp.concatenate(parts, axis=1))
    for src, dst in zip(cast_src, cast_dst):
        dst[...] = src[...].astype(BF16)
    hs = []
    for r, x, act in zip(subs, xs, acts):
        x1 = x + 0.5 * jnp.dot(act, wd_ref[...], preferred_element_type=F32)
        x1_ref[r, :] = x1
        hs.append(_rms(x1, nm_ref[...]).astype(BF16))
    for r, h in zip(subs, hs):
        cu = jnp.dot(h, win_ref[:, 4 * d:6 * d], preferred_element_type=F32)
        xc_buf[CONV_HALO + r.start:CONV_HALO + r.stop, :] = cu[:, 0:d] * cu[:, d:2 * d]
    convs = []
    for r in subs:
        y = cb_ref[...]
        for j in range(CONV_WIDTH):
            off = CONV_HALO - (CONV_WIDTH - 1) + j + r.start
            y = y + xc_buf[off:off + SUB_TILE, :] * cw_ref[j:j + 1, :]
        convs.append(y)
    for r, h, y in zip(subs, hs, convs):
        gate_b = jnp.dot(h, win_ref[:, 3 * d:4 * d], preferred_element_type=F32)
        yc_ref[r, :] = _rms(gate_b * y, cn_ref[...]).astype(BF16)
        proj = jnp.dot(h, win_ref[:, 0:3 * d], preferred_element_type=F32)
        q_ref[r, :] = (proj[:, 0:d] * q_scale).astype(BF16)
        k_ref[r, :] = proj[:, d:2 * d].astype(BF16)
        v_ref[r, :] = proj[:, 2 * d:3 * d].astype(BF16)


def _mix_post_kernel(q_ref, kc_ref, kp_ref, vc_ref, vp_ref, k_hbm, v_hbm, tri_ref, x1_ref, yc_ref,
                     sbn_ref, wo_hbm, n2_ref, wg_hbm, wu_hbm, wd_hbm, fn_ref, out_ref,
                     qs_ref, carry_ref, acc_ref, max_ref, ysn_ref, kbuf, vbuf, sem,
                     wo_ref, wg_ref, wu_ref, wd_ref, wsem,
                     *, tiles_per_seq, n_tiles, final):
    tm = q_ref.shape[0]
    tq = ATT_BLOCK
    n_q = tm // tq
    rows = N_PAIRS * 2 * tq
    s = pl.program_id(0)
    slot = s % 2
    tile = jnp.minimum(s, n_tiles - 1)
    tile_in_seq = tile % tiles_per_seq
    first_block = tile_in_seq * n_q
    seq_row0 = (tile - tile_in_seq) * tm

    lane = lax.broadcasted_iota(jnp.int32, (tq, LANES), 1)
    low = lane < SB_HEAD_DIM
    row = lax.broadcasted_iota(jnp.int32, (rows, tq), 0)
    col = lax.broadcasted_iota(jnp.int32, (rows, tq), 1)
    strict = col < (row & (tq - 1))
    tri_col = lax.broadcasted_iota(jnp.int32, tri_ref.shape, 1)
    zero = jnp.zeros((), BF16)
    prev_shift = jnp.where(tile_in_seq == 0, 2 * tq, 0)

    early = tq // 2
    head_rows = [p * 2 * tq + h * tq for p in range(N_PAIRS) for h in range(2)]

    def early_of(x):
        return jnp.concatenate([x[r:r + early] for r in head_rows], axis=0)

    def with_early(full, part, pad=False):
        pieces = []
        for n, r in enumerate(head_rows):
            rest = jnp.zeros((tq - early, part.shape[1]), part.dtype) if pad else full[r + early:r + tq]
            pieces += [part[n * early:(n + 1) * early], rest]
        return jnp.concatenate(pieces, axis=0)

    def block_weights(j, k_blk, v_blk, shift, carry, diagonal, only_early=False, skip_early=None):
        if shift is None:
            keep_k = lo_half = hi_half = tri = None
        else:
            k_lane = lane + shift
            keep_k, lo_half = k_lane < LANES, k_lane < SB_HEAD_DIM
            hi_half = (k_lane >= SB_HEAD_DIM) & keep_k
            tri = jnp.where(tri_col + shift < 2 * tq, tri_ref[...], zero)
        if only_early:
            q_stack = lambda p: jnp.concatenate([qs_ref[j, p, 0:early], qs_ref[j, p, tq:tq + early]], axis=0)
        else:
            q_stack = lambda p: qs_ref[j, p]
        z = jnp.concatenate(
            [lax.dot_general(q_stack(p), k_blk(p) if shift is None else jnp.where(keep_k, k_blk(p), zero),
                             (((1,), (1,)), ((), ())), preferred_element_type=F32)
             for p in range(N_PAIRS)], axis=0)
        yield None
        sp = jnp.maximum(z, 0.0) + jnp.log(1.0 + jnp.exp(-jnp.abs(z)))
        if diagonal:
            sp = jnp.where(strict, sp, 0.0)
        if skip_early is not None:
            taking_part = (row & (tq - 1)) + jnp.where(skip_early, 0, tq) >= early
            sp = jnp.where(taking_part, sp, 0.0)
        hi = sp.astype(BF16)
        lo = (sp - hi.astype(F32)).astype(BF16)
        sums = jnp.dot(jnp.concatenate([hi, lo], axis=1), tri_ref[...] if shift is None else tri,
                       preferred_element_type=F32)
        yield None
        logit = z + sums[:, 0:tq]
        if carry is not None:
            logit = logit + carry
        a = jnp.exp(logit)
        if diagonal:
            a = jnp.where(strict, a, 0.0)
        if skip_early is not None:
            a = jnp.where(taking_part, a, 0.0)
        carry = sums[:, tq:2 * tq] if carry is None else carry + sums[:, tq:2 * tq]
        halves = (low, ~low) if shift is None else (lo_half, hi_half)
        v2 = [jnp.concatenate([jnp.where(m, v_blk(p), zero) for m in halves], axis=0)
              for p in range(N_PAIRS)]
        yield a.astype(BF16), carry, v2

    def finish(phases):
        out = None
        for out in phases:
            pass
        return out

    def apply_values(j, weights, values, carry, first):
        carry_ref[j] = carry
        heads = []
        for p in range(N_PAIRS):
            r0 = p * 2 * tq
            a2 = jnp.concatenate([w[r:r + tq] for w in weights for r in (r0, r0 + tq)], axis=1)
            v2 = jnp.concatenate([v[p] for v in values], axis=0)
            out = jnp.dot(a2, v2, preferred_element_type=F32)
            heads.append(out if first else acc_ref[j, p] + out)
            acc_ref[j, p] = heads[-1]
        max_ref[j] = jnp.max(carry)
        ysn_ref[slot, j * tq:(j + 1) * tq, :] = _rms(jnp.concatenate(heads, axis=1), sbn_ref[...]).astype(BF16)

    def main_block(dense):
        for j in range(n_q):
            for p in range(N_PAIRS):
                q2 = q_ref[j * tq:(j + 1) * tq, p * LANES:(p + 1) * LANES]
                qs_ref[j, p] = jnp.concatenate([jnp.where(low, q2, zero), jnp.where(low, zero, q2)], axis=0)

        state = {}

        def attention_item(j, b):
            weights, values, carry = state.get(j, ([], [], None))
            if j >= b:
                k_src, v_src, r0, shift = kc_ref, vc_ref, (j - b) * tq, None
            else:
                k_src, v_src, r0, shift = kp_ref, vp_ref, (n_q + j - b) * tq, prev_shift
            k_blk = lambda p: k_src[r0:r0 + tq, p * LANES:(p + 1) * LANES]
            v_blk = lambda p: v_src[r0:r0 + tq, p * LANES:(p + 1) * LANES]
            if b < FIRST_PASS_BLOCKS - 1:
                phases = block_weights(j, k_blk, v_blk, shift, carry, diagonal=(b == 0))
            else:
                phases = block_weights(j, k_blk, v_blk, shift, early_of(carry), False, only_early=True)
            out = None
            for out in phases:
                if out is None:
                    yield
            if b < FIRST_PASS_BLOCKS - 1:
                a, carry, v2 = out
            else:
                a, part, v2 = out
                a, carry = with_early(None, a, pad=True), with_early(carry, part)
            state[j] = (weights + [a], values + [v2], carry)
            if b == FIRST_PASS_BLOCKS - 1:
                apply_values(j, *state[j], first=True)

        items = [(j, b) for j in range(n_q) for b in range(FIRST_PASS_BLOCKS)]
        if not dense:
            for item in items:
                finish(attention_item(*item))
            return
        d_ff = wg_ref.shape[1]
        chunks = [slice(c, c + FF_CHUNK) for c in range(0, d_ff, FF_CHUNK)]
        subs = [slice(r, r + SUB_TILE) for r in range(0, tm, SUB_TILE)]
        slots = [(t, c) for t in range(len(subs)) for c in range(len(chunks))]
        visits = [phased for item in items for phased in [attention_item(*item)] for _ in range(3)]
        after_slot = {}
        for n, phased in enumerate(visits):
            after_slot.setdefault(slots[n * len(slots) // len(visits)], []).append(phased)

        x2s, hs = [], []
        for r in subs:
            y = jnp.concatenate([ysn_ref[1 - slot, r, :], yc_ref[r, :]], axis=1)
            x2s.append(x1_ref[r, :] + jnp.dot(y, wo_ref[...], preferred_element_type=F32))
            hs.append(_rms(x2s[-1], n2_ref[...]).astype(BF16))
        acts = [[] for _ in subs]
        for t, h in enumerate(hs):
            for ci, c in enumerate(chunks):
                gate = jnp.dot(h, wg_ref[:, c], preferred_element_type=F32)
                up = jnp.dot(h, wu_ref[:, c], preferred_element_type=F32)
                acts[t].append((gate * jax.nn.sigmoid(gate) * up).astype(BF16))
                for phased in after_slot.get((t, ci), []):
                    next(phased, None)
        for r, x2, act in zip(subs, x2s, acts):
            x3 = x2 + 0.5 * jnp.dot(jnp.concatenate(act, axis=1), wd_ref[...], preferred_element_type=F32)
            out_ref[r, :] = _rms(x3, fn_ref[...]) if final else x3

    weight_copies = [pltpu.make_async_copy(src, dst, wsem.at[n]) for n, (src, dst) in enumerate(
        ((wo_hbm, wo_ref), (wg_hbm, wg_ref), (wu_hbm, wu_ref), (wd_hbm, wd_ref)))]

    @pl.when(s == 0)
    def _():
        for cp in weight_copies:
            cp.start()
        main_block(False)
        for cp in weight_copies:
            cp.wait()

    pl.when(s > 0)(lambda: main_block(True))

    for j in range(n_q):
        def cond(state):
            kb, m = state
            return jnp.logical_and(kb >= 0, m > LOG_WEIGHT_UNDERFLOW)

        def body(state, j=j):
            kb, _ = state
            r0 = pl.multiple_of(seq_row0 + kb * tq, tq)
            copies = [pltpu.make_async_copy(src.at[pl.ds(r0, tq), :], dst, sem.at[n])
                      for n, (src, dst) in enumerate(((k_hbm, kbuf), (v_hbm, vbuf)))]
            for cp in copies:
                cp.start()
            for cp in copies:
                cp.wait()
            a, carry, v2 = finish(block_weights(
                j, lambda p: kbuf[:, p * LANES:(p + 1) * LANES], lambda p: vbuf[:, p * LANES:(p + 1) * LANES],
                None, carry_ref[j], diagonal=False, skip_early=(kb == resume)))
            apply_values(j, [a], [v2], carry, first=False)
            return kb - 1, max_ref[j]

        resume = first_block + j - (FIRST_PASS_BLOCKS - 1)
        lax.while_loop(cond, body, (resume, max_ref[j]))


def _resident(shape):
    return pl.BlockSpec(shape, lambda *_: (0,) * len(shape), pipeline_mode=pl.Buffered(1))


def _rows(tm, width):
    return pl.BlockSpec((tm, width), lambda i: (i, 0))


def _slab_spec(shape, steps):
    rows, cols = shape
    slab = next(r for r in range(BF16_SUBLANES, rows + 1, BF16_SUBLANES) if rows % r == 0 and r * steps >= rows)
    last = rows // slab - 1
    return pl.BlockSpec((slab, cols), lambda i: (jnp.minimum(i, last), 0))


def _pre_call(x, n1, wg, wu, wd, nm, win, cw, cb, cn, cast, *, seq):
    n, d = x.shape
    tm = PRE_ROW_TILE
    steps = n // tm
    d_ff = wg.shape[1]
    d_conv = cw.shape[1]
    kern = functools.partial(_pre_kernel, tiles_per_seq=seq // tm, q_scale=1.0 / math.sqrt(SB_HEAD_DIM),
                             n_cast=len(cast))
    cast_specs = [_slab_spec(w.shape, steps) for w in cast]
    hbm = pl.BlockSpec(memory_space=pl.ANY)
    own = (wg, wu, wd, win)
    assert all(w.shape[0] % STAGE_ROWS == 0 for w in own)
    stage_cols = max(w.shape[1] for w in own)
    return pl.pallas_call(
        kern,
        grid=(steps,),
        in_specs=[_rows(tm, d), _resident((1, d)), hbm, hbm, hbm, _resident((1, d)), hbm,
                  _resident(cw.shape), _resident((1, d_conv)), _resident((1, d_conv))] + cast_specs,
        out_specs=[_rows(tm, d), _rows(tm, D_SB), _rows(tm, D_SB), _rows(tm, D_SB), _rows(tm, d_conv)]
                  + cast_specs,
        out_shape=[jax.ShapeDtypeStruct((n, d), F32)] + [jax.ShapeDtypeStruct((n, D_SB), BF16)] * 3
                  + [jax.ShapeDtypeStruct((n, d_conv), BF16)]
                  + [jax.ShapeDtypeStruct(w.shape, BF16) for w in cast],
        scratch_shapes=[pltpu.VMEM((tm + 2 * CONV_HALO, d_conv), F32)]
                       + [pltpu.VMEM(w.shape, BF16) for w in own]
                       + [pltpu.VMEM((STAGE_SLOTS, STAGE_ROWS, stage_cols), F32),
                          pltpu.SemaphoreType.DMA((STAGE_SLOTS,))],
        compiler_params=pltpu.CompilerParams(dimension_semantics=("arbitrary",),
                                             vmem_limit_bytes=VMEM_LIMIT),
        name="ffn1_proj_conv",
    )(x, n1, wg, wu, wd, nm, win, cw, cb, cn, *cast)


def _suffix_sum_matrix(tk):
    j = jnp.arange(2 * tk)[:, None] % tk
    s = jnp.arange(2 * tk)[None, :]
    return -jnp.where(s < tk, j >= s, True).astype(BF16)


def _mix_post_call(q, k, v, x1, yc, sbn, wo, n2, wg, wu, wd, fn, *, seq, final):
    n, d = x1.shape
    tm = POST_ROW_TILE
    tq = ATT_BLOCK
    n_tiles = n // tm
    tiles_per_seq = seq // tm
    d_ff = wg.shape[1]
    cur = lambda s: (jnp.minimum(s, n_tiles - 1), 0)
    before_cur = lambda s: (jnp.maximum(jnp.minimum(s, n_tiles - 1) - 1, 0), 0)
    prev = lambda s: (jnp.maximum(s - 1, 0), 0)
    kern = functools.partial(_mix_post_kernel, tiles_per_seq=tiles_per_seq, n_tiles=n_tiles, final=final)
    any_space = pl.BlockSpec(memory_space=pl.ANY)
    return pl.pallas_call(
        kern,
        grid=(n_tiles + 1,),
        in_specs=[pl.BlockSpec((tm, D_SB), cur),
                  pl.BlockSpec((tm, D_SB), cur), pl.BlockSpec((tm, D_SB), before_cur),
                  pl.BlockSpec((tm, D_SB), cur), pl.BlockSpec((tm, D_SB), before_cur),
                  any_space, any_space,
                  _resident((2 * tq, 2 * tq)),
                  pl.BlockSpec((tm, d), prev), pl.BlockSpec((tm, yc.shape[1]), prev),
                  _resident((1, D_SB)), any_space, _resident((1, d)),
                  any_space, any_space, any_space, _resident((1, d))],
        out_specs=pl.BlockSpec((tm, d), prev),
        out_shape=jax.ShapeDtypeStruct((n, d), F32),
        scratch_shapes=[pltpu.VMEM((tm // tq, N_PAIRS, 2 * tq, LANES), BF16),
                        pltpu.VMEM((tm // tq, N_PAIRS * 2 * tq, tq), F32),
                        pltpu.VMEM((tm // tq, N_PAIRS, tq, LANES), F32),
                        pltpu.SMEM((tm // tq,), F32),
                        pltpu.VMEM((2, tm, D_SB), BF16),
                        pltpu.VMEM((tq, D_SB), BF16), pltpu.VMEM((tq, D_SB), BF16),
                        pltpu.SemaphoreType.DMA((2,))]
                       + [pltpu.VMEM(w.shape, BF16) for w in (wo, wg, wu, wd)]
                       + [pltpu.SemaphoreType.DMA((4,))],
        compiler_params=pltpu.CompilerParams(dimension_semantics=("arbitrary",),
                                             vmem_limit_bytes=VMEM_LIMIT),
        name="attention_out_proj_ffn2",
    )(q, k, k, v, v, k, v, _suffix_sum_matrix(tq), x1, yc, sbn, wo, n2, wg, wu, wd, fn)


def kernel(x, ffn1_norm, ffn1_w_gate, ffn1_w_up, ffn1_w_down, mix_norm, w_in, conv_w, conv_b,
           sb_out_norm, conv_out_norm, w_out, ffn2_norm, ffn2_w_gate, ffn2_w_up, ffn2_w_down,
           final_norm):
    batch, seq, d = x.shape
    depth = ffn1_norm.shape[0]
    assert seq % POST_ROW_TILE == 0 and seq % PRE_ROW_TILE == 0 and w_in.shape[2] == 6 * D_SB
    row = lambda t: t.reshape(1, -1)
    xs = x.reshape(batch * seq, d)
    for l in range(depth):
        x1, q, k, v, yc, wo, wg2, wu2, wd2 = _pre_call(
            xs, row(ffn1_norm[l]), ffn1_w_gate[l], ffn1_w_up[l], ffn1_w_down[l],
            row(mix_norm[l]), w_in[l], conv_w[l],
            row(conv_b[l]), row(conv_out_norm[l]),
            (w_out[l], ffn2_w_gate[l], ffn2_w_up[l], ffn2_w_down[l]), seq=seq)
        xs = _mix_post_call(
            q, k, v, x1, yc, row(sb_out_norm[l]), wo, row(ffn2_norm[l]), wg2, wu2, wd2,
            row(final_norm), seq=seq, final=(l == depth - 1))
    return xs.reshape(batch, seq, d)
```

```python
import functools
import math

import jax
import jax.numpy as jnp
from jax import lax
from jax.experimental import pallas as pl
from jax.experimental.pallas import tpu as pltpu

F32 = jnp.float32
BF16 = jnp.bfloat16

EPS = 1e-6
SB_HEADS = 8
SB_HEAD_DIM = 64
D_SB = SB_HEADS * SB_HEAD_DIM
CONV_WIDTH = 3
LANES = 128
BF16_SUBLANES = 16
N_PAIRS = D_SB // LANES
PRE_ROW_TILE = 512
POST_ROW_TILE = 512
SUB_TILE = 256
FF_CHUNK = 256
ATT_BLOCK = 128
FIRST_PASS_BLOCKS = 3
STAGE_ROWS = 128
STAGE_SLOTS = 6
CONV_HALO = 8
LOG_WEIGHT_UNDERFLOW = -104.0
V7X_VMEM_BYTES = 64 * 1024 * 1024
VMEM_LIMIT = V7X_VMEM_BYTES - 8 * 1024 * 1024


def _rms(x, g):
    return x * lax.rsqrt(jnp.mean(x * x, axis=-1, keepdims=True) + EPS) * g


def _load_as_bf16(pairs, stage, sem):
    slots, slab = stage.shape[0], stage.shape[1]
    jobs = [(src, dst, r) for src, dst in pairs for r in range(0, src.shape[0], slab)]

    def copy(n):
        src, _, r = jobs[n]
        k = n % slots
        return pltpu.make_async_copy(src.at[r:r + slab, :], stage.at[k, :, 0:src.shape[1]], sem.at[k])

    for n in range(min(slots - 1, len(jobs))):
        copy(n).start()
    for n, (src, dst, r) in enumerate(jobs):
        if n + slots - 1 < len(jobs):
            copy(n + slots - 1).start()
        copy(n).wait()
        dst[r:r + slab, :] = stage[n % slots, :, 0:src.shape[1]].astype(BF16)


def _pre_kernel(x_ref, n1_ref, wg_hbm, wu_hbm, wd_hbm, nm_ref, win_hbm, cw_ref, cb_ref, cn_ref,
                *rest, tiles_per_seq, q_scale, n_cast):
    cast_src = rest[:n_cast]
    x1_ref, q_ref, k_ref, v_ref, yc_ref = rest[n_cast:n_cast + 5]
    cast_dst = rest[n_cast + 5:2 * n_cast + 5]
    xc_buf, wg_ref, wu_ref, wd_ref, win_ref, stage, sem = rest[2 * n_cast + 5:]
    tm = x_ref.shape[0]
    i = pl.program_id(0)

    @pl.when(i == 0)
    def _():
        _load_as_bf16([(wg_hbm, wg_ref), (wu_hbm, wu_ref), (wd_hbm, wd_ref), (win_hbm, win_ref)], stage, sem)

    first = i % tiles_per_seq == 0

    @pl.when(first)
    def _():
        xc_buf[0:CONV_HALO, :] = jnp.zeros((CONV_HALO, xc_buf.shape[1]), F32)

    @pl.when(jnp.logical_not(first))
    def _():
        xc_buf[0:CONV_HALO, :] = xc_buf[tm:tm + CONV_HALO, :]

    d = D_SB
    subs = [slice(r, r + SUB_TILE) for r in range(0, tm, SUB_TILE)]
    xs = [x_ref[r, :] for r in subs]
    hs = [_rms(x, n1_ref[...]).astype(BF16) for x in xs]
    acts = []
    for h in hs:
        parts = []
        for c in range(0, wg_ref.shape[1], FF_CHUNK):
            gate = jnp.dot(h, wg_ref[:, c:c + FF_CHUNK], preferred_element_type=F32)
            up = jnp.dot(h, wu_ref[:, c:c + FF_CHUNK], preferred_element_type=F32)
            parts.append((gate * jax.nn.sigmoid(gate) * up).astype(BF16))
        acts.append(jnp.concatenate(parts, axis=1))
    for src, dst in zip(cast_src, cast_dst):
        dst[...] = src[...].astype(BF16)
    hs = []
    for r, x, act in zip(subs, xs, acts):
        x1 = x + 0.5 * jnp.dot(act, wd_ref[...], preferred_element_type=F32)
        x1_ref[r, :] = x1
        hs.append(_rms(x1, nm_ref[...]).astype(BF16))
    for r, h in zip(subs, hs):
        cu = jnp.dot(h, win_ref[:, 4 * d:6 * d], preferred_element_type=F32)
        xc_buf[CONV_HALO + r.start:CONV_HALO + r.stop, :] = cu[:, 0:d] * cu[:, d:2 * d]
    convs = []
    for r in subs:
        y = cb_ref[...]
        for j in range(CONV_WIDTH):
            off = CONV_HALO - (CONV_WIDTH - 1) + j + r.start
            y = y + xc_buf[off:off + SUB_TILE, :] * cw_ref[j:j + 1, :]
        convs.append(y)
    for r, h, y in zip(subs, hs, convs):
        gate_b = jnp.dot(h, win_ref[:, 3 * d:4 * d], preferred_element_type=F32)
        yc_ref[r, :] = _rms(gate_b * y, cn_ref[...]).astype(BF16)
        proj = jnp.dot(h, win_ref[:, 0:3 * d], preferred_element_type=F32)
        q_ref[r, :] = (proj[:, 0:d] * q_scale).astype(BF16)
        k_ref[r, :] = proj[:, d:2 * d].astype(BF16)
        v_ref[r, :] = proj[:, 2 * d:3 * d].astype(BF16)


def _mix_post_kernel(q_ref, kc_ref, kp_ref, vc_ref, vp_ref, k_hbm, v_hbm, tri_ref, x1_ref, yc_ref,
                     sbn_ref, wo_hbm, n2_ref, wg_hbm, wu_hbm, wd_hbm, fn_ref, out_ref,
                     qs_ref, carry_ref, acc_ref, max_ref, ysn_ref, kbuf, vbuf, sem,
                     wo_ref, wg_ref, wu_ref, wd_ref, wsem,
                     *, tiles_per_seq, n_tiles, final):
    tm = q_ref.shape[0]
    tq = ATT_BLOCK
    n_q = tm // tq
    rows = N_PAIRS * 2 * tq
    s = pl.program_id(0)
    slot = s % 2
    tile = jnp.minimum(s, n_tiles - 1)
    tile_in_seq = tile % tiles_per_seq
    first_block = tile_in_seq * n_q
    seq_row0 = (tile - tile_in_seq) * tm

    lane = lax.broadcasted_iota(jnp.int32, (tq, LANES), 1)
    low = lane < SB_HEAD_DIM
    row = lax.broadcasted_iota(jnp.int32, (rows, tq), 0)
    col = lax.broadcasted_iota(jnp.int32, (rows, tq), 1)
    strict = col < (row & (tq - 1))
    tri_col = lax.broadcasted_iota(jnp.int32, tri_ref.shape, 1)
    zero = jnp.zeros((), BF16)
    prev_shift = jnp.where(tile_in_seq == 0, 2 * tq, 0)

    early = tq // 2
    head_rows = [p * 2 * tq + h * tq for p in range(N_PAIRS) for h in range(2)]

    def early_of(x):
        return jnp.concatenate([x[r:r + early] for r in head_rows], axis=0)

    def with_early(full, part, pad=False):
        pieces = []
        for n, r in enumerate(head_rows):
            rest = jnp.zeros((tq - early, part.shape[1]), part.dtype) if pad else full[r + early:r + tq]
            pieces += [part[n * early:(n + 1) * early], rest]
        return jnp.concatenate(pieces, axis=0)

    def block_weights(j, k_blk, v_blk, shift, carry, diagonal, only_early=False, skip_early=None):
        if shift is None:
            keep_k = lo_half = hi_half = tri = None
        else:
            k_lane = lane + shift
            keep_k, lo_half = k_lane < LANES, k_lane < SB_HEAD_DIM
            hi_half = (k_lane >= SB_HEAD_DIM) & keep_k
            tri = jnp.where(tri_col + shift < 2 * tq, tri_ref[...], zero)
        if only_early:
            q_stack = lambda p: jnp.concatenate([qs_ref[j, p, 0:early], qs_ref[j, p, tq:tq + early]], axis=0)
        else:
            q_stack = lambda p: qs_ref[j, p]
        z = jnp.concatenate(
            [lax.dot_general(q_stack(p), k_blk(p) if shift is None else jnp.where(keep_k, k_blk(p), zero),
                             (((1,), (1,)), ((), ())), preferred_element_type=F32)
             for p in range(N_PAIRS)], axis=0)
        yield None
        sp = jnp.maximum(z, 0.0) + jnp.log(1.0 + jnp.exp(-jnp.abs(z)))
        if diagonal:
            sp = jnp.where(strict, sp, 0.0)
        if skip_early is not None:
            taking_part = (row & (tq - 1)) + jnp.where(skip_early, 0, tq) >= early
            sp = jnp.where(taking_part, sp, 0.0)
        sums = jnp.dot(sp.astype(BF16), tri_ref[...] if shift is None else tri, preferred_element_type=F32)
        yield None
        logit = z + sums[:, 0:tq]
        if carry is not None:
            logit = logit + carry
        a = jnp.exp(logit)
        if diagonal:
            a = jnp.where(strict, a, 0.0)
        if skip_early is not None:
            a = jnp.where(taking_part, a, 0.0)
        carry = sums[:, tq:2 * tq] if carry is None else carry + sums[:, tq:2 * tq]
        halves = (low, ~low) if shift is None else (lo_half, hi_half)
        v2 = [jnp.concatenate([jnp.where(m, v_blk(p), zero) for m in halves], axis=0)
              for p in range(N_PAIRS)]
        yield a.astype(BF16), carry, v2

    def finish(phases):
        out = None
        for out in phases:
            pass
        return out

    def apply_values(j, weights, values, carry, first):
        carry_ref[j] = carry
        heads = []
        for p in range(N_PAIRS):
            r0 = p * 2 * tq
            a2 = jnp.concatenate([w[r:r + tq] for w in weights for r in (r0, r0 + tq)], axis=1)
            v2 = jnp.concatenate([v[p] for v in values], axis=0)
            out = jnp.dot(a2, v2, preferred_element_type=F32)
            heads.append(out if first else acc_ref[j, p] + out)
            acc_ref[j, p] = heads[-1]
        max_ref[j] = jnp.max(carry)
        ysn_ref[slot, j * tq:(j + 1) * tq, :] = _rms(jnp.concatenate(heads, axis=1), sbn_ref[...]).astype(BF16)

    def main_block(dense):
        for j in range(n_q):
            for p in range(N_PAIRS):
                q2 = q_ref[j * tq:(j + 1) * tq, p * LANES:(p + 1) * LANES]
                qs_ref[j, p] = jnp.concatenate([jnp.where(low, q2, zero), jnp.where(low, zero, q2)], axis=0)

        state = {}

        def attention_item(j, b):
            weights, values, carry = state.get(j, ([], [], None))
            if j >= b:
                k_src, v_src, r0, shift = kc_ref, vc_ref, (j - b) * tq, None
            else:
                k_src, v_src, r0, shift = kp_ref, vp_ref, (n_q + j - b) * tq, prev_shift
            k_blk = lambda p: k_src[r0:r0 + tq, p * LANES:(p + 1) * LANES]
            v_blk = lambda p: v_src[r0:r0 + tq, p * LANES:(p + 1) * LANES]
            if b < FIRST_PASS_BLOCKS - 1:
                phases = block_weights(j, k_blk, v_blk, shift, carry, diagonal=(b == 0))
            else:
                phases = block_weights(j, k_blk, v_blk, shift, early_of(carry), False, only_early=True)
            out = None
            for out in phases:
                if out is None:
                    yield
            if b < FIRST_PASS_BLOCKS - 1:
                a, carry, v2 = out
            else:
                a, part, v2 = out
                a, carry = with_early(None, a, pad=True), with_early(carry, part)
            state[j] = (weights + [a], values + [v2], carry)
            if b == FIRST_PASS_BLOCKS - 1:
                apply_values(j, *state[j], first=True)

        items = [(j, b) for j in range(n_q) for b in range(FIRST_PASS_BLOCKS)]
        if not dense:
            for item in items:
                finish(attention_item(*item))
            return
        d_ff = wg_ref.shape[1]
        chunks = [slice(c, c + FF_CHUNK) for c in range(0, d_ff, FF_CHUNK)]
        subs = [slice(r, r + SUB_TILE) for r in range(0, tm, SUB_TILE)]
        slots = [(t, c) for t in range(len(subs)) for c in range(len(chunks))]
        visits = [phased for item in items for phased in [attention_item(*item)] for _ in range(3)]
        after_slot = {}
        for n, phased in enumerate(visits):
            after_slot.setdefault(slots[n * len(slots) // len(visits)], []).append(phased)

        x2s, hs = [], []
        for r in subs:
            y = jnp.concatenate([ysn_ref[1 - slot, r, :], yc_ref[r, :]], axis=1)
            x2s.append(x1_ref[r, :] + jnp.dot(y, wo_ref[...], preferred_element_type=F32))
            hs.append(_rms(x2s[-1], n2_ref[...]).astype(BF16))
        acts = [[] for _ in subs]
        for t, h in enumerate(hs):
            for ci, c in enumerate(chunks):
                gate = jnp.dot(h, wg_ref[:, c], preferred_element_type=F32)
                up = jnp.dot(h, wu_ref[:, c], preferred_element_type=F32)
                acts[t].append((gate * jax.nn.sigmoid(gate) * up).astype(BF16))
                for phased in after_slot.get((t, ci), []):
                    next(phased, None)
        for r, x2, act in zip(subs, x2s, acts):
            x3 = x2 + 0.5 * jnp.dot(jnp.concatenate(act, axis=1), wd_ref[...], preferred_element_type=F32)
            out_ref[r, :] = _rms(x3, fn_ref[...]) if final else x3

    weight_copies = [pltpu.make_async_copy(src, dst, wsem.at[n]) for n, (src, dst) in enumerate(
        ((wo_hbm, wo_ref), (wg_hbm, wg_ref), (wu_hbm, wu_ref), (wd_hbm, wd_ref)))]

    @pl.when(s == 0)
    def _():
        for cp in weight_copies:
            cp.start()
        main_block(False)
        for cp in weight_copies:
            cp.wait()

    pl.when(s > 0)(lambda: main_block(True))

    for j in range(n_q):
        def cond(state):
            kb, m = state
            return jnp.logical_and(kb >= 0, m > LOG_WEIGHT_UNDERFLOW)

        def body(state, j=j):
            kb, _ = state
            r0 = pl.multiple_of(seq_row0 + kb * tq, tq)
            copies = [pltpu.make_async_copy(src.at[pl.ds(r0, tq), :], dst, sem.at[n])
                      for n, (src, dst) in enumerate(((k_hbm, kbuf), (v_hbm, vbuf)))]
            for cp in copies:
                cp.start()
            for cp in copies:
                cp.wait()
            a, carry, v2 = finish(block_weights(
                j, lambda p: kbuf[:, p * LANES:(p + 1) * LANES], lambda p: vbuf[:, p * LANES:(p + 1) * LANES],
                None, carry_ref[j], diagonal=False, skip_early=(kb == resume)))
            apply_values(j, [a], [v2], carry, first=False)
            return kb - 1, max_ref[j]

        resume = first_block + j - (FIRST_PASS_BLOCKS - 1)
        lax.while_loop(cond, body, (resume, max_ref[j]))


def _resident(shape):
    return pl.BlockSpec(shape, lambda *_: (0,) * len(shape), pipeline_mode=pl.Buffered(1))


def _rows(tm, width):
    return pl.BlockSpec((tm, width), lambda i: (i, 0))


def _slab_spec(shape, steps):
    rows, cols = shape
    slab = next(r for r in range(BF16_SUBLANES, rows + 1, BF16_SUBLANES) if rows % r == 0 and r * steps >= rows)
    last = rows // slab - 1
    return pl.BlockSpec((slab, cols), lambda i: (jnp.minimum(i, last), 0))


def _pre_call(x, n1, wg, wu, wd, nm, win, cw, cb, cn, cast, *, seq):
    n, d = x.shape
    tm = PRE_ROW_TILE
    steps = n // tm
    d_ff = wg.shape[1]
    d_conv = cw.shape[1]
    kern = functools.partial(_pre_kernel, tiles_per_seq=seq // tm, q_scale=1.0 / math.sqrt(SB_HEAD_DIM),
                             n_cast=len(cast))
    cast_specs = [_slab_spec(w.shape, steps) for w in cast]
    hbm = pl.BlockSpec(memory_space=pl.ANY)
    own = (wg, wu, wd, win)
    assert all(w.shape[0] % STAGE_ROWS == 0 for w in own)
    stage_cols = max(w.shape[1] for w in own)
    return pl.pallas_call(
        kern,
        grid=(steps,),
        in_specs=[_rows(tm, d), _resident((1, d)), hbm, hbm, hbm, _resident((1, d)), hbm,
                  _resident(cw.shape), _resident((1, d_conv)), _resident((1, d_conv))] + cast_specs,
        out_specs=[_rows(tm, d), _rows(tm, D_SB), _rows(tm, D_SB), _rows(tm, D_SB), _rows(tm, d_conv)]
                  + cast_specs,
        out_shape=[jax.ShapeDtypeStruct((n, d), F32)] + [jax.ShapeDtypeStruct((n, D_SB), BF16)] * 3
                  + [jax.ShapeDtypeStruct((n, d_conv), BF16)]
                  + [jax.ShapeDtypeStruct(w.shape, BF16) for w in cast],
        scratch_shapes=[pltpu.VMEM((tm + 2 * CONV_HALO, d_conv), F32)]
                       + [pltpu.VMEM(w.shape, BF16) for w in own]
                       + [pltpu.VMEM((STAGE_SLOTS, STAGE_ROWS, stage_cols), F32),
                          pltpu.SemaphoreType.DMA((STAGE_SLOTS,))],
        compiler_params=pltpu.CompilerParams(dimension_semantics=("arbitrary",),
                                             vmem_limit_bytes=VMEM_LIMIT),
        name="ffn1_proj_conv",
    )(x, n1, wg, wu, wd, nm, win, cw, cb, cn, *cast)


def _suffix_sum_matrix(tk):
    j = jnp.arange(tk)[:, None]
    s = jnp.arange(2 * tk)[None, :]
    return -jnp.where(s < tk, j >= s, True).astype(BF16)


def _mix_post_call(q, k, v, x1, yc, sbn, wo, n2, wg, wu, wd, fn, *, seq, final):
    n, d = x1.shape
    tm = POST_ROW_TILE
    tq = ATT_BLOCK
    n_tiles = n // tm
    tiles_per_seq = seq // tm
    d_ff = wg.shape[1]
    cur = lambda s: (jnp.minimum(s, n_tiles - 1), 0)
    before_cur = lambda s: (jnp.maximum(jnp.minimum(s, n_tiles - 1) - 1, 0), 0)
    prev = lambda s: (jnp.maximum(s - 1, 0), 0)
    kern = functools.partial(_mix_post_kernel, tiles_per_seq=tiles_per_seq, n_tiles=n_tiles, final=final)
    any_space = pl.BlockSpec(memory_space=pl.ANY)
    return pl.pallas_call(
        kern,
        grid=(n_tiles + 1,),
        in_specs=[pl.BlockSpec((tm, D_SB), cur),
                  pl.BlockSpec((tm, D_SB), cur), pl.BlockSpec((tm, D_SB), before_cur),
                  pl.BlockSpec((tm, D_SB), cur), pl.BlockSpec((tm, D_SB), before_cur),
                  any_space, any_space,
                  _resident((tq, 2 * tq)),
                  pl.BlockSpec((tm, d), prev), pl.BlockSpec((tm, yc.shape[1]), prev),
                  _resident((1, D_SB)), any_space, _resident((1, d)),
                  any_space, any_space, any_space, _resident((1, d))],
        out_specs=pl.BlockSpec((tm, d), prev),
        out_shape=jax.ShapeDtypeStruct((n, d), F32),
        scratch_shapes=[pltpu.VMEM((tm // tq, N_PAIRS, 2 * tq, LANES), BF16),
                        pltpu.VMEM((tm // tq, N_PAIRS * 2 * tq, tq), F32),
                        pltpu.VMEM((tm // tq, N_PAIRS, tq, LANES), F32),
                        pltpu.SMEM((tm // tq,), F32),
                        pltpu.VMEM((2, tm, D_SB), BF16),
                        pltpu.VMEM((tq, D_SB), BF16), pltpu.VMEM((tq, D_SB), BF16),
                        pltpu.SemaphoreType.DMA((2,))]
                       + [pltpu.VMEM(w.shape, BF16) for w in (wo, wg, wu, wd)]
                       + [pltpu.SemaphoreType.DMA((4,))],
        compiler_params=pltpu.CompilerParams(dimension_semantics=("arbitrary",),
                                             vmem_limit_bytes=VMEM_LIMIT),
        name="attention_out_proj_ffn2",
    )(q, k, k, v, v, k, v, _suffix_sum_matrix(tq), x1, yc, sbn, wo, n2, wg, wu, wd, fn)


def kernel(x, ffn1_norm, ffn1_w_gate, ffn1_w_up, ffn1_w_down, mix_norm, w_in, conv_w, conv_b,
           sb_out_norm, conv_out_norm, w_out, ffn2_norm, ffn2_w_gate, ffn2_w_up, ffn2_w_down,
           final_norm):
    batch, seq, d = x.shape
    depth = ffn1_norm.shape[0]
    assert seq % POST_ROW_TILE == 0 and seq % PRE_ROW_TILE == 0 and w_in.shape[2] == 6 * D_SB
    row = lambda t: t.reshape(1, -1)
    xs = x.reshape(batch * seq, d)
    for l in range(depth):
        x1, q, k, v, yc, wo, wg2, wu2, wd2 = _pre_call(
            xs, row(ffn1_norm[l]), ffn1_w_gate[l], ffn1_w_up[l], ffn1_w_down[l],
            row(mix_norm[l]), w_in[l], conv_w[l],
            row(conv_b[l]), row(conv_out_norm[l]),
            (w_out[l], ffn2_w_gate[l], ffn2_w_up[l], ffn2_w_down[l]), seq=seq)
        xs = _mix_post_call(
            q, k, v, x1, yc, row(sb_out_norm[l]), wo, row(ffn2_norm[l]), wg2, wu2, wd2,
            row(final_norm), seq=seq, final=(l == depth - 1))
    return xs.reshape(batch, seq, d)
```

```python
import functools
import math

import jax
import jax.numpy as jnp
from jax import lax
from jax.experimental import pallas as pl
from jax.experimental.pallas import tpu as pltpu

F32 = jnp.float32
BF16 = jnp.bfloat16

EPS = 1e-6
SB_HEADS = 8
SB_HEAD_DIM = 64
D_SB = SB_HEADS * SB_HEAD_DIM
CONV_WIDTH = 3
LANES = 128
BF16_SUBLANES = 16
N_PAIRS = D_SB // LANES
PRE_ROW_TILE = 512
POST_ROW_TILE = 512
SUB_TILE = 256
FF_CHUNK = 256
ATT_BLOCK = 128
FIRST_PASS_BLOCKS = 3
STAGE_ROWS = 128
STAGE_SLOTS = 6
CONV_HALO = 8
LOG_WEIGHT_UNDERFLOW = -104.0
V7X_VMEM_BYTES = 64 * 1024 * 1024
VMEM_LIMIT = V7X_VMEM_BYTES - 8 * 1024 * 1024


def _rms(x, g):
    return x * lax.rsqrt(jnp.mean(x * x, axis=-1, keepdims=True) + EPS) * g


def _load_as_bf16(pairs, stage, sem):
    slots, slab = stage.shape[0], stage.shape[1]
    jobs = [(src, dst, r) for src, dst in pairs for r in range(0, src.shape[0], slab)]

    def copy(n):
        src, _, r = jobs[n]
        k = n % slots
        return pltpu.make_async_copy(src.at[r:r + slab, :], stage.at[k, :, 0:src.shape[1]], sem.at[k])

    for n in range(min(slots - 1, len(jobs))):
        copy(n).start()
    for n, (src, dst, r) in enumerate(jobs):
        if n + slots - 1 < len(jobs):
            copy(n + slots - 1).start()
        copy(n).wait()
        dst[r:r + slab, :] = stage[n % slots, :, 0:src.shape[1]].astype(BF16)


def _pre_kernel(x_ref, n1_ref, wg_hbm, wu_hbm, wd_hbm, nm_ref, win_hbm, cw_ref, cb_ref, cn_ref,
                *rest, tiles_per_seq, q_scale, n_cast):
    cast_src = rest[:n_cast]
    x1_ref, q_ref, k_ref, v_ref, yc_ref = rest[n_cast:n_cast + 5]
    cast_dst = rest[n_cast + 5:2 * n_cast + 5]
    xc_buf, wg_ref, wu_ref, wd_ref, win_ref, stage, sem = rest[2 * n_cast + 5:]
    tm = x_ref.shape[0]
    i = pl.program_id(0)

    @pl.when(i == 0)
    def _():
        _load_as_bf16([(wg_hbm, wg_ref), (wu_hbm, wu_ref), (wd_hbm, wd_ref), (win_hbm, win_ref)], stage, sem)

    first = i % tiles_per_seq == 0

    @pl.when(first)
    def _():
        xc_buf[0:CONV_HALO, :] = jnp.zeros((CONV_HALO, xc_buf.shape[1]), F32)

    @pl.when(jnp.logical_not(first))
    def _():
        xc_buf[0:CONV_HALO, :] = xc_buf[tm:tm + CONV_HALO, :]

    d = D_SB
    subs = [slice(r, r + SUB_TILE) for r in range(0, tm, SUB_TILE)]
    xs = [x_ref[r, :] for r in subs]
    hs = [_rms(x, n1_ref[...]).astype(BF16) for x in xs]
    acts = []
    for h in hs:
        parts = []
        for c in range(0, wg_ref.shape[1], FF_CHUNK):
            gate = jnp.dot(h, wg_ref[:, c:c + FF_CHUNK], preferred_element_type=F32)
            up = jnp.dot(h, wu_ref[:, c:c + FF_CHUNK], preferred_element_type=F32)
            parts.append((gate * jax.nn.sigmoid(gate) * up).astype(BF16))
        acts.append(jnp.concatenate(parts, axis=1))
    for src, dst in zip(cast_src, cast_dst):
        dst[...] = src[...].astype(BF16)
    hs = []
    for r, x, act in zip(subs, xs, acts):
        x1 = x + 0.5 * jnp.dot(act, wd_ref[...], preferred_element_type=F32)
        x1_ref[r, :] = x1
        hs.append(_rms(x1, nm_ref[...]).astype(BF16))
    for r, h in zip(subs, hs):
        cu = jnp.dot(h, win_ref[:, 4 * d:6 * d], preferred_element_type=F32)
        xc_buf[CONV_HALO + r.start:CONV_HALO + r.stop, :] = cu[:, 0:d] * cu[:, d:2 * d]
    convs = []
    for r in subs:
        y = cb_ref[...]
        for j in range(CONV_WIDTH):
            off = CONV_HALO - (CONV_WIDTH - 1) + j + r.start
            y = y + xc_buf[off:off + SUB_TILE, :] * cw_ref[j:j + 1, :]
        convs.append(y)
    for r, h, y in zip(subs, hs, convs):
        gate_b = jnp.dot(h, win_ref[:, 3 * d:4 * d], preferred_element_type=F32)
        yc_ref[r, :] = _rms(gate_b * y, cn_ref[...]).astype(BF16)
        proj = jnp.dot(h, win_ref[:, 0:3 * d], preferred_element_type=F32)
        q_ref[r, :] = (proj[:, 0:d] * q_scale).astype(BF16)
        k_ref[r, :] = proj[:, d:2 * d].astype(BF16)
        v_ref[r, :] = proj[:, 2 * d:3 * d].astype(BF16)


def _mix_post_kernel(q_ref, kc_ref, kp_ref, vc_ref, vp_ref, k_hbm, v_hbm, tri_ref, tri2_ref, x1_ref, yc_ref,
                     sbn_ref, wo_hbm, n2_ref, wg_hbm, wu_hbm, wd_hbm, fn_ref, out_ref,
                     qs_ref, carry_ref, acc_ref, max_ref, ysn_ref, kbuf, vbuf, sem,
                     wo_ref, wg_ref, wu_ref, wd_ref, wsem,
                     *, tiles_per_seq, n_tiles, final):
    tm = q_ref.shape[0]
    tq = ATT_BLOCK
    n_q = tm // tq
    rows = N_PAIRS * 2 * tq
    s = pl.program_id(0)
    slot = s % 2
    tile = jnp.minimum(s, n_tiles - 1)
    tile_in_seq = tile % tiles_per_seq
    first_block = tile_in_seq * n_q
    seq_row0 = (tile - tile_in_seq) * tm

    lane = lax.broadcasted_iota(jnp.int32, (tq, LANES), 1)
    low = lane < SB_HEAD_DIM
    row = lax.broadcasted_iota(jnp.int32, (rows, tq), 0)
    col = lax.broadcasted_iota(jnp.int32, (rows, tq), 1)
    strict = col < (row & (tq - 1))
    tri_col = lax.broadcasted_iota(jnp.int32, tri_ref.shape, 1)
    zero = jnp.zeros((), BF16)
    prev_shift = jnp.where(tile_in_seq == 0, 2 * tq, 0)

    early = tq // 2
    head_rows = [p * 2 * tq + h * tq for p in range(N_PAIRS) for h in range(2)]

    def early_of(x):
        return jnp.concatenate([x[r:r + early] for r in head_rows], axis=0)

    def with_early(full, part, pad=False):
        pieces = []
        for n, r in enumerate(head_rows):
            rest = jnp.zeros((tq - early, part.shape[1]), part.dtype) if pad else full[r + early:r + tq]
            pieces += [part[n * early:(n + 1) * early], rest]
        return jnp.concatenate(pieces, axis=0)

    def block_weights(j, k_blk, v_blk, shift, carry, diagonal, only_early=False, skip_early=None):
        if shift is None:
            keep_k = lo_half = hi_half = tri = None
        else:
            k_lane = lane + shift
            keep_k, lo_half = k_lane < LANES, k_lane < SB_HEAD_DIM
            hi_half = (k_lane >= SB_HEAD_DIM) & keep_k
            tri = jnp.where(tri_col + shift < 2 * tq, tri_ref[...], zero)
        if only_early:
            q_stack = lambda p: jnp.concatenate([qs_ref[j, p, 0:early], qs_ref[j, p, tq:tq + early]], axis=0)
        else:
            q_stack = lambda p: qs_ref[j, p]
        z = jnp.concatenate(
            [lax.dot_general(q_stack(p), k_blk(p) if shift is None else jnp.where(keep_k, k_blk(p), zero),
                             (((1,), (1,)), ((), ())), preferred_element_type=F32)
             for p in range(N_PAIRS)], axis=0)
        yield None
        sp = jnp.maximum(z, 0.0) + jnp.log(1.0 + jnp.exp(-jnp.abs(z)))
        if diagonal:
            sp = jnp.where(strict, sp, 0.0)
        if skip_early is not None:
            taking_part = (row & (tq - 1)) + jnp.where(skip_early, 0, tq) >= early
            sp = jnp.where(taking_part, sp, 0.0)
        sums = jnp.dot(sp.astype(BF16), tri_ref[...] if shift is None else tri, preferred_element_type=F32)
        yield None
        logit = z + sums[:, 0:tq]
        if carry is not None:
            logit = logit + carry
        a = jnp.exp(logit)
        if diagonal:
            a = jnp.where(strict, a, 0.0)
        if skip_early is not None:
            a = jnp.where(taking_part, a, 0.0)
        carry = sums[:, tq:2 * tq] if carry is None else carry + sums[:, tq:2 * tq]
        halves = (low, ~low) if shift is None else (lo_half, hi_half)
        v2 = [jnp.concatenate([jnp.where(m, v_blk(p), zero) for m in halves], axis=0)
              for p in range(N_PAIRS)]
        yield a.astype(BF16), carry, v2

    def pair_weights(j, k0, v0, k1, v1, shift1):
        keep = None if shift1 is None else lane + shift1 < LANES
        masked = lambda blk: blk if keep is None else jnp.where(keep, blk, zero)
        zs = [jnp.concatenate(
            [lax.dot_general(qs_ref[j, p], kb(p), (((1,), (1,)), ((), ())), preferred_element_type=F32)
             for p in range(N_PAIRS)], axis=0) for kb in (k0, lambda p: masked(k1(p)))]
        yield None
        sps = [jnp.maximum(z, 0.0) + jnp.log(1.0 + jnp.exp(-jnp.abs(z))) for z in zs]
        sps[0] = jnp.where(strict, sps[0], 0.0)
        if shift1 is not None:
            sps[1] = jnp.where(col + shift1 < tq, sps[1], 0.0)
        sums = jnp.dot(jnp.concatenate([t.astype(BF16) for t in sps], axis=1), tri2_ref[...],
                       preferred_element_type=F32)
        yield None
        total0 = jnp.broadcast_to(sums[:, 0:1], (rows, tq))
        total1 = jnp.broadcast_to(sums[:, tq:tq + 1], (rows, tq))
        a0 = jnp.where(strict, jnp.exp(zs[0] + sums[:, 0:tq]), 0.0).astype(BF16)
        a1 = jnp.exp(zs[1] + sums[:, tq:2 * tq] + total0).astype(BF16)
        values = [[jnp.concatenate([jnp.where(m, vb(p), zero) for m in (low, ~low)], axis=0)
                   for p in range(N_PAIRS)] for vb in (v0, lambda p: masked(v1(p)))]
        yield [a0, a1], total0 + total1, values

    def finish(phases):
        out = None
        for out in phases:
            pass
        return out

    def apply_values(j, weights, values, carry, first):
        carry_ref[j] = carry
        heads = []
        for p in range(N_PAIRS):
            r0 = p * 2 * tq
            a2 = jnp.concatenate([w[r:r + tq] for w in weights for r in (r0, r0 + tq)], axis=1)
            v2 = jnp.concatenate([v[p] for v in values], axis=0)
            out = jnp.dot(a2, v2, preferred_element_type=F32)
            heads.append(out if first else acc_ref[j, p] + out)
            acc_ref[j, p] = heads[-1]
        max_ref[j] = jnp.max(carry)
        ysn_ref[slot, j * tq:(j + 1) * tq, :] = _rms(jnp.concatenate(heads, axis=1), sbn_ref[...]).astype(BF16)

    def main_block(dense):
        for j in range(n_q):
            for p in range(N_PAIRS):
                q2 = q_ref[j * tq:(j + 1) * tq, p * LANES:(p + 1) * LANES]
                qs_ref[j, p] = jnp.concatenate([jnp.where(low, q2, zero), jnp.where(low, zero, q2)], axis=0)

        state = {}

        def key_block(j, b):
            if j >= b:
                k_src, v_src, r0, shift = kc_ref, vc_ref, (j - b) * tq, None
            else:
                k_src, v_src, r0, shift = kp_ref, vp_ref, (n_q + j - b) * tq, prev_shift
            return (lambda p: k_src[r0:r0 + tq, p * LANES:(p + 1) * LANES],
                    lambda p: v_src[r0:r0 + tq, p * LANES:(p + 1) * LANES], shift)

        def attention_item(j, last):
            if not last:
                k0, v0, _ = key_block(j, 0)
                k1, v1, shift1 = key_block(j, 1)
                phases = pair_weights(j, k0, v0, k1, v1, shift1)
            else:
                weights, values, carry = state[j]
                k_blk, v_blk, shift = key_block(j, FIRST_PASS_BLOCKS - 1)
                phases = block_weights(j, k_blk, v_blk, shift, early_of(carry), False, only_early=True)
            out = None
            for out in phases:
                if out is None:
                    yield
            if not last:
                state[j] = (out[0], out[2], out[1])
            else:
                a, part, v2 = out
                apply_values(j, weights + [with_early(None, a, pad=True)], values + [v2],
                             with_early(carry, part), first=True)

        items = [(j, last) for j in range(n_q) for last in (False, True)]
        if not dense:
            for item in items:
                finish(attention_item(*item))
            return
        d_ff = wg_ref.shape[1]
        chunks = [slice(c, c + FF_CHUNK) for c in range(0, d_ff, FF_CHUNK)]
        subs = [slice(r, r + SUB_TILE) for r in range(0, tm, SUB_TILE)]
        slots = [(t, c) for t in range(len(subs)) for c in range(len(chunks))]
        visits = [phased for item in items for phased in [attention_item(*item)] for _ in range(3)]
        after_slot = {}
        for n, phased in enumerate(visits):
            after_slot.setdefault(slots[n * len(slots) // len(visits)], []).append(phased)

        x2s, hs = [], []
        for r in subs:
            y = jnp.concatenate([ysn_ref[1 - slot, r, :], yc_ref[r, :]], axis=1)
            x2s.append(x1_ref[r, :] + jnp.dot(y, wo_ref[...], preferred_element_type=F32))
            hs.append(_rms(x2s[-1], n2_ref[...]).astype(BF16))
        acts = [[] for _ in subs]
        for t, h in enumerate(hs):
            for ci, c in enumerate(chunks):
                gate = jnp.dot(h, wg_ref[:, c], preferred_element_type=F32)
                up = jnp.dot(h, wu_ref[:, c], preferred_element_type=F32)
                acts[t].append((gate * jax.nn.sigmoid(gate) * up).astype(BF16))
                for phased in after_slot.get((t, ci), []):
                    next(phased, None)
        for r, x2, act in zip(subs, x2s, acts):
            x3 = x2 + 0.5 * jnp.dot(jnp.concatenate(act, axis=1), wd_ref[...], preferred_element_type=F32)
            out_ref[r, :] = _rms(x3, fn_ref[...]) if final else x3

    weight_copies = [pltpu.make_async_copy(src, dst, wsem.at[n]) for n, (src, dst) in enumerate(
        ((wo_hbm, wo_ref), (wg_hbm, wg_ref), (wu_hbm, wu_ref), (wd_hbm, wd_ref)))]

    @pl.when(s == 0)
    def _():
        for cp in weight_copies:
            cp.start()
        main_block(False)
        for cp in weight_copies:
            cp.wait()

    pl.when(s > 0)(lambda: main_block(True))

    for j in range(n_q):
        def cond(state):
            kb, m = state
            return jnp.logical_and(kb >= 0, m > LOG_WEIGHT_UNDERFLOW)

        def body(state, j=j):
            kb, _ = state
            r0 = pl.multiple_of(seq_row0 + kb * tq, tq)
            copies = [pltpu.make_async_copy(src.at[pl.ds(r0, tq), :], dst, sem.at[n])
                      for n, (src, dst) in enumerate(((k_hbm, kbuf), (v_hbm, vbuf)))]
            for cp in copies:
                cp.start()
            for cp in copies:
                cp.wait()
            a, carry, v2 = finish(block_weights(
                j, lambda p: kbuf[:, p * LANES:(p + 1) * LANES], lambda p: vbuf[:, p * LANES:(p + 1) * LANES],
                None, carry_ref[j], diagonal=False, skip_early=(kb == resume)))
            apply_values(j, [a], [v2], carry, first=False)
            return kb - 1, max_ref[j]

        resume = first_block + j - (FIRST_PASS_BLOCKS - 1)
        lax.while_loop(cond, body, (resume, max_ref[j]))


def _resident(shape):
    return pl.BlockSpec(shape, lambda *_: (0,) * len(shape), pipeline_mode=pl.Buffered(1))


def _rows(tm, width):
    return pl.BlockSpec((tm, width), lambda i: (i, 0))


def _slab_spec(shape, steps):
    rows, cols = shape
    slab = next(r for r in range(BF16_SUBLANES, rows + 1, BF16_SUBLANES) if rows % r == 0 and r * steps >= rows)
    last = rows // slab - 1
    return pl.BlockSpec((slab, cols), lambda i: (jnp.minimum(i, last), 0))


def _pre_call(x, n1, wg, wu, wd, nm, win, cw, cb, cn, cast, *, seq):
    n, d = x.shape
    tm = PRE_ROW_TILE
    steps = n // tm
    d_ff = wg.shape[1]
    d_conv = cw.shape[1]
    kern = functools.partial(_pre_kernel, tiles_per_seq=seq // tm, q_scale=1.0 / math.sqrt(SB_HEAD_DIM),
                             n_cast=len(cast))
    cast_specs = [_slab_spec(w.shape, steps) for w in cast]
    hbm = pl.BlockSpec(memory_space=pl.ANY)
    own = (wg, wu, wd, win)
    assert all(w.shape[0] % STAGE_ROWS == 0 for w in own)
    stage_cols = max(w.shape[1] for w in own)
    return pl.pallas_call(
        kern,
        grid=(steps,),
        in_specs=[_rows(tm, d), _resident((1, d)), hbm, hbm, hbm, _resident((1, d)), hbm,
                  _resident(cw.shape), _resident((1, d_conv)), _resident((1, d_conv))] + cast_specs,
        out_specs=[_rows(tm, d), _rows(tm, D_SB), _rows(tm, D_SB), _rows(tm, D_SB), _rows(tm, d_conv)]
                  + cast_specs,
        out_shape=[jax.ShapeDtypeStruct((n, d), F32)] + [jax.ShapeDtypeStruct((n, D_SB), BF16)] * 3
                  + [jax.ShapeDtypeStruct((n, d_conv), BF16)]
                  + [jax.ShapeDtypeStruct(w.shape, BF16) for w in cast],
        scratch_shapes=[pltpu.VMEM((tm + 2 * CONV_HALO, d_conv), F32)]
                       + [pltpu.VMEM(w.shape, BF16) for w in own]
                       + [pltpu.VMEM((STAGE_SLOTS, STAGE_ROWS, stage_cols), F32),
                          pltpu.SemaphoreType.DMA((STAGE_SLOTS,))],
        compiler_params=pltpu.CompilerParams(dimension_semantics=("arbitrary",),
                                             vmem_limit_bytes=VMEM_LIMIT),
        name="ffn1_proj_conv",
    )(x, n1, wg, wu, wd, nm, win, cw, cb, cn, *cast)


def _suffix_sum_matrix(tk):
    j = jnp.arange(tk)[:, None]
    s = jnp.arange(2 * tk)[None, :]
    return -jnp.where(s < tk, j >= s, True).astype(BF16)


def _pair_suffix_matrix(tk):
    j = jnp.arange(2 * tk)[:, None]
    s = jnp.arange(2 * tk)[None, :]
    return -((j // tk == s // tk) & (j % tk >= s % tk)).astype(BF16)


def _mix_post_call(q, k, v, x1, yc, sbn, wo, n2, wg, wu, wd, fn, *, seq, final):
    n, d = x1.shape
    tm = POST_ROW_TILE
    tq = ATT_BLOCK
    n_tiles = n // tm
    tiles_per_seq = seq // tm
    d_ff = wg.shape[1]
    cur = lambda s: (jnp.minimum(s, n_tiles - 1), 0)
    before_cur = lambda s: (jnp.maximum(jnp.minimum(s, n_tiles - 1) - 1, 0), 0)
    prev = lambda s: (jnp.maximum(s - 1, 0), 0)
    kern = functools.partial(_mix_post_kernel, tiles_per_seq=tiles_per_seq, n_tiles=n_tiles, final=final)
    any_space = pl.BlockSpec(memory_space=pl.ANY)
    return pl.pallas_call(
        kern,
        grid=(n_tiles + 1,),
        in_specs=[pl.BlockSpec((tm, D_SB), cur),
                  pl.BlockSpec((tm, D_SB), cur), pl.BlockSpec((tm, D_SB), before_cur),
                  pl.BlockSpec((tm, D_SB), cur), pl.BlockSpec((tm, D_SB), before_cur),
                  any_space, any_space,
                  _resident((tq, 2 * tq)), _resident((2 * tq, 2 * tq)),
                  pl.BlockSpec((tm, d), prev), pl.BlockSpec((tm, yc.shape[1]), prev),
                  _resident((1, D_SB)), any_space, _resident((1, d)),
                  any_space, any_space, any_space, _resident((1, d))],
        out_specs=pl.BlockSpec((tm, d), prev),
        out_shape=jax.ShapeDtypeStruct((n, d), F32),
        scratch_shapes=[pltpu.VMEM((tm // tq, N_PAIRS, 2 * tq, LANES), BF16),
                        pltpu.VMEM((tm // tq, N_PAIRS * 2 * tq, tq), F32),
                        pltpu.VMEM((tm // tq, N_PAIRS, tq, LANES), F32),
                        pltpu.SMEM((tm // tq,), F32),
                        pltpu.VMEM((2, tm, D_SB), BF16),
                        pltpu.VMEM((tq, D_SB), BF16), pltpu.VMEM((tq, D_SB), BF16),
                        pltpu.SemaphoreType.DMA((2,))]
                       + [pltpu.VMEM(w.shape, BF16) for w in (wo, wg, wu, wd)]
                       + [pltpu.SemaphoreType.DMA((4,))],
        compiler_params=pltpu.CompilerParams(dimension_semantics=("arbitrary",),
                                             vmem_limit_bytes=VMEM_LIMIT),
        name="attention_out_proj_ffn2",
    )(q, k, k, v, v, k, v, _suffix_sum_matrix(tq), _pair_suffix_matrix(tq), x1, yc, sbn, wo, n2, wg, wu, wd, fn)


def kernel(x, ffn1_norm, ffn1_w_gate, ffn1_w_up, ffn1_w_down, mix_norm, w_in, conv_w, conv_b,
           sb_out_norm, conv_out_norm, w_out, ffn2_norm, ffn2_w_gate, ffn2_w_up, ffn2_w_down,
           final_norm):
    batch, seq, d = x.shape
    depth = ffn1_norm.shape[0]
    assert seq % POST_ROW_TILE == 0 and seq % PRE_ROW_TILE == 0 and w_in.shape[2] == 6 * D_SB
    row = lambda t: t.reshape(1, -1)
    xs = x.reshape(batch * seq, d)
    for l in range(depth):
        x1, q, k, v, yc, wo, wg2, wu2, wd2 = _pre_call(
            xs, row(ffn1_norm[l]), ffn1_w_gate[l], ffn1_w_up[l], ffn1_w_down[l],
            row(mix_norm[l]), w_in[l], conv_w[l],
            row(conv_b[l]), row(conv_out_norm[l]),
            (w_out[l], ffn2_w_gate[l], ffn2_w_up[l], ffn2_w_down[l]), seq=seq)
        xs = _mix_post_call(
            q, k, v, x1, yc, row(sb_out_norm[l]), wo, row(ffn2_norm[l]), wg2, wu2, wd2,
            row(final_norm), seq=seq, final=(l == depth - 1))
    return xs.reshape(batch, seq, d)
```

```python
import functools
import math

import jax
import jax.numpy as jnp
from jax import lax
from jax.experimental import pallas as pl
from jax.experimental.pallas import tpu as pltpu

F32 = jnp.float32
BF16 = jnp.bfloat16

EPS = 1e-6
SB_HEADS = 8
SB_HEAD_DIM = 64
D_SB = SB_HEADS * SB_HEAD_DIM
CONV_WIDTH = 3
LANES = 128
BF16_SUBLANES = 16
N_PAIRS = D_SB // LANES
PRE_ROW_TILE = 512
POST_ROW_TILE = 512
SUB_TILE = 256
FF_CHUNK = 256
ATT_BLOCK = 128
FIRST_PASS_BLOCKS = 3
STAGE_ROWS = 128
STAGE_SLOTS = 6
CONV_HALO = 8
LOG_WEIGHT_UNDERFLOW = -104.0
V7X_VMEM_BYTES = 64 * 1024 * 1024
VMEM_LIMIT = V7X_VMEM_BYTES - 8 * 1024 * 1024


def _rms(x, g):
    return x * lax.rsqrt(jnp.mean(x * x, axis=-1, keepdims=True) + EPS) * g


def _load_as_bf16(pairs, stage, sem):
    slots, slab = stage.shape[0], stage.shape[1]
    jobs = [(src, dst, r) for src, dst in pairs for r in range(0, src.shape[0], slab)]

    def copy(n):
        src, _, r = jobs[n]
        k = n % slots
        return pltpu.make_async_copy(src.at[r:r + slab, :], stage.at[k, :, 0:src.shape[1]], sem.at[k])

    for n in range(min(slots - 1, len(jobs))):
        copy(n).start()
    for n, (src, dst, r) in enumerate(jobs):
        if n + slots - 1 < len(jobs):
            copy(n + slots - 1).start()
        copy(n).wait()
        dst[r:r + slab, :] = stage[n % slots, :, 0:src.shape[1]].astype(BF16)


def _pre_kernel(x_ref, n1_ref, wg_hbm, wu_hbm, wd_hbm, nm_ref, win_hbm, cw_ref, cb_ref, cn_ref,
                *rest, tiles_per_seq, q_scale, n_cast):
    cast_src = rest[:n_cast]
    x1_ref, q_ref, k_ref, v_ref, yc_ref = rest[n_cast:n_cast + 5]
    cast_dst = rest[n_cast + 5:2 * n_cast + 5]
    xc_buf, wg_ref, wu_ref, wd_ref, win_ref, stage, sem = rest[2 * n_cast + 5:]
    tm = x_ref.shape[0]
    i = pl.program_id(0)

    @pl.when(i == 0)
    def _():
        _load_as_bf16([(wg_hbm, wg_ref), (wu_hbm, wu_ref), (wd_hbm, wd_ref), (win_hbm, win_ref)], stage, sem)

    first = i % tiles_per_seq == 0

    @pl.when(first)
    def _():
        xc_buf[0:CONV_HALO, :] = jnp.zeros((CONV_HALO, xc_buf.shape[1]), F32)

    @pl.when(jnp.logical_not(first))
    def _():
        xc_buf[0:CONV_HALO, :] = xc_buf[tm:tm + CONV_HALO, :]

    d = D_SB
    subs = [slice(r, r + SUB_TILE) for r in range(0, tm, SUB_TILE)]
    xs = [x_ref[r, :] for r in subs]
    hs = [_rms(x, n1_ref[...]).astype(BF16) for x in xs]
    acts = []
    for h in hs:
        parts = []
        for c in range(0, wg_ref.shape[1], FF_CHUNK):
            gate = jnp.dot(h, wg_ref[:, c:c + FF_CHUNK], preferred_element_type=F32)
            up = jnp.dot(h, wu_ref[:, c:c + FF_CHUNK], preferred_element_type=F32)
            parts.append((gate * jax.nn.sigmoid(gate) * up).astype(BF16))
        acts.append(jnp.concatenate(parts, axis=1))
    for src, dst in zip(cast_src, cast_dst):
        dst[...] = src[...].astype(BF16)
    hs = []
    for r, x, act in zip(subs, xs, acts):
        x1 = x + 0.5 * jnp.dot(act, wd_ref[...], preferred_element_type=F32)
        x1_ref[r, :] = x1
        hs.append(_rms(x1, nm_ref[...]).astype(BF16))
    for r, h in zip(subs, hs):
        cu = jnp.dot(h, win_ref[:, 4 * d:6 * d], preferred_element_type=F32)
        xc_buf[CONV_HALO + r.start:CONV_HALO + r.stop, :] = cu[:, 0:d] * cu[:, d:2 * d]
    convs = []
    for r in subs:
        y = cb_ref[...]
        for j in range(CONV_WIDTH):
            off = CONV_HALO - (CONV_WIDTH - 1) + j + r.start
            y = y + xc_buf[off:off + SUB_TILE, :] * cw_ref[j:j + 1, :]
        convs.append(y)
    for r, h, y in zip(subs, hs, convs):
        gate_b = jnp.dot(h, win_ref[:, 3 * d:4 * d], preferred_element_type=F32)
        yc_ref[r, :] = _rms(gate_b * y, cn_ref[...]).astype(BF16)
        proj = jnp.dot(h, win_ref[:, 0:3 * d], preferred_element_type=F32)
        q_ref[r, :] = (proj[:, 0:d] * q_scale).astype(BF16)
        k_ref[r, :] = proj[:, d:2 * d].astype(BF16)
        v_ref[r, :] = proj[:, 2 * d:3 * d].astype(BF16)


def _mix_post_kernel(q_ref, kc_ref, kp_ref, vc_ref, vp_ref, k_hbm, v_hbm, tri_ref, tri2_ref, x1_ref, yc_ref,
                     sbn_ref, wo_hbm, n2_ref, wg_hbm, wu_hbm, wd_hbm, fn_ref, out_ref,
                     qs_ref, carry_ref, acc_ref, max_ref, ysn_ref, kbuf, vbuf, sem,
                     wo_ref, wg_ref, wu_ref, wd_ref, wsem,
                     *, tiles_per_seq, n_tiles, final):
    tm = q_ref.shape[0]
    tq = ATT_BLOCK
    n_q = tm // tq
    rows = N_PAIRS * 2 * tq
    s = pl.program_id(0)
    slot = s % 2
    tile = jnp.minimum(s, n_tiles - 1)
    tile_in_seq = tile % tiles_per_seq
    first_block = tile_in_seq * n_q
    seq_row0 = (tile - tile_in_seq) * tm

    lane = lax.broadcasted_iota(jnp.int32, (tq, LANES), 1)
    low = lane < SB_HEAD_DIM
    row = lax.broadcasted_iota(jnp.int32, (rows, tq), 0)
    col = lax.broadcasted_iota(jnp.int32, (rows, tq), 1)
    strict = col < (row & (tq - 1))
    tri_col = lax.broadcasted_iota(jnp.int32, tri_ref.shape, 1)
    zero = jnp.zeros((), BF16)
    prev_shift = jnp.where(tile_in_seq == 0, 2 * tq, 0)

    early = tq // 2
    head_rows = [p * 2 * tq + h * tq for p in range(N_PAIRS) for h in range(2)]

    def early_of(x):
        return jnp.concatenate([x[r:r + early] for r in head_rows], axis=0)

    def with_early(full, part, pad=False):
        pieces = []
        for n, r in enumerate(head_rows):
            rest = jnp.zeros((tq - early, part.shape[1]), part.dtype) if pad else full[r + early:r + tq]
            pieces += [part[n * early:(n + 1) * early], rest]
        return jnp.concatenate(pieces, axis=0)

    def block_weights(j, k_blk, v_blk, shift, carry, diagonal, only_early=False, skip_early=None):
        if shift is None:
            keep_k = lo_half = hi_half = tri = None
        else:
            k_lane = lane + shift
            keep_k, lo_half = k_lane < LANES, k_lane < SB_HEAD_DIM
            hi_half = (k_lane >= SB_HEAD_DIM) & keep_k
            tri = jnp.where(tri_col + shift < 2 * tq, tri_ref[...], zero)
        if only_early:
            q_stack = lambda p: jnp.concatenate([qs_ref[j, p, 0:early], qs_ref[j, p, tq:tq + early]], axis=0)
        else:
            q_stack = lambda p: qs_ref[j, p]
        z = jnp.concatenate(
            [lax.dot_general(q_stack(p), k_blk(p) if shift is None else jnp.where(keep_k, k_blk(p), zero),
                             (((1,), (1,)), ((), ())), preferred_element_type=F32)
             for p in range(N_PAIRS)], axis=0)
        yield None
        sp = jnp.maximum(z, 0.0) + jnp.log(1.0 + jnp.exp(-jnp.abs(z)))
        if diagonal:
            sp = jnp.where(strict, sp, 0.0)
        if skip_early is not None:
            taking_part = (row & (tq - 1)) + jnp.where(skip_early, 0, tq) >= early
            sp = jnp.where(taking_part, sp, 0.0)
        sums = jnp.dot(sp.astype(BF16), tri_ref[...] if shift is None else tri, preferred_element_type=F32)
        yield None
        logit = z + sums[:, 0:tq]
        if carry is not None:
            logit = logit + carry
        a = jnp.exp(logit)
        if diagonal:
            a = jnp.where(strict, a, 0.0)
        if skip_early is not None:
            a = jnp.where(taking_part, a, 0.0)
        carry = sums[:, tq:2 * tq] if carry is None else carry + sums[:, tq:2 * tq]
        halves = (low, ~low) if shift is None else (lo_half, hi_half)
        v2 = [jnp.concatenate([jnp.where(m, v_blk(p), zero) for m in halves], axis=0)
              for p in range(N_PAIRS)]
        yield a.astype(BF16), carry, v2

    def pair_weights(j, k0, v0, k1, v1, shift1):
        keep = None if shift1 is None else lane + shift1 < LANES
        masked = lambda blk: blk if keep is None else jnp.where(keep, blk, zero)
        zs = [jnp.concatenate(
            [lax.dot_general(qs_ref[j, p], kb(p), (((1,), (1,)), ((), ())), preferred_element_type=F32)
             for p in range(N_PAIRS)], axis=0) for kb in (k0, lambda p: masked(k1(p)))]
        yield None
        sps = [jnp.maximum(z, 0.0) + jnp.log(1.0 + jnp.exp(-jnp.abs(z))) for z in zs]
        sps[0] = jnp.where(strict, sps[0], 0.0)
        if shift1 is not None:
            sps[1] = jnp.where(col + shift1 < tq, sps[1], 0.0)
        sums = jnp.dot(jnp.concatenate([t.astype(BF16) for t in sps], axis=1), tri2_ref[...],
                       preferred_element_type=F32)
        yield None
        total0 = jnp.broadcast_to(sums[:, 0:1], (rows, tq))
        total1 = jnp.broadcast_to(sums[:, tq:tq + 1], (rows, tq))
        a0 = jnp.where(strict, jnp.exp(zs[0] + sums[:, 0:tq]), 0.0).astype(BF16)
        yield None
        a1 = jnp.exp(zs[1] + sums[:, tq:2 * tq] + total0).astype(BF16)
        values = [[jnp.concatenate([jnp.where(m, vb(p), zero) for m in (low, ~low)], axis=0)
                   for p in range(N_PAIRS)] for vb in (v0, lambda p: masked(v1(p)))]
        yield [a0, a1], total0 + total1, values

    def finish(phases):
        out = None
        for out in phases:
            pass
        return out

    def apply_values(j, weights, values, carry, first):
        carry_ref[j] = carry
        heads = []
        for p in range(N_PAIRS):
            r0 = p * 2 * tq
            a2 = jnp.concatenate([w[r:r + tq] for w in weights for r in (r0, r0 + tq)], axis=1)
            v2 = jnp.concatenate([v[p] for v in values], axis=0)
            out = jnp.dot(a2, v2, preferred_element_type=F32)
            heads.append(out if first else acc_ref[j, p] + out)
            acc_ref[j, p] = heads[-1]
        max_ref[j] = jnp.max(carry)
        ysn_ref[slot, j * tq:(j + 1) * tq, :] = _rms(jnp.concatenate(heads, axis=1), sbn_ref[...]).astype(BF16)

    def main_block(dense):
        for j in range(n_q):
            for p in range(N_PAIRS):
                q2 = q_ref[j * tq:(j + 1) * tq, p * LANES:(p + 1) * LANES]
                qs_ref[j, p] = jnp.concatenate([jnp.where(low, q2, zero), jnp.where(low, zero, q2)], axis=0)

        state = {}

        def key_block(j, b):
            if j >= b:
                k_src, v_src, r0, shift = kc_ref, vc_ref, (j - b) * tq, None
            else:
                k_src, v_src, r0, shift = kp_ref, vp_ref, (n_q + j - b) * tq, prev_shift
            return (lambda p: k_src[r0:r0 + tq, p * LANES:(p + 1) * LANES],
                    lambda p: v_src[r0:r0 + tq, p * LANES:(p + 1) * LANES], shift)

        def attention_item(j, last):
            if not last:
                k0, v0, _ = key_block(j, 0)
                k1, v1, shift1 = key_block(j, 1)
                phases = pair_weights(j, k0, v0, k1, v1, shift1)
            else:
                weights, values, carry = state[j]
                k_blk, v_blk, shift = key_block(j, FIRST_PASS_BLOCKS - 1)
                phases = block_weights(j, k_blk, v_blk, shift, early_of(carry), False, only_early=True)
            out = None
            for out in phases:
                if out is None:
                    yield
            if not last:
                state[j] = (out[0], out[2], out[1])
            else:
                a, part, v2 = out
                apply_values(j, weights + [with_early(None, a, pad=True)], values + [v2],
                             with_early(carry, part), first=True)

        items = [(j, last) for j in range(n_q) for last in (False, True)]
        if not dense:
            for item in items:
                finish(attention_item(*item))
            return
        d_ff = wg_ref.shape[1]
        chunks = [slice(c, c + FF_CHUNK) for c in range(0, d_ff, FF_CHUNK)]
        subs = [slice(r, r + SUB_TILE) for r in range(0, tm, SUB_TILE)]
        slots = [(t, c) for t in range(len(subs)) for c in range(len(chunks))]
        visits = [phased for item in items for phased in [attention_item(*item)] for _ in range(3 if item[1] else 4)]
        after_slot = {}
        for n, phased in enumerate(visits):
            after_slot.setdefault(slots[n * len(slots) // len(visits)], []).append(phased)

        x2s, hs = [], []
        for r in subs:
            y = jnp.concatenate([ysn_ref[1 - slot, r, :], yc_ref[r, :]], axis=1)
            x2s.append(x1_ref[r, :] + jnp.dot(y, wo_ref[...], preferred_element_type=F32))
            hs.append(_rms(x2s[-1], n2_ref[...]).astype(BF16))
        acts = [[] for _ in subs]
        for t, h in enumerate(hs):
            for ci, c in enumerate(chunks):
                gate = jnp.dot(h, wg_ref[:, c], preferred_element_type=F32)
                up = jnp.dot(h, wu_ref[:, c], preferred_element_type=F32)
                acts[t].append((gate * jax.nn.sigmoid(gate) * up).astype(BF16))
                for phased in after_slot.get((t, ci), []):
                    next(phased, None)
        for r, x2, act in zip(subs, x2s, acts):
            x3 = x2 + 0.5 * jnp.dot(jnp.concatenate(act, axis=1), wd_ref[...], preferred_element_type=F32)
            out_ref[r, :] = _rms(x3, fn_ref[...]) if final else x3

    weight_copies = [pltpu.make_async_copy(src, dst, wsem.at[n]) for n, (src, dst) in enumerate(
        ((wo_hbm, wo_ref), (wg_hbm, wg_ref), (wu_hbm, wu_ref), (wd_hbm, wd_ref)))]

    @pl.when(s == 0)
    def _():
        for cp in weight_copies:
            cp.start()
        main_block(False)
        for cp in weight_copies:
            cp.wait()

    pl.when(s > 0)(lambda: main_block(True))

    for j in range(n_q):
        def cond(state):
            kb, m = state
            return jnp.logical_and(kb >= 0, m > LOG_WEIGHT_UNDERFLOW)

        def body(state, j=j):
            kb, _ = state
            r0 = pl.multiple_of(seq_row0 + kb * tq, tq)
            copies = [pltpu.make_async_copy(src.at[pl.ds(r0, tq), :], dst, sem.at[n])
                      for n, (src, dst) in enumerate(((k_hbm, kbuf), (v_hbm, vbuf)))]
            for cp in copies:
                cp.start()
            for cp in copies:
                cp.wait()
            a, carry, v2 = finish(block_weights(
                j, lambda p: kbuf[:, p * LANES:(p + 1) * LANES], lambda p: vbuf[:, p * LANES:(p + 1) * LANES],
                None, carry_ref[j], diagonal=False, skip_early=(kb == resume)))
            apply_values(j, [a], [v2], carry, first=False)
            return kb - 1, max_ref[j]

        resume = first_block + j - (FIRST_PASS_BLOCKS - 1)
        lax.while_loop(cond, body, (resume, max_ref[j]))


def _resident(shape):
    return pl.BlockSpec(shape, lambda *_: (0,) * len(shape), pipeline_mode=pl.Buffered(1))


def _rows(tm, width):
    return pl.BlockSpec((tm, width), lambda i: (i, 0))


def _slab_spec(shape, steps):
    rows, cols = shape
    slab = next(r for r in range(BF16_SUBLANES, rows + 1, BF16_SUBLANES) if rows % r == 0 and r * steps >= rows)
    last = rows // slab - 1
    return pl.BlockSpec((slab, cols), lambda i: (jnp.minimum(i, last), 0))


def _pre_call(x, n1, wg, wu, wd, nm, win, cw, cb, cn, cast, *, seq):
    n, d = x.shape
    tm = PRE_ROW_TILE
    steps = n // tm
    d_ff = wg.shape[1]
    d_conv = cw.shape[1]
    kern = functools.partial(_pre_kernel, tiles_per_seq=seq // tm, q_scale=1.0 / math.sqrt(SB_HEAD_DIM),
                             n_cast=len(cast))
    cast_specs = [_slab_spec(w.shape, steps) for w in cast]
    hbm = pl.BlockSpec(memory_space=pl.ANY)
    own = (wg, wu, wd, win)
    assert all(w.shape[0] % STAGE_ROWS == 0 for w in own)
    stage_cols = max(w.shape[1] for w in own)
    return pl.pallas_call(
        kern,
        grid=(steps,),
        in_specs=[_rows(tm, d), _resident((1, d)), hbm, hbm, hbm, _resident((1, d)), hbm,
                  _resident(cw.shape), _resident((1, d_conv)), _resident((1, d_conv))] + cast_specs,
        out_specs=[_rows(tm, d), _rows(tm, D_SB), _rows(tm, D_SB), _rows(tm, D_SB), _rows(tm, d_conv)]
                  + cast_specs,
        out_shape=[jax.ShapeDtypeStruct((n, d), F32)] + [jax.ShapeDtypeStruct((n, D_SB), BF16)] * 3
                  + [jax.ShapeDtypeStruct((n, d_conv), BF16)]
                  + [jax.ShapeDtypeStruct(w.shape, BF16) for w in cast],
        scratch_shapes=[pltpu.VMEM((tm + 2 * CONV_HALO, d_conv), F32)]
                       + [pltpu.VMEM(w.shape, BF16) for w in own]
                       + [pltpu.VMEM((STAGE_SLOTS, STAGE_ROWS, stage_cols), F32),
                          pltpu.SemaphoreType.DMA((STAGE_SLOTS,))],
        compiler_params=pltpu.CompilerParams(dimension_semantics=("arbitrary",),
                                             vmem_limit_bytes=VMEM_LIMIT),
        name="ffn1_proj_conv",
    )(x, n1, wg, wu, wd, nm, win, cw, cb, cn, *cast)


def _suffix_sum_matrix(tk):
    j = jnp.arange(tk)[:, None]
    s = jnp.arange(2 * tk)[None, :]
    return -jnp.where(s < tk, j >= s, True).astype(BF16)


def _pair_suffix_matrix(tk):
    j = jnp.arange(2 * tk)[:, None]
    s = jnp.arange(2 * tk)[None, :]
    return -((j // tk == s // tk) & (j % tk >= s % tk)).astype(BF16)


def _mix_post_call(q, k, v, x1, yc, sbn, wo, n2, wg, wu, wd, fn, *, seq, final):
    n, d = x1.shape
    tm = POST_ROW_TILE
    tq = ATT_BLOCK
    n_tiles = n // tm
    tiles_per_seq = seq // tm
    d_ff = wg.shape[1]
    cur = lambda s: (jnp.minimum(s, n_tiles - 1), 0)
    before_cur = lambda s: (jnp.maximum(jnp.minimum(s, n_tiles - 1) - 1, 0), 0)
    prev = lambda s: (jnp.maximum(s - 1, 0), 0)
    kern = functools.partial(_mix_post_kernel, tiles_per_seq=tiles_per_seq, n_tiles=n_tiles, final=final)
    any_space = pl.BlockSpec(memory_space=pl.ANY)
    return pl.pallas_call(
        kern,
        grid=(n_tiles + 1,),
        in_specs=[pl.BlockSpec((tm, D_SB), cur),
                  pl.BlockSpec((tm, D_SB), cur), pl.BlockSpec((tm, D_SB), before_cur),
                  pl.BlockSpec((tm, D_SB), cur), pl.BlockSpec((tm, D_SB), before_cur),
                  any_space, any_space,
                  _resident((tq, 2 * tq)), _resident((2 * tq, 2 * tq)),
                  pl.BlockSpec((tm, d), prev), pl.BlockSpec((tm, yc.shape[1]), prev),
                  _resident((1, D_SB)), any_space, _resident((1, d)),
                  any_space, any_space, any_space, _resident((1, d))],
        out_specs=pl.BlockSpec((tm, d), prev),
        out_shape=jax.ShapeDtypeStruct((n, d), F32),
        scratch_shapes=[pltpu.VMEM((tm // tq, N_PAIRS, 2 * tq, LANES), BF16),
                        pltpu.VMEM((tm // tq, N_PAIRS * 2 * tq, tq), F32),
                        pltpu.VMEM((tm // tq, N_PAIRS, tq, LANES), F32),
                        pltpu.SMEM((tm // tq,), F32),
                        pltpu.VMEM((2, tm, D_SB), BF16),
                        pltpu.VMEM((tq, D_SB), BF16), pltpu.VMEM((tq, D_SB), BF16),
                        pltpu.SemaphoreType.DMA((2,))]
                       + [pltpu.VMEM(w.shape, BF16) for w in (wo, wg, wu, wd)]
                       + [pltpu.SemaphoreType.DMA((4,))],
        compiler_params=pltpu.CompilerParams(dimension_semantics=("arbitrary",),
                                             vmem_limit_bytes=VMEM_LIMIT),
        name="attention_out_proj_ffn2",
    )(q, k, k, v, v, k, v, _suffix_sum_matrix(tq), _pair_suffix_matrix(tq), x1, yc, sbn, wo, n2, wg, wu, wd, fn)


def kernel(x, ffn1_norm, ffn1_w_gate, ffn1_w_up, ffn1_w_down, mix_norm, w_in, conv_w, conv_b,
           sb_out_norm, conv_out_norm, w_out, ffn2_norm, ffn2_w_gate, ffn2_w_up, ffn2_w_down,
           final_norm):
    batch, seq, d = x.shape
    depth = ffn1_norm.shape[0]
    assert seq % POST_ROW_TILE == 0 and seq % PRE_ROW_TILE == 0 and w_in.shape[2] == 6 * D_SB
    row = lambda t: t.reshape(1, -1)
    xs = x.reshape(batch * seq, d)
    for l in range(depth):
        x1, q, k, v, yc, wo, wg2, wu2, wd2 = _pre_call(
            xs, row(ffn1_norm[l]), ffn1_w_gate[l], ffn1_w_up[l], ffn1_w_down[l],
            row(mix_norm[l]), w_in[l], conv_w[l],
            row(conv_b[l]), row(conv_out_norm[l]),
            (w_out[l], ffn2_w_gate[l], ffn2_w_up[l], ffn2_w_down[l]), seq=seq)
        xs = _mix_post_call(
            q, k, v, x1, yc, row(sb_out_norm[l]), wo, row(ffn2_norm[l]), wg2, wu2, wd2,
            row(final_norm), seq=seq, final=(l == depth - 1))
    return xs.reshape(batch, seq, d)
```
